```python
import jax, jax.numpy as jnp
from jax import lax
import numpy as np


D_MODEL = 2048
BATCH = 4
SEQ = 8192
DEPTH = 4
DEC_BATCH = 16
DEC_SEQ = 16
PAST_LEN = 2048

CHUNK = 64
LIN_BLOCK = CHUNK // 4
EPS = 1e-6

GLA_HEADS = 4
GLA_DK = 96
GLA_DV = 192
GLA_WIDTH = GLA_HEADS * GLA_DV
GLA_GATE_RANK = 16
GLA_GATE_NORMALIZER = 16.0
GLA_COLS = 2 * GLA_HEADS * GLA_DK + GLA_WIDTH + GLA_GATE_RANK + GLA_WIDTH

RWKV_HEADS = 10
RWKV_HD = 64
RWKV_WIDTH = RWKV_HEADS * RWKV_HD
RWKV_DECAY_LORA = 64
RWKV_AAA_LORA = 64
RWKV_GATE_LORA = 128
RWKV_GN_EPS = 64e-5
RWKV_COLS = 3 * RWKV_WIDTH + RWKV_DECAY_LORA + RWKV_AAA_LORA + RWKV_GATE_LORA

HGRN_HEADS = 5
HGRN_EXPAND = 128
HGRN_HD = 128
HGRN_WIDTH = HGRN_HEADS * HGRN_HD
HGRN_COLS = 4 * HGRN_WIDTH

MIX_WIDTH = GLA_WIDTH + RWKV_WIDTH + HGRN_WIDTH
IN_COLS = GLA_COLS + RWKV_COLS + HGRN_COLS
D_FF = -(-8 * D_MODEL // (3 * 256)) * 256

kernel_name = 'hybrid_gla_rwkv7_hgrn2_stream_step'


def rms_norm(x, g):
    xf = x.astype(jnp.float32)
    y = xf * lax.rsqrt(jnp.mean(xf * xf, axis=-1, keepdims=True) + EPS)
    return (y * g.astype(jnp.float32)).astype(x.dtype)


def split_cols(a, sizes):
    bounds = [int(s) for s in np.cumsum(sizes)[:-1]]
    return jnp.split(a, bounds, axis=-1)


def heads(a, n):
    return a.reshape(a.shape[0], a.shape[1], n, a.shape[-1] // n)


def gated_linear_recurrence(q, k, v, g, s0):
    B, T, H, _ = q.shape
    dv = v.shape[-1]
    L = LIN_BLOCK
    pad = (-T) % L

    def prep(a):
        a = jnp.pad(a.astype(jnp.float32), ((0, 0), (0, pad), (0, 0), (0, 0)))
        n = a.shape[1] // L
        return a.reshape(B, n, L, H, a.shape[-1]).transpose(1, 0, 3, 2, 4)

    causal = jnp.tril(jnp.ones((L, L), dtype=bool))[:, :, None]

    def step(S, blk):
        qc, kc, vc, gc = blk
        b = jnp.cumsum(gc, axis=2)
        diff = b[:, :, :, None, :] - b[:, :, None, :, :]
        decay = jnp.where(causal, jnp.exp(jnp.where(causal, diff, 0.0)), 0.0)
        attn = jnp.einsum('bhtd,bhsd,bhtsd->bhts', qc, kc, decay)
        o = (jnp.einsum('bhts,bhsv->bhtv', attn, vc)
             + jnp.einsum('bhtd,bhdv->bhtv', qc * jnp.exp(b), S))
        b_last = b[:, :, -1:, :]
        S = (jnp.exp(b_last[:, :, 0, :])[..., None] * S
             + jnp.einsum('bhsd,bhsv->bhdv', kc * jnp.exp(b_last - b), vc))
        return S, o

    S, ob = lax.scan(step, s0.astype(jnp.float32), (prep(q), prep(k), prep(v), prep(g)))
    o = ob.transpose(1, 0, 3, 2, 4).reshape(B, -1, H, dv)[:, :T]
    return o, S


def rwkv7_recurrence(r, w, k, v, kk, b, s0):
    def step(S, inp):
        rt, wt, kt, vt, kkt, bt = inp
        sa = jnp.einsum('bhvk,bhk->bhv', S, kkt)
        S = S * wt[:, :, None, :] - sa[..., None] * bt[:, :, None, :] + vt[..., None] * kt[:, :, None, :]
        return S, jnp.einsum('bhvk,bhk->bhv', S, rt)

    xs = tuple(a.astype(jnp.float32).transpose(1, 0, 2, 3) for a in (r, w, k, v, kk, b))
    S, o = lax.scan(step, s0.astype(jnp.float32), xs)
    return o.transpose(1, 0, 2, 3), S


def hybrid_mixer(h, s_gla, s_rwkv, s_shift, s_hgrn, lb,
                 w_in, gla_gate_up, gla_gate_bias, gla_norm,
                 rwkv_mu, rwkv_w0, rwkv_w_up, rwkv_a0, rwkv_a_up, rwkv_g_up,
                 rwkv_k_k, rwkv_k_a, rwkv_r_k, rwkv_ln_w, rwkv_ln_b,
                 hgrn_norm, w_out):
    f32 = jnp.float32
    B, T, _ = h.shape
    proj = jnp.einsum('btd,dc->btc', h, w_in)
    p_gla, p_rwkv, p_hgrn = split_cols(proj, [GLA_COLS, RWKV_COLS, HGRN_COLS])

    gq, gk, gv, g_down, g_out = split_cols(
        p_gla, [GLA_HEADS * GLA_DK, GLA_HEADS * GLA_DK, GLA_WIDTH, GLA_GATE_RANK, GLA_WIDTH])
    g_log = jax.nn.log_sigmoid((g_down @ gla_gate_up + gla_gate_bias).astype(f32)) / GLA_GATE_NORMALIZER
    o_a, s_gla_new = gated_linear_recurrence(
        heads(gq, GLA_HEADS).astype(f32) * (GLA_DK ** -0.5), heads(gk, GLA_HEADS),
        heads(gv, GLA_HEADS), heads(g_log, GLA_HEADS), s_gla)
    o_a = rms_norm(o_a, gla_norm) * jax.nn.silu(heads(g_out, GLA_HEADS).astype(f32))

    prev = jnp.concatenate([s_shift[:, None, :].astype(p_rwkv.dtype), p_rwkv[:, :-1]], axis=1)
    p_mix = (p_rwkv + (prev - p_rwkv) * rwkv_mu).astype(f32)
    s_shift_new = p_rwkv[:, -1]
    r_, w_down, k_, v_, a_down, gate_down = split_cols(
        p_mix, [RWKV_WIDTH, RWKV_DECAY_LORA, RWKV_WIDTH, RWKV_WIDTH, RWKV_AAA_LORA, RWKV_GATE_LORA])
    w_log = -jax.nn.softplus(-(rwkv_w0 + jnp.tanh(w_down) @ rwkv_w_up)) - 0.5
    decay = jnp.exp(-jnp.exp(w_log))
    a = jax.nn.sigmoid(rwkv_a0 + a_down @ rwkv_a_up)
    gate = jax.nn.sigmoid(gate_down) @ rwkv_g_up
    kk = heads(k_ * rwkv_k_k, RWKV_HEADS)
    kk = kk * lax.rsqrt(jnp.maximum(jnp.sum(kk * kk, axis=-1, keepdims=True), 1e-24))
    k_ = k_ * (1.0 + (a - 1.0) * rwkv_k_a)
    r_h, k_h, v_h, a_h = (heads(t, RWKV_HEADS) for t in (r_, k_, v_, a))
    o_b, s_rwkv_new = rwkv7_recurrence(r_h, heads(decay, RWKV_HEADS), k_h, v_h, kk, kk * a_h, s_rwkv)
    mu = jnp.mean(o_b, axis=-1, keepdims=True)
    var = jnp.mean(jnp.square(o_b - mu), axis=-1, keepdims=True)
    o_b = ((o_b - mu) * lax.rsqrt(var + RWKV_GN_EPS)).reshape(B, T, RWKV_WIDTH) * rwkv_ln_w + rwkv_ln_b
    bonus = jnp.sum(r_h * k_h * rwkv_r_k, axis=-1, keepdims=True) * v_h
    o_b = (o_b + bonus.reshape(B, T, RWKV_WIDTH)) * gate

    hq, hf, hi, hg = split_cols(p_hgrn.astype(f32), [HGRN_WIDTH] * 4)
    lb = lb.astype(f32)
    f_gate = lb + (1.0 - lb) * jax.nn.sigmoid(hf)
    log_f = jnp.log(f_gate)
    k_c = (1.0 - lb) * jax.nn.sigmoid(-hf)
    o_c, s_hgrn_new = gated_linear_recurrence(
        heads(jax.nn.silu(hq), HGRN_HEADS), heads(k_c, HGRN_HEADS),
        heads(hi, HGRN_HEADS), heads(log_f, HGRN_HEADS), s_hgrn)
    o_c = rms_norm(o_c, hgrn_norm) * jax.nn.silu(heads(hg, HGRN_HEADS))

    o = jnp.concatenate([o_a.reshape(B, T, GLA_WIDTH), o_b, o_c.reshape(B, T, HGRN_WIDTH)],
                        axis=-1).astype(h.dtype)
    out = jnp.einsum('btc,cd->btd', o, w_out)
    return (out, s_gla_new.astype(s_gla.dtype), s_rwkv_new.astype(s_rwkv.dtype),
            s_shift_new.astype(s_shift.dtype), s_hgrn_new.astype(s_hgrn.dtype))


def swiglu_ffn(h, w_gate, w_up, w_down):
    return (jax.nn.silu(h @ w_gate) * (h @ w_up)) @ w_down


def trunk(x, s_gla, s_rwkv, s_shift, s_hgrn, lbs,
          norm_mix_pre, norm_mix_post, norm_ffn_pre, norm_ffn_post, w_in,
          gla_gate_up, gla_gate_bias, gla_norm,
          rwkv_mu, rwkv_w0, rwkv_w_up, rwkv_a0, rwkv_a_up, rwkv_g_up,
          rwkv_k_k, rwkv_k_a, rwkv_r_k, rwkv_ln_w, rwkv_ln_b,
          hgrn_norm, w_out, ffn_w_gate, ffn_w_up, ffn_w_down):
    new_gla, new_rwkv, new_shift, new_hgrn = [], [], [], []
    for l in range(DEPTH):
        h = rms_norm(x, norm_mix_pre[l])
        m, g1, r1, sh1, h1 = hybrid_mixer(
            h, s_gla[l], s_rwkv[l], s_shift[l], s_hgrn[l], lbs[l],
            w_in[l], gla_gate_up[l], gla_gate_bias[l], gla_norm[l],
            rwkv_mu[l], rwkv_w0[l], rwkv_w_up[l], rwkv_a0[l], rwkv_a_up[l], rwkv_g_up[l],
            rwkv_k_k[l], rwkv_k_a[l], rwkv_r_k[l], rwkv_ln_w[l], rwkv_ln_b[l],
            hgrn_norm[l], w_out[l])
        x = x + rms_norm(m, norm_mix_post[l])
        h = rms_norm(x, norm_ffn_pre[l])
        x = x + rms_norm(swiglu_ffn(h, ffn_w_gate[l], ffn_w_up[l], ffn_w_down[l]), norm_ffn_post[l])
        new_gla.append(g1)
        new_rwkv.append(r1)
        new_shift.append(sh1)
        new_hgrn.append(h1)
    return x, jnp.stack(new_gla), jnp.stack(new_rwkv), jnp.stack(new_shift), jnp.stack(new_hgrn)


def setup_inputs(seed: int = 0) -> dict:
    key = jax.random.key(seed)
    ks = iter(jax.random.split(key, 40))

    def nrm(shape, scale):
        return scale * jax.random.normal(next(ks), shape, jnp.float32)

    def uni(shape):
        return jax.random.uniform(next(ks), shape, jnp.float32)

    return {
        'x_prompt': nrm((BATCH, SEQ, D_MODEL), 1.0),
        'x_sample': nrm((DEC_BATCH, DEC_SEQ, D_MODEL), 1.0),
        'state_gla': nrm((DEPTH, DEC_BATCH, GLA_HEADS, GLA_DK, GLA_DV), 0.3),
        'state_rwkv': nrm((DEPTH, DEC_BATCH, RWKV_HEADS, RWKV_HD, RWKV_HD), 0.3),
        'state_rwkv_shift': nrm((DEPTH, DEC_BATCH, RWKV_COLS), 1.0),
        'state_hgrn': nrm((DEPTH, DEC_BATCH, HGRN_HEADS, HGRN_EXPAND, HGRN_HD), 0.5),
        'norm_mix_pre': 1.0 + nrm((DEPTH, D_MODEL), 0.05),
        'norm_mix_post': 1.0 + nrm((DEPTH, D_MODEL), 0.05),
        'norm_ffn_pre': 1.0 + nrm((DEPTH, D_MODEL), 0.05),
        'norm_ffn_post': 1.0 + nrm((DEPTH, D_MODEL), 0.05),
        'w_in': nrm((DEPTH, D_MODEL, IN_COLS), D_MODEL ** -0.5),
        'gla_gate_up': nrm((DEPTH, GLA_GATE_RANK, GLA_HEADS * GLA_DK), GLA_GATE_RANK ** -0.5),
        'gla_gate_bias': nrm((DEPTH, GLA_HEADS * GLA_DK), 0.1),
        'gla_norm': 1.0 + nrm((DEPTH, GLA_DV), 0.05),
        'rwkv_mu': uni((DEPTH, RWKV_COLS)),
        'rwkv_w0': nrm((DEPTH, RWKV_WIDTH), 0.5),
        'rwkv_w_up': nrm((DEPTH, RWKV_DECAY_LORA, RWKV_WIDTH), RWKV_DECAY_LORA ** -0.5),
        'rwkv_a0': nrm((DEPTH, RWKV_WIDTH), 0.1),
        'rwkv_a_up': nrm((DEPTH, RWKV_AAA_LORA, RWKV_WIDTH), RWKV_AAA_LORA ** -0.5),
        'rwkv_g_up': nrm((DEPTH, RWKV_GATE_LORA, RWKV_WIDTH), RWKV_GATE_LORA ** -0.5),
        'rwkv_k_k': 0.85 + nrm((DEPTH, RWKV_WIDTH), 0.05),
        'rwkv_k_a': 1.0 + nrm((DEPTH, RWKV_WIDTH), 0.05),
        'rwkv_r_k': nrm((DEPTH, RWKV_HEADS, RWKV_HD), 0.1),
        'rwkv_ln_w': 1.0 + nrm((DEPTH, RWKV_WIDTH), 0.05),
        'rwkv_ln_b': nrm((DEPTH, RWKV_WIDTH), 0.02),
        'hgrn_lb_logits': nrm((DEPTH, HGRN_WIDTH), 0.1),
        'hgrn_norm': 1.0 + nrm((DEPTH, HGRN_HD), 0.05),
        'w_out': nrm((DEPTH, MIX_WIDTH, D_MODEL), MIX_WIDTH ** -0.5),
        'ffn_w_gate': nrm((DEPTH, D_MODEL, D_FF), D_MODEL ** -0.5),
        'ffn_w_up': nrm((DEPTH, D_MODEL, D_FF), D_MODEL ** -0.5),
        'ffn_w_down': nrm((DEPTH, D_FF, D_MODEL), D_FF ** -0.5),
    }


def reference(x_prompt, x_sample, state_gla, state_rwkv, state_rwkv_shift, state_hgrn,
              norm_mix_pre, norm_mix_post, norm_ffn_pre, norm_ffn_post, w_in,
              gla_gate_up, gla_gate_bias, gla_norm,
              rwkv_mu, rwkv_w0, rwkv_w_up, rwkv_a0, rwkv_a_up, rwkv_g_up,
              rwkv_k_k, rwkv_k_a, rwkv_r_k, rwkv_ln_w, rwkv_ln_b,
              hgrn_lb_logits, hgrn_norm, w_out, ffn_w_gate, ffn_w_up, ffn_w_down):
    p = jax.nn.softmax(hgrn_lb_logits.astype(jnp.float32), axis=0)
    lbs = jnp.cumsum(p, axis=0) - p[0:1]
    weights = (norm_mix_pre, norm_mix_post, norm_ffn_pre, norm_ffn_post, w_in,
               gla_gate_up, gla_gate_bias, gla_norm,
               rwkv_mu, rwkv_w0, rwkv_w_up, rwkv_a0, rwkv_a_up, rwkv_g_up,
               rwkv_k_k, rwkv_k_a, rwkv_r_k, rwkv_ln_w, rwkv_ln_b,
               hgrn_norm, w_out, ffn_w_gate, ffn_w_up, ffn_w_down)

    bp = x_prompt.shape[0]
    dt = x_prompt.dtype
    z_gla = jnp.zeros((DEPTH, bp) + state_gla.shape[2:], dt)
    z_rwkv = jnp.zeros((DEPTH, bp) + state_rwkv.shape[2:], dt)
    z_shift = jnp.zeros((DEPTH, bp) + state_rwkv_shift.shape[2:], dt)
    z_hgrn = jnp.zeros((DEPTH, bp) + state_hgrn.shape[2:], dt)
    y_prompt, gla_p, rwkv_p, shift_p, hgrn_p = trunk(
        x_prompt, z_gla, z_rwkv, z_shift, z_hgrn, lbs, *weights)

    y_sample, gla_s, rwkv_s, shift_s, hgrn_s = trunk(
        x_sample, state_gla, state_rwkv, state_rwkv_shift, state_hgrn, lbs, *weights)

    return (y_prompt, y_sample, gla_p, rwkv_p, shift_p, hgrn_p, gla_s, rwkv_s, shift_s, hgrn_s)
```

```python
import functools

import jax
import jax.numpy as jnp
import numpy as np
from jax import lax
from jax.experimental import pallas as pl
from jax.experimental.pallas import tpu as pltpu

F32 = jnp.float32
BF16 = jnp.bfloat16

D_MODEL = 2048
DEPTH = 4
EPS = 1e-6

GLA_HEADS, GLA_DK, GLA_DV = 4, 96, 192
GLA_DKP, GLA_DVP = 128, 256
GLA_RANK, GLA_RANKP = 16, 128
GLA_GATE_NORMALIZER = 16.0
GLA_QW = GLA_HEADS * GLA_DKP
GLA_VW = GLA_HEADS * GLA_DVP
GLA_COLS_P = 2 * GLA_QW + 2 * GLA_VW + GLA_RANKP

RWKV_HEADS, RWKV_HD = 10, 64
RWKV_WIDTH = RWKV_HEADS * RWKV_HD
RWKV_PAIRS = RWKV_HEADS // 2
RWKV_COLS = 2176
RWKV_GN_EPS = 64e-5

HGRN_HEADS, HGRN_D = 5, 128
HGRN_WIDTH = HGRN_HEADS * HGRN_D
HGRN_COLS = 4 * HGRN_WIDTH

D_FF = 5632
LANE = 128
DIAG = 8

VMEM_LIMIT = 56 * 1024 * 1024


def _sigmoid(x):
    return 1.0 / (1.0 + jnp.exp(-x))


def _silu(x):
    return x * _sigmoid(x)


def _log_sigmoid(x):
    return jnp.minimum(x, 0.0) - jnp.log(1.0 + jnp.exp(-jnp.abs(x)))


def _dot(a, b):
    return jnp.dot(a.astype(BF16), b.astype(BF16), preferred_element_type=F32)


def _dot_nt(a, b):
    return lax.dot_general(a.astype(BF16), b.astype(BF16), (((1,), (1,)), ((), ())),
                           preferred_element_type=F32)


def _dot_tn(a, b):
    return lax.dot_general(a.astype(BF16), b.astype(BF16), (((0,), (0,)), ((), ())),
                           preferred_element_type=F32)


def _split_bf16(x, parts):
    out = []
    r = x
    for i in range(parts):
        p = r.astype(BF16)
        out.append(p)
        if i + 1 < parts:
            r = r - p.astype(F32)
    return out


def _dot_exact_lhs(m_bf16, x, parts=3):
    acc = None
    for p in _split_bf16(x, parts):
        t = jnp.dot(m_bf16, p, preferred_element_type=F32)
        acc = t if acc is None else acc + t
    return acc


def _dot_exact_rhs(x, m_bf16, parts=2):
    acc = None
    for p in _split_bf16(x, parts):
        t = jnp.dot(p, m_bf16, preferred_element_type=F32)
        acc = t if acc is None else acc + t
    return acc


def _iota(shape, dim):
    return lax.broadcasted_iota(jnp.int32, shape, dim)


def _tril_ones(c):
    return jnp.where(_iota((c, c), 0) >= _iota((c, c), 1), 1.0, 0.0).astype(BF16)


def _rms_rows(x, g):
    ms = jnp.mean(x * x, axis=-1, keepdims=True)
    return x * lax.rsqrt(ms + EPS) * g


def _compiler_params(sem):
    return pltpu.CompilerParams(dimension_semantics=sem, vmem_limit_bytes=VMEM_LIMIT)


def _norm_matmul_kernel(x_ref, g_ref, w_ref, o_ref, h_scr):
    @pl.when(pl.program_id(1) == 0)
    def _():
        h_scr[...] = _rms_rows(x_ref[...], g_ref[...]).astype(BF16)

    o_ref[...] = jnp.dot(h_scr[...], w_ref[...], preferred_element_type=F32)


def _norm_matmul(x, g, w, tm, tn):
    n, d = x.shape
    cout = w.shape[1]
    return pl.pallas_call(
        _norm_matmul_kernel,
        grid=(n // tm, cout // tn),
        in_specs=[pl.BlockSpec((tm, d), lambda i, j: (i, 0)),
                  pl.BlockSpec((1, d), lambda i, j: (0, 0)),
                  pl.BlockSpec((d, tn), lambda i, j: (0, j))],
        out_specs=pl.BlockSpec((tm, tn), lambda i, j: (i, j)),
        out_shape=jax.ShapeDtypeStruct((n, cout), F32),
        scratch_shapes=[pltpu.VMEM((tm, d), BF16)],
        compiler_params=_compiler_params(("arbitrary", "arbitrary")),
    )(x, g, w)


def _out_proj_kernel(oa_ref, ob_ref, oc_ref, wa_ref, wb_ref, wc_ref, x_ref, g_ref, y_ref):
    m = (jnp.dot(oa_ref[...], wa_ref[...], preferred_element_type=F32)
         + jnp.dot(ob_ref[...], wb_ref[...], preferred_element_type=F32)
         + jnp.dot(oc_ref[...], wc_ref[...], preferred_element_type=F32))
    y_ref[...] = x_ref[...] + _rms_rows(m, g_ref[...])


def _out_proj(oa, ob, oc, wa, wb, wc, x, g, tm):
    n, d = x.shape
    row = lambda w: pl.BlockSpec((tm, w), lambda i: (i, 0))
    full = lambda a: pl.BlockSpec(a.shape, lambda i: (0, 0))
    return pl.pallas_call(
        _out_proj_kernel,
        grid=(n // tm,),
        in_specs=[row(oa.shape[1]), row(ob.shape[1]), row(oc.shape[1]),
                  full(wa), full(wb), full(wc), row(d), full(g)],
        out_specs=row(d),
        out_shape=jax.ShapeDtypeStruct((n, d), F32),
        compiler_params=_compiler_params(("arbitrary",)),
    )(oa, ob, oc, wa, wb, wc, x, g)


def _ffn_kernel(x_ref, gpre_ref, wg_ref, wu_ref, wd_ref, gpost_ref, y_ref, h_scr, acc_scr):
    j = pl.program_id(1)

    @pl.when(j == 0)
    def _():
        h_scr[...] = _rms_rows(x_ref[...], gpre_ref[...]).astype(BF16)
        acc_scr[...] = jnp.zeros_like(acc_scr)

    h = h_scr[...]
    a = jnp.dot(h, wg_ref[...], preferred_element_type=F32)
    u = jnp.dot(h, wu_ref[...], preferred_element_type=F32)
    acc_scr[...] += jnp.dot((_silu(a) * u).astype(BF16), wd_ref[...], preferred_element_type=F32)

    @pl.when(j == pl.num_programs(1) - 1)
    def _():
        y_ref[...] = x_ref[...] + _rms_rows(acc_scr[...], gpost_ref[...])


def _ffn(x, gpre, wg, wu, wd, gpost, tm, tf):
    n, d = x.shape
    dff = wg.shape[1]
    return pl.pallas_call(
        _ffn_kernel,
        grid=(n // tm, dff // tf),
        in_specs=[pl.BlockSpec((tm, d), lambda i, j: (i, 0)),
                  pl.BlockSpec((1, d), lambda i, j: (0, 0)),
                  pl.BlockSpec((d, tf), lambda i, j: (0, j)),
                  pl.BlockSpec((d, tf), lambda i, j: (0, j)),
                  pl.BlockSpec((tf, d), lambda i, j: (j, 0)),
                  pl.BlockSpec((1, d), lambda i, j: (0, 0))],
        out_specs=pl.BlockSpec((tm, d), lambda i, j: (i, 0)),
        out_shape=jax.ShapeDtypeStruct((n, d), F32),
        scratch_shapes=[pltpu.VMEM((tm, d), BF16), pltpu.VMEM((tm, d), F32)],
        compiler_params=_compiler_params(("arbitrary", "arbitrary")),
    )(x, gpre, wg, wu, wd, gpost)


class _GlrMasks:
    def __init__(self, c, dk):
        self.c = c
        row_k = _iota((c, dk), 0)
        row_a = _iota((c, c), 0)
        col_a = _iota((c, c), 1)
        self.tril = _tril_ones(c)
        self.levels = []
        h = DIAG
        while 2 * h <= c:
            is_q = (row_k // h) % 2 == 1
            same = (row_a // (2 * h)) == (col_a // (2 * h))
            self.levels.append((h, is_q, same))
            h *= 2
        self.diag = []
        for d in range(min(DIAG, c)):
            self.diag.append((col_a == row_a - d) & ((row_a % DIAG) >= d))


def _glr_chunk(q, k, v, g, st, mk):
    c, dk = q.shape
    b = _dot_exact_lhs(mk.tril, g)
    att = jnp.zeros((c, c), F32)
    for h, is_q, same in mk.levels:
        n = c // (2 * h)
        ref = b.reshape(n, 2 * h, dk)[:, h - 1:h, :]
        refb = jnp.broadcast_to(ref, (n, 2 * h, dk)).reshape(c, dk)
        e = jnp.exp(jnp.where(is_q, b - refb, refb - b))
        qh = jnp.where(is_q, q * e, 0.0)
        kh = jnp.where(is_q, 0.0, k * e)
        att = att + jnp.where(same, _dot_nt(qh, kh), 0.0)
    for d, valid in enumerate(mk.diag):
        if d == 0:
            p = q * k
        else:
            ks = pltpu.roll(k, d, 0)
            bs = pltpu.roll(b, d, 0)
            p = q * ks * jnp.exp(jnp.minimum(b - bs, 0.0))
        att = jnp.where(valid, jnp.sum(p, axis=-1, keepdims=True), att)
    o = _dot(att, v) + _dot_nt(q * jnp.exp(b), st)
    b_end = b[c - 1:c, :]
    st_new = st * jnp.exp(b_end) + _dot_tn(v, k * jnp.exp(b_end - b))
    return o, st_new


def _gla_kernel(p_ref, s0_ref, gup_ref, gbias_ref, gnorm_ref, o_ref, sout_ref, s_scr, *, c):
    ci = pl.program_id(1)

    @pl.when(ci == 0)
    def _():
        s_scr[...] = s0_ref[0]

    mk = _GlrMasks(c, GLA_DKP)
    p = p_ref[...]
    q_all = p[:, 0:GLA_QW] * (GLA_DK ** -0.5)
    k_all = p[:, GLA_QW:2 * GLA_QW]
    v_all = p[:, 2 * GLA_QW:2 * GLA_QW + GLA_VW]
    gout = p[:, 2 * GLA_QW + GLA_VW:2 * GLA_QW + 2 * GLA_VW]
    gdown = p[:, 2 * GLA_QW + 2 * GLA_VW:]
    g_all = _log_sigmoid(_dot(gdown, gup_ref[...]) + gbias_ref[...]) / GLA_GATE_NORMALIZER
    gnorm = gnorm_ref[...]
    for h in range(GLA_HEADS):
        ks = slice(h * GLA_DKP, (h + 1) * GLA_DKP)
        vs = slice(h * GLA_DVP, (h + 1) * GLA_DVP)
        o, st = _glr_chunk(q_all[:, ks], k_all[:, ks], v_all[:, vs], g_all[:, ks], s_scr[h], mk)
        s_scr[h] = st
        ms = jnp.sum(o * o, axis=-1, keepdims=True) * (1.0 / GLA_DV)
        o = o * lax.rsqrt(ms + EPS) * gnorm * _silu(gout[:, vs])
        o_ref[:, vs] = o.astype(BF16)

    @pl.when(ci == pl.num_programs(1) - 1)
    def _():
        sout_ref[0] = s_scr[...]


def _gla_mixer(p, s0t, gup, gbias, gnorm, b, t, c):
    nc = t // c
    kern = functools.partial(_gla_kernel, c=c)
    full = lambda a: pl.BlockSpec(a.shape, lambda i, j: (0,) * a.ndim)
    st_spec = pl.BlockSpec((1, GLA_HEADS, GLA_DVP, GLA_DKP), lambda i, j: (i, 0, 0, 0))
    return pl.pallas_call(
        kern,
        grid=(b, nc),
        in_specs=[pl.BlockSpec((c, GLA_COLS_P), lambda i, j: (i * nc + j, 0)),
                  st_spec, full(gup), full(gbias), full(gnorm)],
        out_specs=[pl.BlockSpec((c, GLA_VW), lambda i, j: (i * nc + j, 0)), st_spec],
        out_shape=[jax.ShapeDtypeStruct((b * t, GLA_VW), BF16),
                   jax.ShapeDtypeStruct((b, GLA_HEADS, GLA_DVP, GLA_DKP), F32)],
        scratch_shapes=[pltpu.VMEM((GLA_HEADS, GLA_DVP, GLA_DKP), F32)],
        compiler_params=_compiler_params(("arbitrary", "arbitrary")),
    )(p, s0t, gup, gbias, gnorm)


def _hgrn_kernel(p_ref, s0_ref, lbl_ref, hnorm_ref, o_ref, sout_ref, s_scr, *, c, layer):
    ci = pl.program_id(1)

    @pl.when(ci == 0)
    def _():
        s_scr[...] = s0_ref[0]

    lg = lbl_ref[...]
    e = jnp.exp(lg - jnp.max(lg, axis=0, keepdims=True))
    prob = e / jnp.sum(e, axis=0, keepdims=True)
    lb = jnp.zeros((1, HGRN_WIDTH), F32)
    for i in range(1, layer + 1):
        lb = lb + prob[i:i + 1, :]

    mk = _GlrMasks(c, HGRN_D)
    p = p_ref[...]
    hq = p[:, 0:HGRN_WIDTH]
    hf = p[:, HGRN_WIDTH:2 * HGRN_WIDTH]
    hi = p[:, 2 * HGRN_WIDTH:3 * HGRN_WIDTH]
    hg = p[:, 3 * HGRN_WIDTH:]
    q_all = _silu(hq)
    g_all = jnp.log(lb + (1.0 - lb) * _sigmoid(hf))
    k_all = (1.0 - lb) * _sigmoid(-hf)
    hnorm = hnorm_ref[...]
    for h in range(HGRN_HEADS):
        s = slice(h * HGRN_D, (h + 1) * HGRN_D)
        o, st = _glr_chunk(q_all[:, s], k_all[:, s], hi[:, s], g_all[:, s], s_scr[h], mk)
        s_scr[h] = st
        o = _rms_rows(o, hnorm) * _silu(hg[:, s])
        o_ref[:, s] = o.astype(BF16)

    @pl.when(ci == pl.num_programs(1) - 1)
    def _():
        sout_ref[0] = s_scr[...]


def _hgrn_mixer(p, s0t, lb_logits, hnorm, b, t, c, layer):
    nc = t // c
    kern = functools.partial(_hgrn_kernel, c=c, layer=layer)
    full = lambda a: pl.BlockSpec(a.shape, lambda i, j: (0,) * a.ndim)
    st_spec = pl.BlockSpec((1, HGRN_HEADS, HGRN_D, HGRN_D), lambda i, j: (i, 0, 0, 0))
    return pl.pallas_call(
        kern,
        grid=(b, nc),
        in_specs=[pl.BlockSpec((c, HGRN_COLS), lambda i, j: (i * nc + j, 0)),
                  st_spec, full(lb_logits), full(hnorm)],
        out_specs=[pl.BlockSpec((c, HGRN_WIDTH), lambda i, j: (i * nc + j, 0)), st_spec],
        out_shape=[jax.ShapeDtypeStruct((b * t, HGRN_WIDTH), BF16),
                   jax.ShapeDtypeStruct((b, HGRN_HEADS, HGRN_D, HGRN_D), F32)],
        scratch_shapes=[pltpu.VMEM((HGRN_HEADS, HGRN_D, HGRN_D), F32)],
        compiler_params=_compiler_params(("arbitrary", "arbitrary")),
    )(p, s0t, lb_logits, hnorm)


def _rwkv_kernel(p_ref, shift0_ref, s0_ref, mu_ref, w0_ref, wup_ref, a0_ref, aup_ref, gup_ref,
                 kk_ref, ka_ref, rk_ref, lnw_ref, lnb_ref,
                 o_ref, shift_ref, sout_ref, s_scr, carry_scr, *, c):
    ci = pl.program_id(1)

    @pl.when(ci == 0)
    def _():
        s_scr[...] = s0_ref[0]
        carry_scr[...] = shift0_ref[0]

    w = RWKV_WIDTH
    p = p_ref[...]
    row = _iota((c, RWKV_COLS), 0)
    prev = jnp.where(row == 0, carry_scr[...], pltpu.roll(p, 1, 0))
    last = p[c - 1:c, :]
    carry_scr[...] = last
    shift_ref[0] = last
    pm = p + (prev - p) * mu_ref[...]
    r_all = pm[:, 0:w]
    k_all = pm[:, w:2 * w]
    v_all = pm[:, 2 * w:3 * w]
    wa = pm[:, 3 * w:3 * w + LANE]
    gd = pm[:, 3 * w + LANE:]
    lw_all = -jnp.exp(_log_sigmoid(w0_ref[...] + _dot(jnp.tanh(wa), wup_ref[...])) - 0.5)
    a_all = _sigmoid(a0_ref[...] + _dot(wa, aup_ref[...]))
    gate_all = _dot(_sigmoid(gd), gup_ref[...])

    c2 = 2 * c
    lane = _iota((c, LANE), 1)
    lo_half = lane < RWKV_HD
    ri = _iota((c2, c2), 0)
    cj = _iota((c2, c2), 1)
    strict = (ri % c) > (cj % c)
    incl = (ri % c) >= (cj % c)
    eye = jnp.where(ri == cj, 1.0, 0.0)
    inv_levels = []
    blk = 1
    while blk < c:
        inv_levels.append(((ri // (2 * blk)) == (cj // (2 * blk)))
                          & ((ri // blk) % 2 == 1) & ((cj // blk) % 2 == 0))
        blk *= 2
    tril = _tril_ones(c)
    seg = jnp.where((_iota((LANE, LANE), 0) // RWKV_HD) == (_iota((LANE, LANE), 1) // RWKV_HD),
                    1.0, 0.0).astype(BF16)

    def stack(x):
        return jnp.concatenate([jnp.where(lo_half, x, 0.0), jnp.where(lo_half, 0.0, x)], axis=0)

    for j in range(RWKV_PAIRS):
        s = slice(j * LANE, (j + 1) * LANE)
        r, k, v, a, lw = r_all[:, s], k_all[:, s], v_all[:, s], a_all[:, s], lw_all[:, s]
        kk = k * kk_ref[:, s]
        kk = kk * lax.rsqrt(jnp.maximum(_dot_exact_rhs(kk * kk, seg), 1e-24))
        k = k * (1.0 + (a - 1.0) * ka_ref[:, s])
        beta = kk * a
        cin = _dot_exact_lhs(tril, lw)
        cex = cin - lw
        cend = cin[c - 1:c, :]
        ginv = jnp.exp(-cin)
        gend = jnp.exp(cend - cin)
        kap_s = stack(kk * jnp.exp(cex))
        r_s = stack(r * jnp.exp(cin))
        k_s = stack(k * ginv)
        b_s = stack(beta * ginv)
        v_s = stack(v)
        st = s_scr[j]
        a1 = jnp.where(strict, _dot_nt(kap_s, k_s), 0.0)
        a2 = jnp.where(strict, _dot_nt(kap_s, b_s), 0.0)
        a3 = jnp.where(incl, _dot_nt(r_s, k_s), 0.0)
        a4 = jnp.where(incl, _dot_nt(r_s, b_s), 0.0)
        tinv = eye
        for lvl, m in enumerate(inv_levels):
            nl = jnp.where(m, a2, 0.0)
            if lvl == 0:
                tinv = eye - nl
            else:
                tinv = tinv - _dot(_dot(tinv, nl), tinv)
        u = _dot(tinv, _dot_nt(kap_s, st) + _dot(a1, v_s))
        o2 = _dot_nt(r_s, st) + _dot(a3, v_s) - _dot(a4, u)
        o = o2[0:c] + o2[c:c2]
        s_scr[j] = (st * jnp.exp(cend) + _dot_tn(v_s, stack(k * gend))
                    - _dot_tn(u, stack(beta * gend)))
        mean = _dot_exact_rhs(o, seg) * (1.0 / RWKV_HD)
        d = o - mean
        var = _dot_exact_rhs(d * d, seg) * (1.0 / RWKV_HD)
        on = d * lax.rsqrt(var + RWKV_GN_EPS) * lnw_ref[:, s] + lnb_ref[:, s]
        bonus = _dot_exact_rhs(r * k * rk_ref[:, s], seg) * v
        o_ref[:, s] = ((on + bonus) * gate_all[:, s]).astype(BF16)

    @pl.when(ci == pl.num_programs(1) - 1)
    def _():
        sout_ref[0] = s_scr[...]


def _rwkv_mixer(p, shift0, s0, wts, b, t, c):
    nc = t // c
    kern = functools.partial(_rwkv_kernel, c=c)
    full = lambda a: pl.BlockSpec(a.shape, lambda i, j: (0,) * a.ndim)
    st_spec = pl.BlockSpec((1, RWKV_PAIRS, LANE, LANE), lambda i, j: (i, 0, 0, 0))
    sh_spec = pl.BlockSpec((1, 1, RWKV_COLS), lambda i, j: (i, 0, 0))
    return pl.pallas_call(
        kern,
        grid=(b, nc),
        in_specs=[pl.BlockSpec((c, RWKV_COLS), lambda i, j: (i * nc + j, 0)), sh_spec, st_spec]
                 + [full(a) for a in wts],
        out_specs=[pl.BlockSpec((c, RWKV_WIDTH), lambda i, j: (i * nc + j, 0)), sh_spec, st_spec],
        out_shape=[jax.ShapeDtypeStruct((b * t, RWKV_WIDTH), BF16),
                   jax.ShapeDtypeStruct((b, 1, RWKV_COLS), F32),
                   jax.ShapeDtypeStruct((b, RWKV_PAIRS, LANE, LANE), F32)],
        scratch_shapes=[pltpu.VMEM((RWKV_PAIRS, LANE, LANE), F32), pltpu.VMEM((1, RWKV_COLS), F32)],
        compiler_params=_compiler_params(("arbitrary", "arbitrary")),
    )(p, shift0, s0, *wts)


_RWKV_PERM = np.concatenate([np.arange(0, 640), np.arange(704, 1344), np.arange(1344, 1984),
                             np.arange(640, 704), np.arange(1984, 2048), np.arange(2048, 2176)])
_RWKV_INV_PERM = np.argsort(_RWKV_PERM)


def _pad_heads(w, heads, d, dp):
    lead = w.shape[:-1]
    w = w.reshape(lead + (heads, d))
    w = jnp.pad(w, [(0, 0)] * len(lead) + [(0, 0), (0, dp - d)])
    return w.reshape(lead + (heads * dp,))


def _prep_weights(w_in, gla_gate_up, gla_gate_bias, gla_norm, rwkv_mu, rwkv_w_up, rwkv_a_up,
                  rwkv_r_k, w_out):
    gq, gk, gv, gdn, gout, rw, hg = jnp.split(
        w_in, [384, 768, 1536, 1552, 2320, 2320 + RWKV_COLS], axis=-1)
    w_gla = jnp.concatenate([
        _pad_heads(gq, GLA_HEADS, GLA_DK, GLA_DKP), _pad_heads(gk, GLA_HEADS, GLA_DK, GLA_DKP),
        _pad_heads(gv, GLA_HEADS, GLA_DV, GLA_DVP), _pad_heads(gout, GLA_HEADS, GLA_DV, GLA_DVP),
        jnp.pad(gdn, ((0, 0), (0, 0), (0, GLA_RANKP - GLA_RANK)))], axis=-1).astype(BF16)
    w_rwkv = rw[..., _RWKV_PERM].astype(BF16)
    w_hgrn = hg.astype(BF16)
    gup = jnp.pad(_pad_heads(gla_gate_up, GLA_HEADS, GLA_DK, GLA_DKP),
                  ((0, 0), (0, GLA_RANKP - GLA_RANK), (0, 0))).astype(BF16)
    gbias = _pad_heads(gla_gate_bias, GLA_HEADS, GLA_DK, GLA_DKP)[:, None, :]
    gnorm = jnp.pad(gla_norm, ((0, 0), (0, GLA_DVP - GLA_DV)))[:, None, :]
    mu = rwkv_mu[:, _RWKV_PERM][:, None, :]
    wup = jnp.pad(rwkv_w_up, ((0, 0), (0, 64), (0, 0))).astype(BF16)
    aup = jnp.pad(rwkv_a_up, ((0, 0), (64, 0), (0, 0))).astype(BF16)
    rk = rwkv_r_k.reshape(DEPTH, 1, RWKV_WIDTH)
    wo_a = _pad_heads(w_out[:, :768].transpose(0, 2, 1), GLA_HEADS, GLA_DV, GLA_DVP)
    wo_a = wo_a.transpose(0, 2, 1).astype(BF16)
    wo_b = w_out[:, 768:1408].astype(BF16)
    wo_c = w_out[:, 1408:].astype(BF16)
    return w_gla, w_rwkv, w_hgrn, gup, gbias, gnorm, mu, wup, aup, rk, wo_a, wo_b, wo_c


def _gla_state_in(s):
    s = jnp.swapaxes(s, -1, -2)
    return jnp.pad(s, [(0, 0)] * 3 + [(0, GLA_DVP - GLA_DV), (0, GLA_DKP - GLA_DK)])


def _gla_state_out(s):
    return jnp.swapaxes(s[..., :GLA_DV, :GLA_DK], -1, -2)


def _rwkv_state_in(s):
    d, b = s.shape[:2]
    s = s.reshape(d, b, RWKV_PAIRS, 2, RWKV_HD, RWKV_HD)
    out = jnp.einsum('dbjhvk,hg->dbjhvgk', s, jnp.eye(2, dtype=s.dtype))
    return out.reshape(d, b, RWKV_PAIRS, LANE, LANE)


def _rwkv_state_out(s):
    d, b = s.shape[:2]
    s = s.reshape(d, b, RWKV_PAIRS, 2, RWKV_HD, 2, RWKV_HD)
    out = jnp.stack([s[:, :, :, 0, :, 0, :], s[:, :, :, 1, :, 1, :]], axis=3)
    return out.reshape(d, b, RWKV_HEADS, RWKV_HD, RWKV_HD)


def _row_tile(n, cap):
    t = cap
    while n % t:
        t //= 2
    return t


def _trunk(x, s_gla, s_rwkv, s_shift, s_hgrn, wt, b, t):
    n = b * t
    tm = _row_tile(n, 512)
    c_glr = min(128, t)
    c_rwkv = min(64, t)
    new_gla, new_rwkv, new_shift, new_hgrn = [], [], [], []
    for l in range(DEPTH):
        pg = _norm_matmul(x, wt['norm_mix_pre'][l], wt['w_gla'][l], tm, 640)
        pr = _norm_matmul(x, wt['norm_mix_pre'][l], wt['w_rwkv'][l], tm, RWKV_COLS)
        ph = _norm_matmul(x, wt['norm_mix_pre'][l], wt['w_hgrn'][l], tm, 1280)
        oa, g1 = _gla_mixer(pg, s_gla[l], wt['gup'][l], wt['gbias'][l], wt['gnorm'][l], b, t, c_glr)
        rw = [wt[k][l] for k in ('mu', 'w0', 'wup', 'a0', 'aup', 'g_up', 'k_k', 'k_a', 'rk',
                                 'ln_w', 'ln_b')]
        ob, sh1, r1 = _rwkv_mixer(pr, s_shift[l], s_rwkv[l], rw, b, t, c_rwkv)
        oc, h1 = _hgrn_mixer(ph, s_hgrn[l], wt['lb_logits'], wt['hnorm'][l], b, t, c_glr, l)
        x = _out_proj(oa, ob, oc, wt['wo_a'][l], wt['wo_b'][l], wt['wo_c'][l], x,
                      wt['norm_mix_post'][l], tm)
        x = _ffn(x, wt['norm_ffn_pre'][l], wt['ffn_w_gate'][l], wt['ffn_w_up'][l],
                 wt['ffn_w_down'][l], wt['norm_ffn_post'][l], tm, 512)
        new_gla.append(g1)
        new_rwkv.append(r1)
        new_shift.append(sh1)
        new_hgrn.append(h1)
    return x, jnp.stack(new_gla), jnp.stack(new_rwkv), jnp.stack(new_shift), jnp.stack(new_hgrn)


def _run_path(x, state_gla, state_rwkv, state_shift, state_hgrn, wt):
    b, t, d = x.shape
    y, g, r, sh, h = _trunk(
        x.reshape(b * t, d), _gla_state_in(state_gla), _rwkv_state_in(state_rwkv),
        state_shift[:, :, _RWKV_PERM][:, :, None, :], jnp.swapaxes(state_hgrn, -1, -2), wt, b, t)
    return (y.reshape(b, t, d), _gla_state_out(g), _rwkv_state_out(r),
            sh[:, :, 0, :][:, :, _RWKV_INV_PERM], jnp.swapaxes(h, -1, -2))


def kernel(x_prompt, x_sample, state_gla, state_rwkv, state_rwkv_shift, state_hgrn, norm_mix_pre, norm_mix_post, norm_ffn_pre, norm_ffn_post, w_in, gla_gate_up, gla_gate_bias, gla_norm, rwkv_mu, rwkv_w0, rwkv_w_up, rwkv_a0, rwkv_a_up, rwkv_g_up, rwkv_k_k, rwkv_k_a, rwkv_r_k, rwkv_ln_w, rwkv_ln_b, hgrn_lb_logits, hgrn_norm, w_out, ffn_w_gate, ffn_w_up, ffn_w_down):
    (w_gla, w_rwkv, w_hgrn, gup, gbias, gnorm, mu, wup, aup, rk, wo_a, wo_b, wo_c) = _prep_weights(
        w_in, gla_gate_up, gla_gate_bias, gla_norm, rwkv_mu, rwkv_w_up, rwkv_a_up, rwkv_r_k, w_out)
    vec = lambda a: a[:, None, :]
    wt = dict(
        norm_mix_pre=vec(norm_mix_pre), norm_mix_post=vec(norm_mix_post),
        norm_ffn_pre=vec(norm_ffn_pre), norm_ffn_post=vec(norm_ffn_post),
        w_gla=w_gla, w_rwkv=w_rwkv, w_hgrn=w_hgrn, gup=gup, gbias=gbias, gnorm=gnorm,
        mu=mu, w0=vec(rwkv_w0), wup=wup, a0=vec(rwkv_a0), aup=aup, g_up=rwkv_g_up.astype(BF16),
        k_k=vec(rwkv_k_k), k_a=vec(rwkv_k_a), rk=rk, ln_w=vec(rwkv_ln_w), ln_b=vec(rwkv_ln_b),
        lb_logits=hgrn_lb_logits, hnorm=vec(hgrn_norm),
        wo_a=wo_a, wo_b=wo_b, wo_c=wo_c,
        ffn_w_gate=ffn_w_gate.astype(BF16), ffn_w_up=ffn_w_up.astype(BF16),
        ffn_w_down=ffn_w_down.astype(BF16))

    bp = x_prompt.shape[0]
    zeros = lambda s: jnp.zeros((DEPTH, bp) + s.shape[2:], x_prompt.dtype)
    y_p, gla_p, rwkv_p, shift_p, hgrn_p = _run_path(
        x_prompt, zeros(state_gla), zeros(state_rwkv), zeros(state_rwkv_shift), zeros(state_hgrn), wt)
    y_s, gla_s, rwkv_s, shift_s, hgrn_s = _run_path(
        x_sample, state_gla, state_rwkv, state_rwkv_shift, state_hgrn, wt)
    return (y_p, y_s, gla_p, rwkv_p, shift_p, hgrn_p, gla_s, rwkv_s, shift_s, hgrn_s)
```

```python
import functools

import jax
import jax.numpy as jnp
import numpy as np
from jax import lax
from jax.experimental import pallas as pl
from jax.experimental.pallas import tpu as pltpu

F32 = jnp.float32
BF16 = jnp.bfloat16

D_MODEL = 2048
DEPTH = 4
EPS = 1e-6

GLA_HEADS, GLA_DK, GLA_DV = 4, 96, 192
GLA_DKP, GLA_DVP = 128, 256
GLA_RANK, GLA_RANKP = 16, 128
GLA_GATE_NORMALIZER = 16.0
GLA_QW = GLA_HEADS * GLA_DKP
GLA_VW = GLA_HEADS * GLA_DVP
GLA_COLS_P = 2 * GLA_QW + 2 * GLA_VW + GLA_RANKP

RWKV_HEADS, RWKV_HD = 10, 64
RWKV_WIDTH = RWKV_HEADS * RWKV_HD
RWKV_PAIRS = RWKV_HEADS // 2
RWKV_COLS = 2176
RWKV_GN_EPS = 64e-5

HGRN_HEADS, HGRN_D = 5, 128
HGRN_WIDTH = HGRN_HEADS * HGRN_D
HGRN_COLS = 4 * HGRN_WIDTH

D_FF = 5632
LANE = 128
DIAG = 8

VMEM_LIMIT = 56 * 1024 * 1024


def _sigmoid(x):
    return 1.0 / (1.0 + jnp.exp(-x))


def _silu(x):
    return x * _sigmoid(x)


def _log_sigmoid(x):
    return jnp.minimum(x, 0.0) - jnp.log(1.0 + jnp.exp(-jnp.abs(x)))


def _dot(a, b):
    return jnp.dot(a.astype(BF16), b.astype(BF16), preferred_element_type=F32)


def _dot_nt(a, b):
    return lax.dot_general(a.astype(BF16), b.astype(BF16), (((1,), (1,)), ((), ())),
                           preferred_element_type=F32)


def _dot_tn(a, b):
    return lax.dot_general(a.astype(BF16), b.astype(BF16), (((0,), (0,)), ((), ())),
                           preferred_element_type=F32)


def _split_bf16(x, parts):
    out = []
    r = x
    for i in range(parts):
        p = r.astype(BF16)
        out.append(p)
        if i + 1 < parts:
            r = r - p.astype(F32)
    return out


def _dot_exact_lhs(m_bf16, x, parts=3):
    acc = None
    for p in _split_bf16(x, parts):
        t = jnp.dot(m_bf16, p, preferred_element_type=F32)
        acc = t if acc is None else acc + t
    return acc


def _dot_exact_rhs(x, m_bf16, parts=2):
    acc = None
    for p in _split_bf16(x, parts):
        t = jnp.dot(p, m_bf16, preferred_element_type=F32)
        acc = t if acc is None else acc + t
    return acc


def _iota(shape, dim):
    return lax.broadcasted_iota(jnp.int32, shape, dim)


def _tril_ones(c):
    return jnp.where(_iota((c, c), 0) >= _iota((c, c), 1), 1.0, 0.0).astype(BF16)


def _rms_rows(x, g):
    ms = jnp.mean(x * x, axis=-1, keepdims=True)
    return x * lax.rsqrt(ms + EPS) * g


def _compiler_params(sem):
    return pltpu.CompilerParams(dimension_semantics=sem, vmem_limit_bytes=VMEM_LIMIT)


def _norm_matmul_kernel(x_ref, g_ref, w_ref, o_ref, h_scr):
    @pl.when(pl.program_id(1) == 0)
    def _():
        h_scr[...] = _rms_rows(x_ref[...], g_ref[...]).astype(BF16)

    o_ref[...] = jnp.dot(h_scr[...], w_ref[...], preferred_element_type=F32)


def _norm_matmul(x, g, w, tm, tn):
    n, d = x.shape
    cout = w.shape[1]
    return pl.pallas_call(
        _norm_matmul_kernel,
        grid=(n // tm, cout // tn),
        in_specs=[pl.BlockSpec((tm, d), lambda i, j: (i, 0)),
                  pl.BlockSpec((1, d), lambda i, j: (0, 0)),
                  pl.BlockSpec((d, tn), lambda i, j: (0, j))],
        out_specs=pl.BlockSpec((tm, tn), lambda i, j: (i, j)),
        out_shape=jax.ShapeDtypeStruct((n, cout), F32),
        scratch_shapes=[pltpu.VMEM((tm, d), BF16)],
        compiler_params=_compiler_params(("arbitrary", "arbitrary")),
    )(x, g, w)


def _out_proj_kernel(oa_ref, ob_ref, oc_ref, wa_ref, wb_ref, wc_ref, x_ref, g_ref, y_ref):
    m = (jnp.dot(oa_ref[...], wa_ref[...], preferred_element_type=F32)
         + jnp.dot(ob_ref[...], wb_ref[...], preferred_element_type=F32)
         + jnp.dot(oc_ref[...], wc_ref[...], preferred_element_type=F32))
    y_ref[...] = x_ref[...] + _rms_rows(m, g_ref[...])


def _out_proj(oa, ob, oc, wa, wb, wc, x, g, tm):
    n, d = x.shape
    row = lambda w: pl.BlockSpec((tm, w), lambda i: (i, 0))
    full = lambda a: pl.BlockSpec(a.shape, lambda i: (0, 0))
    return pl.pallas_call(
        _out_proj_kernel,
        grid=(n // tm,),
        in_specs=[row(oa.shape[1]), row(ob.shape[1]), row(oc.shape[1]),
                  full(wa), full(wb), full(wc), row(d), full(g)],
        out_specs=row(d),
        out_shape=jax.ShapeDtypeStruct((n, d), F32),
        compiler_params=_compiler_params(("arbitrary",)),
    )(oa, ob, oc, wa, wb, wc, x, g)


def _ffn_kernel(x_ref, gpre_ref, wg_ref, wu_ref, wd_ref, gpost_ref, y_ref, h_scr, acc_scr):
    j = pl.program_id(1)

    @pl.when(j == 0)
    def _():
        h_scr[...] = _rms_rows(x_ref[...], gpre_ref[...]).astype(BF16)
        acc_scr[...] = jnp.zeros_like(acc_scr)

    h = h_scr[...]
    a = jnp.dot(h, wg_ref[...], preferred_element_type=F32)
    u = jnp.dot(h, wu_ref[...], preferred_element_type=F32)
    acc_scr[...] += jnp.dot((_silu(a) * u).astype(BF16), wd_ref[...], preferred_element_type=F32)

    @pl.when(j == pl.num_programs(1) - 1)
    def _():
        y_ref[...] = x_ref[...] + _rms_rows(acc_scr[...], gpost_ref[...])


def _ffn(x, gpre, wg, wu, wd, gpost, tm, tf):
    n, d = x.shape
    dff = wg.shape[1]
    return pl.pallas_call(
        _ffn_kernel,
        grid=(n // tm, dff // tf),
        in_specs=[pl.BlockSpec((tm, d), lambda i, j: (i, 0)),
                  pl.BlockSpec((1, d), lambda i, j: (0, 0)),
                  pl.BlockSpec((d, tf), lambda i, j: (0, j)),
                  pl.BlockSpec((d, tf), lambda i, j: (0, j)),
                  pl.BlockSpec((tf, d), lambda i, j: (j, 0)),
                  pl.BlockSpec((1, d), lambda i, j: (0, 0))],
        out_specs=pl.BlockSpec((tm, d), lambda i, j: (i, 0)),
        out_shape=jax.ShapeDtypeStruct((n, d), F32),
        scratch_shapes=[pltpu.VMEM((tm, d), BF16), pltpu.VMEM((tm, d), F32)],
        compiler_params=_compiler_params(("arbitrary", "arbitrary")),
    )(x, gpre, wg, wu, wd, gpost)


class _GlrMasks:
    def __init__(self, c, dk):
        self.c = c
        row_k = _iota((c, dk), 0)
        row_a = _iota((c, c), 0)
        col_a = _iota((c, c), 1)
        self.tril = _tril_ones(c)
        self.levels = []
        h = DIAG
        while 2 * h <= c:
            is_q = (row_k // h) % 2 == 1
            same = (row_a // (2 * h)) == (col_a // (2 * h))
            self.levels.append((h, is_q, same))
            h *= 2
        self.diag = []
        for d in range(min(DIAG, c)):
            self.diag.append((col_a == row_a - d) & ((row_a % DIAG) >= d))


def _glr_chunk(q, k, v, g, st, mk):
    c, dk = q.shape
    b = _dot_exact_lhs(mk.tril, g)
    att = jnp.zeros((c, c), F32)
    for h, is_q, same in mk.levels:
        n = c // (2 * h)
        ref = b.reshape(n, 2 * h, dk)[:, h - 1:h, :]
        refb = jnp.broadcast_to(ref, (n, 2 * h, dk)).reshape(c, dk)
        e = jnp.exp(jnp.where(is_q, b - refb, refb - b))
        qh = jnp.where(is_q, q * e, 0.0)
        kh = jnp.where(is_q, 0.0, k * e)
        att = att + jnp.where(same, _dot_nt(qh, kh), 0.0)
    for d, valid in enumerate(mk.diag):
        if d == 0:
            p = q * k
        else:
            ks = pltpu.roll(k, d, 0)
            bs = pltpu.roll(b, d, 0)
            p = q * ks * jnp.exp(jnp.minimum(b - bs, 0.0))
        att = jnp.where(valid, jnp.sum(p, axis=-1, keepdims=True), att)
    o = _dot(att, v) + _dot_nt(q * jnp.exp(b), st)
    b_end = b[c - 1:c, :]
    st_new = st * jnp.exp(b_end) + _dot_tn(v, k * jnp.exp(b_end - b))
    return o, st_new


def _gla_kernel(p_ref, s0_ref, gup_ref, gbias_ref, gnorm_ref, o_ref, sout_ref, s_scr, *, c):
    ci = pl.program_id(1)

    @pl.when(ci == 0)
    def _():
        s_scr[...] = s0_ref[0]

    mk = _GlrMasks(c, GLA_DKP)
    p = p_ref[...]
    q_all = p[:, 0:GLA_QW] * (GLA_DK ** -0.5)
    k_all = p[:, GLA_QW:2 * GLA_QW]
    v_all = p[:, 2 * GLA_QW:2 * GLA_QW + GLA_VW]
    gout = p[:, 2 * GLA_QW + GLA_VW:2 * GLA_QW + 2 * GLA_VW]
    gdown = p[:, 2 * GLA_QW + 2 * GLA_VW:]
    g_all = _log_sigmoid(_dot(gdown, gup_ref[...]) + gbias_ref[...]) / GLA_GATE_NORMALIZER
    gnorm = gnorm_ref[...]
    for h in range(GLA_HEADS):
        ks = slice(h * GLA_DKP, (h + 1) * GLA_DKP)
        vs = slice(h * GLA_DVP, (h + 1) * GLA_DVP)
        o, st = _glr_chunk(q_all[:, ks], k_all[:, ks], v_all[:, vs], g_all[:, ks], s_scr[h], mk)
        s_scr[h] = st
        ms = jnp.sum(o * o, axis=-1, keepdims=True) * (1.0 / GLA_DV)
        o = o * lax.rsqrt(ms + EPS) * gnorm * _silu(gout[:, vs])
        o_ref[:, vs] = o.astype(BF16)

    @pl.when(ci == pl.num_programs(1) - 1)
    def _():
        sout_ref[0] = s_scr[...]


def _gla_mixer(p, s0t, gup, gbias, gnorm, b, t, c):
    nc = t // c
    kern = functools.partial(_gla_kernel, c=c)
    full = lambda a: pl.BlockSpec(a.shape, lambda i, j: (0,) * a.ndim)
    st_spec = pl.BlockSpec((1, GLA_HEADS, GLA_DVP, GLA_DKP), lambda i, j: (i, 0, 0, 0))
    return pl.pallas_call(
        kern,
        grid=(b, nc),
        in_specs=[pl.BlockSpec((c, GLA_COLS_P), lambda i, j: (i * nc + j, 0)),
                  st_spec, full(gup), full(gbias), full(gnorm)],
        out_specs=[pl.BlockSpec((c, GLA_VW), lambda i, j: (i * nc + j, 0)), st_spec],
        out_shape=[jax.ShapeDtypeStruct((b * t, GLA_VW), BF16),
                   jax.ShapeDtypeStruct((b, GLA_HEADS, GLA_DVP, GLA_DKP), F32)],
        scratch_shapes=[pltpu.VMEM((GLA_HEADS, GLA_DVP, GLA_DKP), F32)],
        compiler_params=_compiler_params(("arbitrary", "arbitrary")),
    )(p, s0t, gup, gbias, gnorm)


def _hgrn_kernel(p_ref, s0_ref, lbl_ref, hnorm_ref, o_ref, sout_ref, s_scr, *, c, layer):
    ci = pl.program_id(1)

    @pl.when(ci == 0)
    def _():
        s_scr[...] = s0_ref[0]

    lg = lbl_ref[...]
    e = jnp.exp(lg - jnp.max(lg, axis=0, keepdims=True))
    prob = e / jnp.sum(e, axis=0, keepdims=True)
    lb = jnp.zeros((1, HGRN_WIDTH), F32)
    for i in range(1, layer + 1):
        lb = lb + prob[i:i + 1, :]

    mk = _GlrMasks(c, HGRN_D)
    p = p_ref[...]
    hq = p[:, 0:HGRN_WIDTH]
    hf = p[:, HGRN_WIDTH:2 * HGRN_WIDTH]
    hi = p[:, 2 * HGRN_WIDTH:3 * HGRN_WIDTH]
    hg = p[:, 3 * HGRN_WIDTH:]
    q_all = _silu(hq)
    g_all = jnp.log(lb + (1.0 - lb) * _sigmoid(hf))
    k_all = (1.0 - lb) * _sigmoid(-hf)
    hnorm = hnorm_ref[...]
    for h in range(HGRN_HEADS):
        s = slice(h * HGRN_D, (h + 1) * HGRN_D)
        o, st = _glr_chunk(q_all[:, s], k_all[:, s], hi[:, s], g_all[:, s], s_scr[h], mk)
        s_scr[h] = st
        o = _rms_rows(o, hnorm) * _silu(hg[:, s])
        o_ref[:, s] = o.astype(BF16)

    @pl.when(ci == pl.num_programs(1) - 1)
    def _():
        sout_ref[0] = s_scr[...]


def _hgrn_mixer(p, s0t, lb_logits, hnorm, b, t, c, layer):
    nc = t // c
    kern = functools.partial(_hgrn_kernel, c=c, layer=layer)
    full = lambda a: pl.BlockSpec(a.shape, lambda i, j: (0,) * a.ndim)
    st_spec = pl.BlockSpec((1, HGRN_HEADS, HGRN_D, HGRN_D), lambda i, j: (i, 0, 0, 0))
    return pl.pallas_call(
        kern,
        grid=(b, nc),
        in_specs=[pl.BlockSpec((c, HGRN_COLS), lambda i, j: (i * nc + j, 0)),
                  st_spec, full(lb_logits), full(hnorm)],
        out_specs=[pl.BlockSpec((c, HGRN_WIDTH), lambda i, j: (i * nc + j, 0)), st_spec],
        out_shape=[jax.ShapeDtypeStruct((b * t, HGRN_WIDTH), BF16),
                   jax.ShapeDtypeStruct((b, HGRN_HEADS, HGRN_D, HGRN_D), F32)],
        scratch_shapes=[pltpu.VMEM((HGRN_HEADS, HGRN_D, HGRN_D), F32)],
        compiler_params=_compiler_params(("arbitrary", "arbitrary")),
    )(p, s0t, lb_logits, hnorm)


def _rwkv_kernel(p_ref, shift0_ref, s0_ref, mu_ref, w0_ref, wup_ref, a0_ref, aup_ref, gup_ref,
                 kk_ref, ka_ref, rk_ref, lnw_ref, lnb_ref,
                 o_ref, shift_ref, sout_ref, s_scr, carry_scr, *, c, nsub):
    ci = pl.program_id(1)

    @pl.when(ci == 0)
    def _():
        s_scr[...] = s0_ref[0]
        carry_scr[...] = shift0_ref[0]

    w = RWKV_WIDTH
    rows = c * nsub
    p = p_ref[...]
    row = _iota((rows, RWKV_COLS), 0)
    prev = jnp.where(row == 0, carry_scr[...], pltpu.roll(p, 1, 0))
    last = p[rows - 1:rows, :]
    carry_scr[...] = last
    shift_ref[0] = last
    pm = p + (prev - p) * mu_ref[...]
    r_all = pm[:, 0:w]
    k_all = pm[:, w:2 * w]
    v_all = pm[:, 2 * w:3 * w]
    wa = pm[:, 3 * w:3 * w + LANE]
    gd = pm[:, 3 * w + LANE:]
    lw_all = -jnp.exp(_log_sigmoid(w0_ref[...] + _dot(jnp.tanh(wa), wup_ref[...])) - 0.5)
    a_all = _sigmoid(a0_ref[...] + _dot(wa, aup_ref[...]))
    gate_all = _dot(_sigmoid(gd), gup_ref[...])

    c2 = 2 * c
    lane = _iota((c, LANE), 1)
    lo_half = lane < RWKV_HD
    ri = _iota((c2, c2), 0)
    cj = _iota((c2, c2), 1)
    strict = (ri % c) > (cj % c)
    incl = (ri % c) >= (cj % c)
    eye = jnp.where(ri == cj, 1.0, 0.0)
    inv_levels = []
    blk = 1
    while blk < c:
        inv_levels.append(((ri // (2 * blk)) == (cj // (2 * blk)))
                          & ((ri // blk) % 2 == 1) & ((cj // blk) % 2 == 0))
        blk *= 2
    tr, tc = _iota((rows, rows), 0), _iota((rows, rows), 1)
    tril = jnp.where((tr >= tc) & (tr // c == tc // c), 1.0, 0.0).astype(BF16)
    seg = jnp.where((_iota((LANE, LANE), 0) // RWKV_HD) == (_iota((LANE, LANE), 1) // RWKV_HD),
                    1.0, 0.0).astype(BF16)

    def stack(x):
        return jnp.concatenate([jnp.where(lo_half, x, 0.0), jnp.where(lo_half, 0.0, x)], axis=0)

    def head_sum(x):
        return jnp.concatenate([_dot_exact_rhs(x[:, j * LANE:(j + 1) * LANE], seg)
                                for j in range(RWKV_PAIRS)], axis=1)

    kk_all = k_all * kk_ref[...]
    kk_all = kk_all * lax.rsqrt(jnp.maximum(head_sum(kk_all * kk_all), 1e-24))
    k_all = k_all * (1.0 + (a_all - 1.0) * ka_ref[...])
    beta_all = kk_all * a_all
    bonus_all = head_sum(r_all * k_all * rk_ref[...]) * v_all
    cin_all = _dot_exact_lhs(tril, lw_all)
    ginv = jnp.exp(-cin_all)
    kap_all = kk_all * jnp.exp(cin_all - lw_all)
    rt_all = r_all * jnp.exp(cin_all)
    kt_all = k_all * ginv
    bt_all = beta_all * ginv
    kend_all, bend_all, dec_all = [], [], []
    for i in range(nsub):
        rs = slice(i * c, (i + 1) * c)
        cend = cin_all[(i + 1) * c - 1:(i + 1) * c, :]
        gend = jnp.exp(cend - cin_all[rs])
        kend_all.append(k_all[rs] * gend)
        bend_all.append(beta_all[rs] * gend)
        dec_all.append(jnp.exp(cend))
    pairs = []
    for i, j in [(i, j) for i in range(nsub) for j in range(RWKV_PAIRS)]:
        s = slice(j * LANE, (j + 1) * LANE)
        rs = slice(i * c, (i + 1) * c)
        pairs.append(dict(
            s=s, rs=rs, kap_s=stack(kap_all[rs, s]), r_s=stack(rt_all[rs, s]),
            k_s=stack(kt_all[rs, s]), b_s=stack(bt_all[rs, s]), v_s=stack(v_all[rs, s]),
            kend_s=stack(kend_all[i][:, s]), bend_s=stack(bend_all[i][:, s]), dec=dec_all[i][:, s]))
    for pr in pairs:
        pr['a2'] = jnp.where(strict, _dot_nt(pr['kap_s'], pr['b_s']), 0.0)
    for pr in pairs:
        pr['a1'] = jnp.where(strict, _dot_nt(pr['kap_s'], pr['k_s']), 0.0)
        pr['a3'] = jnp.where(incl, _dot_nt(pr['r_s'], pr['k_s']), 0.0)
        pr['a4'] = jnp.where(incl, _dot_nt(pr['r_s'], pr['b_s']), 0.0)
    for pr in pairs:
        pr['a1v'] = _dot(pr['a1'], pr['v_s'])
        pr['a3v'] = _dot(pr['a3'], pr['v_s'])
        pr['kv'] = _dot_tn(pr['v_s'], pr['kend_s'])
    tinv = [eye - jnp.where(inv_levels[0], pr['a2'], 0.0) for pr in pairs]
    for m in inv_levels[1:]:
        half = [_dot(t, jnp.where(m, pr['a2'], 0.0)) for t, pr in zip(tinv, pairs)]
        tinv = [t - _dot(hf, t) for t, hf in zip(tinv, half)]
    for pr, tj in zip(pairs, tinv):
        pr['tk'] = _dot(tj, pr['kap_s'])
        pr['tv'] = _dot(tj, pr['a1v'])
    for pr in pairs:
        pr['reff'] = pr['r_s'] - _dot(pr['a4'], pr['tk'])
        pr['oc'] = pr['a3v'] - _dot(pr['a4'], pr['tv'])
        pr['m'] = _dot_tn(pr['tk'], pr['bend_s'])
        pr['q'] = pr['kv'] - _dot_tn(pr['tv'], pr['bend_s'])
    sts = [s_scr[j] for j in range(RWKV_PAIRS)]
    os_ = []
    for i in range(nsub):
        sub = list(zip(pairs[i * RWKV_PAIRS:(i + 1) * RWKV_PAIRS], sts))
        o2s = [_dot_nt(pr['reff'], st) + pr['oc'] for pr, st in sub]
        sts = [st * pr['dec'] - _dot(st, pr['m']) + pr['q'] for pr, st in sub]
        os_ += [o2[0:c] + o2[c:c2] for o2 in o2s]
    for j, st in enumerate(sts):
        s_scr[j] = st
    ds = [o - _dot_exact_rhs(o, seg) * (1.0 / RWKV_HD) for o in os_]
    vars_ = [_dot_exact_rhs(d * d, seg) * (1.0 / RWKV_HD) for d in ds]
    for pr, d, var in zip(pairs, ds, vars_):
        s, rs = pr['s'], pr['rs']
        on = d * lax.rsqrt(var + RWKV_GN_EPS) * lnw_ref[:, s] + lnb_ref[:, s]
        o_ref[rs, s] = ((on + bonus_all[rs, s]) * gate_all[rs, s]).astype(BF16)

    @pl.when(ci == pl.num_programs(1) - 1)
    def _():
        sout_ref[0] = s_scr[...]


def _rwkv_mixer(p, shift0, s0, wts, b, t, c, nsub):
    rows = c * nsub
    nc = t // rows
    kern = functools.partial(_rwkv_kernel, c=c, nsub=nsub)
    full = lambda a: pl.BlockSpec(a.shape, lambda i, j: (0,) * a.ndim)
    st_spec = pl.BlockSpec((1, RWKV_PAIRS, LANE, LANE), lambda i, j: (i, 0, 0, 0))
    sh_spec = pl.BlockSpec((1, 1, RWKV_COLS), lambda i, j: (i, 0, 0))
    return pl.pallas_call(
        kern,
        grid=(b, nc),
        in_specs=[pl.BlockSpec((rows, RWKV_COLS), lambda i, j: (i * nc + j, 0)), sh_spec, st_spec]
                 + [full(a) for a in wts],
        out_specs=[pl.BlockSpec((rows, RWKV_WIDTH), lambda i, j: (i * nc + j, 0)), sh_spec, st_spec],
        out_shape=[jax.ShapeDtypeStruct((b * t, RWKV_WIDTH), BF16),
                   jax.ShapeDtypeStruct((b, 1, RWKV_COLS), F32),
                   jax.ShapeDtypeStruct((b, RWKV_PAIRS, LANE, LANE), F32)],
        scratch_shapes=[pltpu.VMEM((RWKV_PAIRS, LANE, LANE), F32), pltpu.VMEM((1, RWKV_COLS), F32)],
        compiler_params=_compiler_params(("arbitrary", "arbitrary")),
    )(p, shift0, s0, *wts)


_RWKV_PERM = np.concatenate([np.arange(0, 640), np.arange(704, 1344), np.arange(1344, 1984),
                             np.arange(640, 704), np.arange(1984, 2048), np.arange(2048, 2176)])
_RWKV_INV_PERM = np.argsort(_RWKV_PERM)


def _pad_heads(w, heads, d, dp):
    lead = w.shape[:-1]
    w = w.reshape(lead + (heads, d))
    w = jnp.pad(w, [(0, 0)] * len(lead) + [(0, 0), (0, dp - d)])
    return w.reshape(lead + (heads * dp,))


def _prep_weights(w_in, gla_gate_up, gla_gate_bias, gla_norm, rwkv_mu, rwkv_w_up, rwkv_a_up,
                  rwkv_r_k, w_out):
    gq, gk, gv, gdn, gout, rw, hg = jnp.split(
        w_in, [384, 768, 1536, 1552, 2320, 2320 + RWKV_COLS], axis=-1)
    w_gla = jnp.concatenate([
        _pad_heads(gq, GLA_HEADS, GLA_DK, GLA_DKP), _pad_heads(gk, GLA_HEADS, GLA_DK, GLA_DKP),
        _pad_heads(gv, GLA_HEADS, GLA_DV, GLA_DVP), _pad_heads(gout, GLA_HEADS, GLA_DV, GLA_DVP),
        jnp.pad(gdn, ((0, 0), (0, 0), (0, GLA_RANKP - GLA_RANK)))], axis=-1).astype(BF16)
    w_rwkv = rw[..., _RWKV_PERM].astype(BF16)
    w_hgrn = hg.astype(BF16)
    gup = jnp.pad(_pad_heads(gla_gate_up, GLA_HEADS, GLA_DK, GLA_DKP),
                  ((0, 0), (0, GLA_RANKP - GLA_RANK), (0, 0))).astype(BF16)
    gbias = _pad_heads(gla_gate_bias, GLA_HEADS, GLA_DK, GLA_DKP)[:, None, :]
    gnorm = jnp.pad(gla_norm, ((0, 0), (0, GLA_DVP - GLA_DV)))[:, None, :]
    mu = rwkv_mu[:, _RWKV_PERM][:, None, :]
    wup = jnp.pad(rwkv_w_up, ((0, 0), (0, 64), (0, 0))).astype(BF16)
    aup = jnp.pad(rwkv_a_up, ((0, 0), (64, 0), (0, 0))).astype(BF16)
    rk = rwkv_r_k.reshape(DEPTH, 1, RWKV_WIDTH)
    wo_a = _pad_heads(w_out[:, :768].transpose(0, 2, 1), GLA_HEADS, GLA_DV, GLA_DVP)
    wo_a = wo_a.transpose(0, 2, 1).astype(BF16)
    wo_b = w_out[:, 768:1408].astype(BF16)
    wo_c = w_out[:, 1408:].astype(BF16)
    return w_gla, w_rwkv, w_hgrn, gup, gbias, gnorm, mu, wup, aup, rk, wo_a, wo_b, wo_c


def _gla_state_in(s):
    s = jnp.swapaxes(s, -1, -2)
    return jnp.pad(s, [(0, 0)] * 3 + [(0, GLA_DVP - GLA_DV), (0, GLA_DKP - GLA_DK)])


def _gla_state_out(s):
    return jnp.swapaxes(s[..., :GLA_DV, :GLA_DK], -1, -2)


def _rwkv_state_in(s):
    d, b = s.shape[:2]
    s = s.reshape(d, b, RWKV_PAIRS, 2, RWKV_HD, RWKV_HD)
    out = jnp.einsum('dbjhvk,hg->dbjhvgk', s, jnp.eye(2, dtype=s.dtype))
    return out.reshape(d, b, RWKV_PAIRS, LANE, LANE)


def _rwkv_state_out(s):
    d, b = s.shape[:2]
    s = s.reshape(d, b, RWKV_PAIRS, 2, RWKV_HD, 2, RWKV_HD)
    out = jnp.stack([s[:, :, :, 0, :, 0, :], s[:, :, :, 1, :, 1, :]], axis=3)
    return out.reshape(d, b, RWKV_HEADS, RWKV_HD, RWKV_HD)


def _row_tile(n, cap):
    t = cap
    while n % t:
        t //= 2
    return t


def _trunk(x, s_gla, s_rwkv, s_shift, s_hgrn, wt, b, t):
    n = b * t
    tm = _row_tile(n, 512)
    c_glr = min(128, t)
    c_rwkv = min(64, t)
    n_rwkv = 2 if t % (2 * c_rwkv) == 0 else 1
    new_gla, new_rwkv, new_shift, new_hgrn = [], [], [], []
    for l in range(DEPTH):
        pg = _norm_matmul(x, wt['norm_mix_pre'][l], wt['w_gla'][l], tm, 640)
        pr = _norm_matmul(x, wt['norm_mix_pre'][l], wt['w_rwkv'][l], tm, RWKV_COLS)
        ph = _norm_matmul(x, wt['norm_mix_pre'][l], wt['w_hgrn'][l], tm, 1280)
        oa, g1 = _gla_mixer(pg, s_gla[l], wt['gup'][l], wt['gbias'][l], wt['gnorm'][l], b, t, c_glr)
        rw = [wt[k][l] for k in ('mu', 'w0', 'wup', 'a0', 'aup', 'g_up', 'k_k', 'k_a', 'rk',
                                 'ln_w', 'ln_b')]
        ob, sh1, r1 = _rwkv_mixer(pr, s_shift[l], s_rwkv[l], rw, b, t, c_rwkv, n_rwkv)
        oc, h1 = _hgrn_mixer(ph, s_hgrn[l], wt['lb_logits'], wt['hnorm'][l], b, t, c_glr, l)
        x = _out_proj(oa, ob, oc, wt['wo_a'][l], wt['wo_b'][l], wt['wo_c'][l], x,
                      wt['norm_mix_post'][l], tm)
        x = _ffn(x, wt['norm_ffn_pre'][l], wt['ffn_w_gate'][l], wt['ffn_w_up'][l],
                 wt['ffn_w_down'][l], wt['norm_ffn_post'][l], tm, 512)
        new_gla.append(g1)
        new_rwkv.append(r1)
        new_shift.append(sh1)
        new_hgrn.append(h1)
    return x, jnp.stack(new_gla), jnp.stack(new_rwkv), jnp.stack(new_shift), jnp.stack(new_hgrn)


def _run_path(x, state_gla, state_rwkv, state_shift, state_hgrn, wt):
    b, t, d = x.shape
    y, g, r, sh, h = _trunk(
        x.reshape(b * t, d), _gla_state_in(state_gla), _rwkv_state_in(state_rwkv),
        state_shift[:, :, _RWKV_PERM][:, :, None, :], jnp.swapaxes(state_hgrn, -1, -2), wt, b, t)
    return (y.reshape(b, t, d), _gla_state_out(g), _rwkv_state_out(r),
            sh[:, :, 0, :][:, :, _RWKV_INV_PERM], jnp.swapaxes(h, -1, -2))


def kernel(x_prompt, x_sample, state_gla, state_rwkv, state_rwkv_shift, state_hgrn, norm_mix_pre, norm_mix_post, norm_ffn_pre, norm_ffn_post, w_in, gla_gate_up, gla_gate_bias, gla_norm, rwkv_mu, rwkv_w0, rwkv_w_up, rwkv_a0, rwkv_a_up, rwkv_g_up, rwkv_k_k, rwkv_k_a, rwkv_r_k, rwkv_ln_w, rwkv_ln_b, hgrn_lb_logits, hgrn_norm, w_out, ffn_w_gate, ffn_w_up, ffn_w_down):
    (w_gla, w_rwkv, w_hgrn, gup, gbias, gnorm, mu, wup, aup, rk, wo_a, wo_b, wo_c) = _prep_weights(
        w_in, gla_gate_up, gla_gate_bias, gla_norm, rwkv_mu, rwkv_w_up, rwkv_a_up, rwkv_r_k, w_out)
    vec = lambda a: a[:, None, :]
    wt = dict(
        norm_mix_pre=vec(norm_mix_pre), norm_mix_post=vec(norm_mix_post),
        norm_ffn_pre=vec(norm_ffn_pre), norm_ffn_post=vec(norm_ffn_post),
        w_gla=w_gla, w_rwkv=w_rwkv, w_hgrn=w_hgrn, gup=gup, gbias=gbias, gnorm=gnorm,
        mu=mu, w0=vec(rwkv_w0), wup=wup, a0=vec(rwkv_a0), aup=aup, g_up=rwkv_g_up.astype(BF16),
        k_k=vec(rwkv_k_k), k_a=vec(rwkv_k_a), rk=rk, ln_w=vec(rwkv_ln_w), ln_b=vec(rwkv_ln_b),
        lb_logits=hgrn_lb_logits, hnorm=vec(hgrn_norm),
        wo_a=wo_a, wo_b=wo_b, wo_c=wo_c,
        ffn_w_gate=ffn_w_gate.astype(BF16), ffn_w_up=ffn_w_up.astype(BF16),
        ffn_w_down=ffn_w_down.astype(BF16))

    bp = x_prompt.shape[0]
    zeros = lambda s: jnp.zeros((DEPTH, bp) + s.shape[2:], x_prompt.dtype)
    y_p, gla_p, rwkv_p, shift_p, hgrn_p = _run_path(
        x_prompt, zeros(state_gla), zeros(state_rwkv), zeros(state_rwkv_shift), zeros(state_hgrn), wt)
    y_s, gla_s, rwkv_s, shift_s, hgrn_s = _run_path(
        x_sample, state_gla, state_rwkv, state_rwkv_shift, state_hgrn, wt)
    return (y_p, y_s, gla_p, rwkv_p, shift_p, hgrn_p, gla_s, rwkv_s, shift_s, hgrn_s)
```

```python
import functools

import jax
import jax.numpy as jnp
import numpy as np
from jax import lax
from jax.experimental import pallas as pl
from jax.experimental.pallas import tpu as pltpu

F32 = jnp.float32
BF16 = jnp.bfloat16

D_MODEL = 2048
DEPTH = 4
EPS = 1e-6

GLA_HEADS, GLA_DK, GLA_DV = 4, 96, 192
GLA_DKP, GLA_DVP = 128, 256
GLA_RANK, GLA_RANKP = 16, 128
GLA_GATE_NORMALIZER = 16.0
GLA_QW = GLA_HEADS * GLA_DKP
GLA_VW = GLA_HEADS * GLA_DVP
GLA_COLS_P = 2 * GLA_QW + 2 * GLA_VW + GLA_RANKP

RWKV_HEADS, RWKV_HD = 10, 64
RWKV_WIDTH = RWKV_HEADS * RWKV_HD
RWKV_PAIRS = RWKV_HEADS // 2
RWKV_COLS = 2176
RWKV_GN_EPS = 64e-5

HGRN_HEADS, HGRN_D = 5, 128
HGRN_WIDTH = HGRN_HEADS * HGRN_D
HGRN_COLS = 4 * HGRN_WIDTH

D_FF = 5632
LANE = 128
DIAG = 8

VMEM_LIMIT = 56 * 1024 * 1024


def _sigmoid(x):
    return 1.0 / (1.0 + jnp.exp(-x))


def _silu(x):
    return x * _sigmoid(x)


def _log_sigmoid(x):
    return jnp.minimum(x, 0.0) - jnp.log(1.0 + jnp.exp(-jnp.abs(x)))


def _dot(a, b):
    return jnp.dot(a.astype(BF16), b.astype(BF16), preferred_element_type=F32)


def _dot_nt(a, b):
    return lax.dot_general(a.astype(BF16), b.astype(BF16), (((1,), (1,)), ((), ())),
                           preferred_element_type=F32)


def _dot_tn(a, b):
    return lax.dot_general(a.astype(BF16), b.astype(BF16), (((0,), (0,)), ((), ())),
                           preferred_element_type=F32)


def _split_bf16(x, parts):
    out = []
    r = x
    for i in range(parts):
        p = r.astype(BF16)
        out.append(p)
        if i + 1 < parts:
            r = r - p.astype(F32)
    return out


def _dot_exact_lhs(m_bf16, x, parts=3):
    acc = None
    for p in _split_bf16(x, parts):
        t = jnp.dot(m_bf16, p, preferred_element_type=F32)
        acc = t if acc is None else acc + t
    return acc


def _dot_exact_rhs(x, m_bf16, parts=2):
    acc = None
    for p in _split_bf16(x, parts):
        t = jnp.dot(p, m_bf16, preferred_element_type=F32)
        acc = t if acc is None else acc + t
    return acc


def _iota(shape, dim):
    return lax.broadcasted_iota(jnp.int32, shape, dim)


def _tril_ones(c):
    return jnp.where(_iota((c, c), 0) >= _iota((c, c), 1), 1.0, 0.0).astype(BF16)


def _rms_rows(x, g):
    ms = jnp.mean(x * x, axis=-1, keepdims=True)
    return x * lax.rsqrt(ms + EPS) * g


def _compiler_params(sem):
    return pltpu.CompilerParams(dimension_semantics=sem, vmem_limit_bytes=VMEM_LIMIT)


def _norm_kernel(x_ref, g_ref, h_ref):
    h_ref[...] = _rms_rows(x_ref[...], g_ref[...]).astype(BF16)


def _norm(x, g, tm):
    n, d = x.shape
    return pl.pallas_call(
        _norm_kernel,
        grid=(n // tm,),
        in_specs=[pl.BlockSpec((tm, d), lambda i: (i, 0)), pl.BlockSpec((1, d), lambda i: (0, 0))],
        out_specs=pl.BlockSpec((tm, d), lambda i: (i, 0)),
        out_shape=jax.ShapeDtypeStruct((n, d), BF16),
        compiler_params=_compiler_params(("arbitrary",)),
    )(x, g)


def _matmul_kernel(h_ref, w_ref, o_ref):
    o_ref[...] = jnp.dot(h_ref[...], w_ref[...], preferred_element_type=F32)


def _matmul(h, w, tm):
    n, d = h.shape
    cout = w.shape[1]
    return pl.pallas_call(
        _matmul_kernel,
        grid=(n // tm,),
        in_specs=[pl.BlockSpec((tm, d), lambda i: (i, 0)),
                  pl.BlockSpec((d, cout), lambda i: (0, 0), pipeline_mode=pl.Buffered(1))],
        out_specs=pl.BlockSpec((tm, cout), lambda i: (i, 0)),
        out_shape=jax.ShapeDtypeStruct((n, cout), F32),
        compiler_params=_compiler_params(("arbitrary",)),
    )(h, w)


def _out_proj_kernel(oa_ref, ob_ref, oc_ref, wa_ref, wb_ref, wc_ref, x_ref, g_ref, gnext_ref,
                     y_ref, hn_ref):
    m = (jnp.dot(oa_ref[...], wa_ref[...], preferred_element_type=F32)
         + jnp.dot(ob_ref[...], wb_ref[...], preferred_element_type=F32)
         + jnp.dot(oc_ref[...], wc_ref[...], preferred_element_type=F32))
    y = x_ref[...] + _rms_rows(m, g_ref[...])
    y_ref[...] = y
    hn_ref[...] = _rms_rows(y, gnext_ref[...]).astype(BF16)


def _out_proj(oa, ob, oc, wa, wb, wc, x, g, gnext, tm):
    n, d = x.shape
    row = lambda w: pl.BlockSpec((tm, w), lambda i: (i, 0))
    full = lambda a: pl.BlockSpec(a.shape, lambda i: (0, 0))
    return pl.pallas_call(
        _out_proj_kernel,
        grid=(n // tm,),
        in_specs=[row(oa.shape[1]), row(ob.shape[1]), row(oc.shape[1]),
                  full(wa), full(wb), full(wc), row(d), full(g), full(gnext)],
        out_specs=[row(d), row(d)],
        out_shape=[jax.ShapeDtypeStruct((n, d), F32), jax.ShapeDtypeStruct((n, d), BF16)],
        compiler_params=_compiler_params(("arbitrary",)),
    )(oa, ob, oc, wa, wb, wc, x, g, gnext)


def _ffn_kernel(h_ref, x_ref, wg_ref, wu_ref, wd_ref, gpost_ref, gnext_ref, y_ref, hn_ref, acc_scr):
    j = pl.program_id(1)

    @pl.when(j == 0)
    def _():
        acc_scr[...] = jnp.zeros_like(acc_scr)

    h = h_ref[...]
    a = jnp.dot(h, wg_ref[...], preferred_element_type=F32)
    u = jnp.dot(h, wu_ref[...], preferred_element_type=F32)
    acc_scr[...] += jnp.dot((_silu(a) * u).astype(BF16), wd_ref[...], preferred_element_type=F32)

    @pl.when(j == pl.num_programs(1) - 1)
    def _():
        y = x_ref[...] + _rms_rows(acc_scr[...], gpost_ref[...])
        y_ref[...] = y
        hn_ref[...] = _rms_rows(y, gnext_ref[...]).astype(BF16)


def _ffn(h, x, wg, wu, wd, gpost, gnext, tm, tf):
    n, d = x.shape
    dff = wg.shape[1]
    return pl.pallas_call(
        _ffn_kernel,
        grid=(n // tm, dff // tf),
        in_specs=[pl.BlockSpec((tm, d), lambda i, j: (i, 0)),
                  pl.BlockSpec((tm, d), lambda i, j: (i, 0)),
                  pl.BlockSpec((d, tf), lambda i, j: (0, j)),
                  pl.BlockSpec((d, tf), lambda i, j: (0, j)),
                  pl.BlockSpec((tf, d), lambda i, j: (j, 0)),
                  pl.BlockSpec((1, d), lambda i, j: (0, 0)),
                  pl.BlockSpec((1, d), lambda i, j: (0, 0))],
        out_specs=[pl.BlockSpec((tm, d), lambda i, j: (i, 0)),
                   pl.BlockSpec((tm, d), lambda i, j: (i, 0))],
        out_shape=[jax.ShapeDtypeStruct((n, d), F32), jax.ShapeDtypeStruct((n, d), BF16)],
        scratch_shapes=[pltpu.VMEM((tm, d), F32)],
        compiler_params=_compiler_params(("arbitrary", "arbitrary")),
    )(h, x, wg, wu, wd, gpost, gnext)


class _GlrMasks:
    def __init__(self, c, dk):
        self.c = c
        row_k = _iota((c, dk), 0)
        row_a = _iota((c, c), 0)
        col_a = _iota((c, c), 1)
        self.tril = _tril_ones(c)
        self.levels = []
        h = DIAG
        while 2 * h <= c:
            is_q = (row_k // h) % 2 == 1
            same = (row_a // (2 * h)) == (col_a // (2 * h))
            self.levels.append((h, is_q, same))
            h *= 2
        self.diag = []
        for d in range(min(DIAG, c)):
            self.diag.append((col_a == row_a - d) & ((row_a % DIAG) >= d))


def _glr_chunk(q, k, v, g, st, mk):
    c, dk = q.shape
    b = _dot_exact_lhs(mk.tril, g)
    att = jnp.zeros((c, c), F32)
    for h, is_q, same in mk.levels:
        n = c // (2 * h)
        ref = b.reshape(n, 2 * h, dk)[:, h - 1:h, :]
        refb = jnp.broadcast_to(ref, (n, 2 * h, dk)).reshape(c, dk)
        e = jnp.exp(jnp.where(is_q, b - refb, refb - b))
        qh = jnp.where(is_q, q * e, 0.0)
        kh = jnp.where(is_q, 0.0, k * e)
        att = att + jnp.where(same, _dot_nt(qh, kh), 0.0)
    for d, valid in enumerate(mk.diag):
        if d == 0:
            p = q * k
        else:
            ks = pltpu.roll(k, d, 0)
            bs = pltpu.roll(b, d, 0)
            p = q * ks * jnp.exp(jnp.minimum(b - bs, 0.0))
        att = jnp.where(valid, jnp.sum(p, axis=-1, keepdims=True), att)
    o = _dot(att, v) + _dot_nt(q * jnp.exp(b), st)
    b_end = b[c - 1:c, :]
    st_new = st * jnp.exp(b_end) + _dot_tn(v, k * jnp.exp(b_end - b))
    return o, st_new


def _gla_kernel(p_ref, s0_ref, gup_ref, gbias_ref, gnorm_ref, o_ref, sout_ref, s_scr, *, c):
    ci = pl.program_id(1)

    @pl.when(ci == 0)
    def _():
        s_scr[...] = s0_ref[0]

    mk = _GlrMasks(c, GLA_DKP)
    p = p_ref[...]
    q_all = p[:, 0:GLA_QW] * (GLA_DK ** -0.5)
    k_all = p[:, GLA_QW:2 * GLA_QW]
    v_all = p[:, 2 * GLA_QW:2 * GLA_QW + GLA_VW]
    gout = p[:, 2 * GLA_QW + GLA_VW:2 * GLA_QW + 2 * GLA_VW]
    gdown = p[:, 2 * GLA_QW + 2 * GLA_VW:]
    g_all = _log_sigmoid(_dot(gdown, gup_ref[...]) + gbias_ref[...]) / GLA_GATE_NORMALIZER
    gnorm = gnorm_ref[...]
    for h in range(GLA_HEADS):
        ks = slice(h * GLA_DKP, (h + 1) * GLA_DKP)
        vs = slice(h * GLA_DVP, (h + 1) * GLA_DVP)
        o, st = _glr_chunk(q_all[:, ks], k_all[:, ks], v_all[:, vs], g_all[:, ks], s_scr[h], mk)
        s_scr[h] = st
        ms = jnp.sum(o * o, axis=-1, keepdims=True) * (1.0 / GLA_DV)
        o = o * lax.rsqrt(ms + EPS) * gnorm * _silu(gout[:, vs])
        o_ref[:, vs] = o.astype(BF16)

    @pl.when(ci == pl.num_programs(1) - 1)
    def _():
        sout_ref[0] = s_scr[...]


def _gla_mixer(p, s0t, gup, gbias, gnorm, b, t, c):
    nc = t // c
    kern = functools.partial(_gla_kernel, c=c)
    full = lambda a: pl.BlockSpec(a.shape, lambda i, j: (0,) * a.ndim)
    st_spec = pl.BlockSpec((1, GLA_HEADS, GLA_DVP, GLA_DKP), lambda i, j: (i, 0, 0, 0))
    return pl.pallas_call(
        kern,
        grid=(b, nc),
        in_specs=[pl.BlockSpec((c, GLA_COLS_P), lambda i, j: (i * nc + j, 0)),
                  st_spec, full(gup), full(gbias), full(gnorm)],
        out_specs=[pl.BlockSpec((c, GLA_VW), lambda i, j: (i * nc + j, 0)), st_spec],
        out_shape=[jax.ShapeDtypeStruct((b * t, GLA_VW), BF16),
                   jax.ShapeDtypeStruct((b, GLA_HEADS, GLA_DVP, GLA_DKP), F32)],
        scratch_shapes=[pltpu.VMEM((GLA_HEADS, GLA_DVP, GLA_DKP), F32)],
        compiler_params=_compiler_params(("arbitrary", "arbitrary")),
    )(p, s0t, gup, gbias, gnorm)


def _hgrn_kernel(p_ref, s0_ref, lbl_ref, hnorm_ref, o_ref, sout_ref, s_scr, *, c, layer):
    ci = pl.program_id(1)

    @pl.when(ci == 0)
    def _():
        s_scr[...] = s0_ref[0]

    lg = lbl_ref[...]
    e = jnp.exp(lg - jnp.max(lg, axis=0, keepdims=True))
    prob = e / jnp.sum(e, axis=0, keepdims=True)
    lb = jnp.zeros((1, HGRN_WIDTH), F32)
    for i in range(1, layer + 1):
        lb = lb + prob[i:i + 1, :]

    mk = _GlrMasks(c, HGRN_D)
    p = p_ref[...]
    hq = p[:, 0:HGRN_WIDTH]
    hf = p[:, HGRN_WIDTH:2 * HGRN_WIDTH]
    hi = p[:, 2 * HGRN_WIDTH:3 * HGRN_WIDTH]
    hg = p[:, 3 * HGRN_WIDTH:]
    q_all = _silu(hq)
    g_all = jnp.log(lb + (1.0 - lb) * _sigmoid(hf))
    k_all = (1.0 - lb) * _sigmoid(-hf)
    hnorm = hnorm_ref[...]
    for h in range(HGRN_HEADS):
        s = slice(h * HGRN_D, (h + 1) * HGRN_D)
        o, st = _glr_chunk(q_all[:, s], k_all[:, s], hi[:, s], g_all[:, s], s_scr[h], mk)
        s_scr[h] = st
        o = _rms_rows(o, hnorm) * _silu(hg[:, s])
        o_ref[:, s] = o.astype(BF16)

    @pl.when(ci == pl.num_programs(1) - 1)
    def _():
        sout_ref[0] = s_scr[...]


def _hgrn_mixer(p, s0t, lb_logits, hnorm, b, t, c, layer):
    nc = t // c
    kern = functools.partial(_hgrn_kernel, c=c, layer=layer)
    full = lambda a: pl.BlockSpec(a.shape, lambda i, j: (0,) * a.ndim)
    st_spec = pl.BlockSpec((1, HGRN_HEADS, HGRN_D, HGRN_D), lambda i, j: (i, 0, 0, 0))
    return pl.pallas_call(
        kern,
        grid=(b, nc),
        in_specs=[pl.BlockSpec((c, HGRN_COLS), lambda i, j: (i * nc + j, 0)),
                  st_spec, full(lb_logits), full(hnorm)],
        out_specs=[pl.BlockSpec((c, HGRN_WIDTH), lambda i, j: (i * nc + j, 0)), st_spec],
        out_shape=[jax.ShapeDtypeStruct((b * t, HGRN_WIDTH), BF16),
                   jax.ShapeDtypeStruct((b, HGRN_HEADS, HGRN_D, HGRN_D), F32)],
        scratch_shapes=[pltpu.VMEM((HGRN_HEADS, HGRN_D, HGRN_D), F32)],
        compiler_params=_compiler_params(("arbitrary", "arbitrary")),
    )(p, s0t, lb_logits, hnorm)


def _rwkv_kernel(p_ref, shift0_ref, s0_ref, mu_ref, w0_ref, wup_ref, a0_ref, aup_ref, gup_ref,
                 kk_ref, ka_ref, rk_ref, lnw_ref, lnb_ref,
                 o_ref, shift_ref, sout_ref, s_scr, carry_scr, *, c, nsub):
    ci = pl.program_id(1)

    @pl.when(ci == 0)
    def _():
        s_scr[...] = s0_ref[0]
        carry_scr[...] = shift0_ref[0]

    w = RWKV_WIDTH
    rows = c * nsub
    p = p_ref[...]
    row = _iota((rows, RWKV_COLS), 0)
    prev = jnp.where(row == 0, carry_scr[...], pltpu.roll(p, 1, 0))
    last = p[rows - 1:rows, :]
    carry_scr[...] = last
    shift_ref[0] = last
    pm = p + (prev - p) * mu_ref[...]
    r_all = pm[:, 0:w]
    k_all = pm[:, w:2 * w]
    v_all = pm[:, 2 * w:3 * w]
    wa = pm[:, 3 * w:3 * w + LANE]
    gd = pm[:, 3 * w + LANE:]
    lw_all = -jnp.exp(_log_sigmoid(w0_ref[...] + _dot(jnp.tanh(wa), wup_ref[...])) - 0.5)
    a_all = _sigmoid(a0_ref[...] + _dot(wa, aup_ref[...]))
    gate_all = _dot(_sigmoid(gd), gup_ref[...])

    c2 = 2 * c
    lane = _iota((c, LANE), 1)
    lo_half = lane < RWKV_HD
    ri = _iota((c2, c2), 0)
    cj = _iota((c2, c2), 1)
    strict = (ri % c) > (cj % c)
    incl = (ri % c) >= (cj % c)
    eye = jnp.where(ri == cj, 1.0, 0.0)
    inv_levels = []
    blk = 1
    while blk < c:
        inv_levels.append(((ri // (2 * blk)) == (cj // (2 * blk)))
                          & ((ri // blk) % 2 == 1) & ((cj // blk) % 2 == 0))
        blk *= 2
    tr, tc = _iota((rows, rows), 0), _iota((rows, rows), 1)
    tril = jnp.where((tr >= tc) & (tr // c == tc // c), 1.0, 0.0).astype(BF16)
    seg = jnp.where((_iota((LANE, LANE), 0) // RWKV_HD) == (_iota((LANE, LANE), 1) // RWKV_HD),
                    1.0, 0.0).astype(BF16)

    def stack(x):
        return jnp.concatenate([jnp.where(lo_half, x, 0.0), jnp.where(lo_half, 0.0, x)], axis=0)

    def head_sum(x):
        return jnp.concatenate([_dot_exact_rhs(x[:, j * LANE:(j + 1) * LANE], seg)
                                for j in range(RWKV_PAIRS)], axis=1)

    kk_all = k_all * kk_ref[...]
    kk_all = kk_all * lax.rsqrt(jnp.maximum(head_sum(kk_all * kk_all), 1e-24))
    k_all = k_all * (1.0 + (a_all - 1.0) * ka_ref[...])
    beta_all = kk_all * a_all
    bonus_all = head_sum(r_all * k_all * rk_ref[...]) * v_all
    cin_all = _dot_exact_lhs(tril, lw_all)
    ginv = jnp.exp(-cin_all)
    kap_all = kk_all * jnp.exp(cin_all - lw_all)
    rt_all = r_all * jnp.exp(cin_all)
    kt_all = k_all * ginv
    bt_all = beta_all * ginv
    kend_all, bend_all, dec_all = [], [], []
    for i in range(nsub):
        rs = slice(i * c, (i + 1) * c)
        cend = cin_all[(i + 1) * c - 1:(i + 1) * c, :]
        gend = jnp.exp(cend - cin_all[rs])
        kend_all.append(k_all[rs] * gend)
        bend_all.append(beta_all[rs] * gend)
        dec_all.append(jnp.exp(cend))
    pairs = []
    for i, j in [(i, j) for i in range(nsub) for j in range(RWKV_PAIRS)]:
        s = slice(j * LANE, (j + 1) * LANE)
        rs = slice(i * c, (i + 1) * c)
        pairs.append(dict(
            s=s, rs=rs, kap_s=stack(kap_all[rs, s]), r_s=stack(rt_all[rs, s]),
            k_s=stack(kt_all[rs, s]), b_s=stack(bt_all[rs, s]), v_s=stack(v_all[rs, s]),
            kend_s=stack(kend_all[i][:, s]), bend_s=stack(bend_all[i][:, s]), dec=dec_all[i][:, s]))
    for pr in pairs:
        pr['a2'] = jnp.where(strict, _dot_nt(pr['kap_s'], pr['b_s']), 0.0)
    for pr in pairs:
        pr['a1'] = jnp.where(strict, _dot_nt(pr['kap_s'], pr['k_s']), 0.0)
        pr['a3'] = jnp.where(incl, _dot_nt(pr['r_s'], pr['k_s']), 0.0)
        pr['a4'] = jnp.where(incl, _dot_nt(pr['r_s'], pr['b_s']), 0.0)
    for pr in pairs:
        pr['a1v'] = _dot(pr['a1'], pr['v_s'])
        pr['a3v'] = _dot(pr['a3'], pr['v_s'])
        pr['kv'] = _dot_tn(pr['v_s'], pr['kend_s'])
    tinv = [eye - jnp.where(inv_levels[0], pr['a2'], 0.0) for pr in pairs]
    for m in inv_levels[1:]:
        half = [_dot(t, jnp.where(m, pr['a2'], 0.0)) for t, pr in zip(tinv, pairs)]
        tinv = [t - _dot(hf, t) for t, hf in zip(tinv, half)]
    for pr, tj in zip(pairs, tinv):
        pr['tk'] = _dot(tj, pr['kap_s'])
        pr['tv'] = _dot(tj, pr['a1v'])
    for pr in pairs:
        pr['reff'] = pr['r_s'] - _dot(pr['a4'], pr['tk'])
        pr['oc'] = pr['a3v'] - _dot(pr['a4'], pr['tv'])
        pr['m'] = _dot_tn(pr['tk'], pr['bend_s'])
        pr['q'] = pr['kv'] - _dot_tn(pr['tv'], pr['bend_s'])
    sts = [s_scr[j] for j in range(RWKV_PAIRS)]
    os_ = []
    for i in range(nsub):
        sub = list(zip(pairs[i * RWKV_PAIRS:(i + 1) * RWKV_PAIRS], sts))
        o2s = [_dot_nt(pr['reff'], st) + pr['oc'] for pr, st in sub]
        sts = [st * pr['dec'] - _dot(st, pr['m']) + pr['q'] for pr, st in sub]
        os_ += [o2[0:c] + o2[c:c2] for o2 in o2s]
    for j, st in enumerate(sts):
        s_scr[j] = st
    ds = [o - _dot_exact_rhs(o, seg) * (1.0 / RWKV_HD) for o in os_]
    vars_ = [_dot_exact_rhs(d * d, seg) * (1.0 / RWKV_HD) for d in ds]
    for pr, d, var in zip(pairs, ds, vars_):
        s, rs = pr['s'], pr['rs']
        on = d * lax.rsqrt(var + RWKV_GN_EPS) * lnw_ref[:, s] + lnb_ref[:, s]
        o_ref[rs, s] = ((on + bonus_all[rs, s]) * gate_all[rs, s]).astype(BF16)

    @pl.when(ci == pl.num_programs(1) - 1)
    def _():
        sout_ref[0] = s_scr[...]


def _rwkv_mixer(p, shift0, s0, wts, b, t, c, nsub):
    rows = c * nsub
    nc = t // rows
    kern = functools.partial(_rwkv_kernel, c=c, nsub=nsub)
    full = lambda a: pl.BlockSpec(a.shape, lambda i, j: (0,) * a.ndim)
    st_spec = pl.BlockSpec((1, RWKV_PAIRS, LANE, LANE), lambda i, j: (i, 0, 0, 0))
    sh_spec = pl.BlockSpec((1, 1, RWKV_COLS), lambda i, j: (i, 0, 0))
    return pl.pallas_call(
        kern,
        grid=(b, nc),
        in_specs=[pl.BlockSpec((rows, RWKV_COLS), lambda i, j: (i * nc + j, 0)), sh_spec, st_spec]
                 + [full(a) for a in wts],
        out_specs=[pl.BlockSpec((rows, RWKV_WIDTH), lambda i, j: (i * nc + j, 0)), sh_spec, st_spec],
        out_shape=[jax.ShapeDtypeStruct((b * t, RWKV_WIDTH), BF16),
                   jax.ShapeDtypeStruct((b, 1, RWKV_COLS), F32),
                   jax.ShapeDtypeStruct((b, RWKV_PAIRS, LANE, LANE), F32)],
        scratch_shapes=[pltpu.VMEM((RWKV_PAIRS, LANE, LANE), F32), pltpu.VMEM((1, RWKV_COLS), F32)],
        compiler_params=_compiler_params(("arbitrary", "arbitrary")),
    )(p, shift0, s0, *wts)


_RWKV_PERM = np.concatenate([np.arange(0, 640), np.arange(704, 1344), np.arange(1344, 1984),
                             np.arange(640, 704), np.arange(1984, 2048), np.arange(2048, 2176)])
_RWKV_INV_PERM = np.argsort(_RWKV_PERM)


def _pad_heads(w, heads, d, dp):
    lead = w.shape[:-1]
    w = w.reshape(lead + (heads, d))
    w = jnp.pad(w, [(0, 0)] * len(lead) + [(0, 0), (0, dp - d)])
    return w.reshape(lead + (heads * dp,))


def _prep_weights(w_in, gla_gate_up, gla_gate_bias, gla_norm, rwkv_mu, rwkv_w_up, rwkv_a_up,
                  rwkv_r_k, w_out):
    gq, gk, gv, gdn, gout, rw, hg = jnp.split(
        w_in, [384, 768, 1536, 1552, 2320, 2320 + RWKV_COLS], axis=-1)
    w_gla = jnp.concatenate([
        _pad_heads(gq, GLA_HEADS, GLA_DK, GLA_DKP), _pad_heads(gk, GLA_HEADS, GLA_DK, GLA_DKP),
        _pad_heads(gv, GLA_HEADS, GLA_DV, GLA_DVP), _pad_heads(gout, GLA_HEADS, GLA_DV, GLA_DVP),
        jnp.pad(gdn, ((0, 0), (0, 0), (0, GLA_RANKP - GLA_RANK)))], axis=-1).astype(BF16)
    w_rwkv = rw[..., _RWKV_PERM].astype(BF16)
    w_hgrn = hg.astype(BF16)
    gup = jnp.pad(_pad_heads(gla_gate_up, GLA_HEADS, GLA_DK, GLA_DKP),
                  ((0, 0), (0, GLA_RANKP - GLA_RANK), (0, 0))).astype(BF16)
    gbias = _pad_heads(gla_gate_bias, GLA_HEADS, GLA_DK, GLA_DKP)[:, None, :]
    gnorm = jnp.pad(gla_norm, ((0, 0), (0, GLA_DVP - GLA_DV)))[:, None, :]
    mu = rwkv_mu[:, _RWKV_PERM][:, None, :]
    wup = jnp.pad(rwkv_w_up, ((0, 0), (0, 64), (0, 0))).astype(BF16)
    aup = jnp.pad(rwkv_a_up, ((0, 0), (64, 0), (0, 0))).astype(BF16)
    rk = rwkv_r_k.reshape(DEPTH, 1, RWKV_WIDTH)
    wo_a = _pad_heads(w_out[:, :768].transpose(0, 2, 1), GLA_HEADS, GLA_DV, GLA_DVP)
    wo_a = wo_a.transpose(0, 2, 1).astype(BF16)
    wo_b = w_out[:, 768:1408].astype(BF16)
    wo_c = w_out[:, 1408:].astype(BF16)
    return w_gla, w_rwkv, w_hgrn, gup, gbias, gnorm, mu, wup, aup, rk, wo_a, wo_b, wo_c


def _gla_state_in(s):
    s = jnp.swapaxes(s, -1, -2)
    return jnp.pad(s, [(0, 0)] * 3 + [(0, GLA_DVP - GLA_DV), (0, GLA_DKP - GLA_DK)])


def _gla_state_out(s):
    return jnp.swapaxes(s[..., :GLA_DV, :GLA_DK], -1, -2)


def _rwkv_state_in(s):
    d, b = s.shape[:2]
    s = s.reshape(d, b, RWKV_PAIRS, 2, RWKV_HD, RWKV_HD)
    out = jnp.einsum('dbjhvk,hg->dbjhvgk', s, jnp.eye(2, dtype=s.dtype))
    return out.reshape(d, b, RWKV_PAIRS, LANE, LANE)


def _rwkv_state_out(s):
    d, b = s.shape[:2]
    s = s.reshape(d, b, RWKV_PAIRS, 2, RWKV_HD, 2, RWKV_HD)
    out = jnp.stack([s[:, :, :, 0, :, 0, :], s[:, :, :, 1, :, 1, :]], axis=3)
    return out.reshape(d, b, RWKV_HEADS, RWKV_HD, RWKV_HD)


def _row_tile(n, cap):
    t = cap
    while n % t:
        t //= 2
    return t


def _trunk(x, s_gla, s_rwkv, s_shift, s_hgrn, wt, b, t):
    n = b * t
    tm = _row_tile(n, 512)
    c_glr = min(128, t)
    c_rwkv = min(64, t)
    n_rwkv = 2 if t % (2 * c_rwkv) == 0 else 1
    new_gla, new_rwkv, new_shift, new_hgrn = [], [], [], []
    h = _norm(x, wt['norm_mix_pre'][0], tm)
    for l in range(DEPTH):
        pg = _matmul(h, wt['w_gla'][l], tm)
        pr = _matmul(h, wt['w_rwkv'][l], tm)
        ph = _matmul(h, wt['w_hgrn'][l], tm)
        oa, g1 = _gla_mixer(pg, s_gla[l], wt['gup'][l], wt['gbias'][l], wt['gnorm'][l], b, t, c_glr)
        rw = [wt[k][l] for k in ('mu', 'w0', 'wup', 'a0', 'aup', 'g_up', 'k_k', 'k_a', 'rk',
                                 'ln_w', 'ln_b')]
        ob, sh1, r1 = _rwkv_mixer(pr, s_shift[l], s_rwkv[l], rw, b, t, c_rwkv, n_rwkv)
        oc, h1 = _hgrn_mixer(ph, s_hgrn[l], wt['lb_logits'], wt['hnorm'][l], b, t, c_glr, l)
        x, h = _out_proj(oa, ob, oc, wt['wo_a'][l], wt['wo_b'][l], wt['wo_c'][l], x,
                         wt['norm_mix_post'][l], wt['norm_ffn_pre'][l], tm)
        x, h = _ffn(h, x, wt['ffn_w_gate'][l], wt['ffn_w_up'][l], wt['ffn_w_down'][l],
                    wt['norm_ffn_post'][l], wt['norm_mix_pre'][(l + 1) % DEPTH], tm, 512)
        new_gla.append(g1)
        new_rwkv.append(r1)
        new_shift.append(sh1)
        new_hgrn.append(h1)
    return x, jnp.stack(new_gla), jnp.stack(new_rwkv), jnp.stack(new_shift), jnp.stack(new_hgrn)


def _run_path(x, state_gla, state_rwkv, state_shift, state_hgrn, wt):
    b, t, d = x.shape
    y, g, r, sh, h = _trunk(
        x.reshape(b * t, d), _gla_state_in(state_gla), _rwkv_state_in(state_rwkv),
        state_shift[:, :, _RWKV_PERM][:, :, None, :], jnp.swapaxes(state_hgrn, -1, -2), wt, b, t)
    return (y.reshape(b, t, d), _gla_state_out(g), _rwkv_state_out(r),
            sh[:, :, 0, :][:, :, _RWKV_INV_PERM], jnp.swapaxes(h, -1, -2))


def kernel(x_prompt, x_sample, state_gla, state_rwkv, state_rwkv_shift, state_hgrn, norm_mix_pre, norm_mix_post, norm_ffn_pre, norm_ffn_post, w_in, gla_gate_up, gla_gate_bias, gla_norm, rwkv_mu, rwkv_w0, rwkv_w_up, rwkv_a0, rwkv_a_up, rwkv_g_up, rwkv_k_k, rwkv_k_a, rwkv_r_k, rwkv_ln_w, rwkv_ln_b, hgrn_lb_logits, hgrn_norm, w_out, ffn_w_gate, ffn_w_up, ffn_w_down):
    (w_gla, w_rwkv, w_hgrn, gup, gbias, gnorm, mu, wup, aup, rk, wo_a, wo_b, wo_c) = _prep_weights(
        w_in, gla_gate_up, gla_gate_bias, gla_norm, rwkv_mu, rwkv_w_up, rwkv_a_up, rwkv_r_k, w_out)
    vec = lambda a: a[:, None, :]
    wt = dict(
        norm_mix_pre=vec(norm_mix_pre), norm_mix_post=vec(norm_mix_post),
        norm_ffn_pre=vec(norm_ffn_pre), norm_ffn_post=vec(norm_ffn_post),
        w_gla=w_gla, w_rwkv=w_rwkv, w_hgrn=w_hgrn, gup=gup, gbias=gbias, gnorm=gnorm,
        mu=mu, w0=vec(rwkv_w0), wup=wup, a0=vec(rwkv_a0), aup=aup, g_up=rwkv_g_up.astype(BF16),
        k_k=vec(rwkv_k_k), k_a=vec(rwkv_k_a), rk=rk, ln_w=vec(rwkv_ln_w), ln_b=vec(rwkv_ln_b),
        lb_logits=hgrn_lb_logits, hnorm=vec(hgrn_norm),
        wo_a=wo_a, wo_b=wo_b, wo_c=wo_c,
        ffn_w_gate=ffn_w_gate.astype(BF16), ffn_w_up=ffn_w_up.astype(BF16),
        ffn_w_down=ffn_w_down.astype(BF16))

    bp = x_prompt.shape[0]
    zeros = lambda s: jnp.zeros((DEPTH, bp) + s.shape[2:], x_prompt.dtype)
    y_p, gla_p, rwkv_p, shift_p, hgrn_p = _run_path(
        x_prompt, zeros(state_gla), zeros(state_rwkv), zeros(state_rwkv_shift), zeros(state_hgrn), wt)
    y_s, gla_s, rwkv_s, shift_s, hgrn_s = _run_path(
        x_sample, state_gla, state_rwkv, state_rwkv_shift, state_hgrn, wt)
    return (y_p, y_s, gla_p, rwkv_p, shift_p, hgrn_p, gla_s, rwkv_s, shift_s, hgrn_s)
```

```python
import functools

import jax
import jax.numpy as jnp
from jax import lax
from jax.experimental import pallas as pl
from jax.experimental.pallas import tpu as pltpu

F32 = jnp.float32
BF16 = jnp.bfloat16

D_MODEL = 2048
DEPTH = 4
EPS = 1e-6

GLA_HEADS, GLA_DK, GLA_DV = 4, 96, 192
GLA_DKP, GLA_DVP = 128, 256
GLA_RANK, GLA_RANKP = 16, 128
GLA_GATE_NORMALIZER = 16.0
GLA_QW = GLA_HEADS * GLA_DKP
GLA_VW = GLA_HEADS * GLA_DVP
GLA_COLS_P = 2 * GLA_QW + 2 * GLA_VW + GLA_RANKP

RWKV_HEADS, RWKV_HD = 10, 64
RWKV_WIDTH = RWKV_HEADS * RWKV_HD
RWKV_PAIRS = RWKV_HEADS // 2
RWKV_COLS = 2176
RWKV_GN_EPS = 64e-5

HGRN_HEADS, HGRN_D = 5, 128
HGRN_WIDTH = HGRN_HEADS * HGRN_D
HGRN_COLS = 4 * HGRN_WIDTH

D_FF = 5632
LANE = 128
LOG2_E = 1.4426950408889634

VMEM_LIMIT = 56 * 1024 * 1024


def _sigmoid(x):
    return 1.0 / (1.0 + jnp.exp(-x))


def _silu(x):
    return x * _sigmoid(x)


def _log_sigmoid(x):
    return jnp.minimum(x, 0.0) - jnp.log(1.0 + jnp.exp(-jnp.abs(x)))


def _dot(a, b):
    return jnp.dot(a.astype(BF16), b.astype(BF16), preferred_element_type=F32)


def _dot_nt(a, b):
    return lax.dot_general(a.astype(BF16), b.astype(BF16), (((1,), (1,)), ((), ())),
                           preferred_element_type=F32)


def _dot_tn(a, b):
    return lax.dot_general(a.astype(BF16), b.astype(BF16), (((0,), (0,)), ((), ())),
                           preferred_element_type=F32)


def _split_bf16(x, parts):
    out = []
    r = x
    for i in range(parts):
        p = r.astype(BF16)
        out.append(p)
        if i + 1 < parts:
            r = r - p.astype(F32)
    return out


def _dot_exact_lhs(m_bf16, x, parts=3):
    acc = None
    for p in _split_bf16(x, parts):
        t = jnp.dot(m_bf16, p, preferred_element_type=F32)
        acc = t if acc is None else acc + t
    return acc


def _dot_exact_rhs(x, m_bf16, parts=2):
    acc = None
    for p in _split_bf16(x, parts):
        t = jnp.dot(p, m_bf16, preferred_element_type=F32)
        acc = t if acc is None else acc + t
    return acc


def _iota(shape, dim):
    return lax.broadcasted_iota(jnp.int32, shape, dim)


def _tril_ones(c):
    return jnp.where(_iota((c, c), 0) >= _iota((c, c), 1), 1.0, 0.0).astype(BF16)


def _rms_rows(x, g):
    ms = jnp.mean(x * x, axis=-1, keepdims=True)
    return x * lax.rsqrt(ms + EPS) * g


def _compiler_params(sem):
    return pltpu.CompilerParams(dimension_semantics=sem, vmem_limit_bytes=VMEM_LIMIT)


def _norm_kernel(x_ref, g_ref, h_ref):
    h_ref[...] = _rms_rows(x_ref[...], g_ref[...]).astype(BF16)


def _norm(x, g, tm):
    n, d = x.shape
    return pl.pallas_call(
        _norm_kernel,
        grid=(n // tm,),
        in_specs=[pl.BlockSpec((tm, d), lambda i: (i, 0)), pl.BlockSpec((1, d), lambda i: (0, 0))],
        out_specs=pl.BlockSpec((tm, d), lambda i: (i, 0)),
        out_shape=jax.ShapeDtypeStruct((n, d), BF16),
        compiler_params=_compiler_params(("arbitrary",)),
    )(x, g)


def _matmul_kernel(h_ref, w_ref, o_ref):
    o_ref[...] = jnp.dot(h_ref[...], w_ref[...], preferred_element_type=F32)


def _matmul(h, w, tm):
    n, d = h.shape
    cout = w.shape[1]
    return pl.pallas_call(
        _matmul_kernel,
        grid=(n // tm,),
        in_specs=[pl.BlockSpec((tm, d), lambda i: (i, 0)),
                  pl.BlockSpec((d, cout), lambda i: (0, 0), pipeline_mode=pl.Buffered(1))],
        out_specs=pl.BlockSpec((tm, cout), lambda i: (i, 0)),
        out_shape=jax.ShapeDtypeStruct((n, cout), F32),
        compiler_params=_compiler_params(("arbitrary",)),
    )(h, w)


def _out_proj_kernel(oa_ref, ob_ref, oc_ref, wa_ref, wb_ref, wc_ref, x_ref, g_ref, gnext_ref,
                     y_ref, hn_ref):
    m = (jnp.dot(oa_ref[...], wa_ref[...], preferred_element_type=F32)
         + jnp.dot(ob_ref[...], wb_ref[...], preferred_element_type=F32)
         + jnp.dot(oc_ref[...], wc_ref[...], preferred_element_type=F32))
    y = x_ref[...] + _rms_rows(m, g_ref[...])
    y_ref[...] = y
    hn_ref[...] = _rms_rows(y, gnext_ref[...]).astype(BF16)


def _out_proj(oa, ob, oc, wa, wb, wc, x, g, gnext, tm):
    n, d = x.shape
    row = lambda w: pl.BlockSpec((tm, w), lambda i: (i, 0))
    full = lambda a: pl.BlockSpec(a.shape, lambda i: (0, 0))
    return pl.pallas_call(
        _out_proj_kernel,
        grid=(n // tm,),
        in_specs=[row(oa.shape[1]), row(ob.shape[1]), row(oc.shape[1]),
                  full(wa), full(wb), full(wc), row(d), full(g), full(gnext)],
        out_specs=[row(d), row(d)],
        out_shape=[jax.ShapeDtypeStruct((n, d), F32), jax.ShapeDtypeStruct((n, d), BF16)],
        compiler_params=_compiler_params(("arbitrary",)),
    )(oa, ob, oc, wa, wb, wc, x, g, gnext)


def _ffn_kernel(h_ref, x_ref, wg_ref, wu_ref, wd_ref, gpost_ref, gnext_ref, y_ref, hn_ref, acc_scr):
    j = pl.program_id(1)

    @pl.when(j == 0)
    def _():
        acc_scr[...] = jnp.zeros_like(acc_scr)

    h = h_ref[...]
    a = jnp.dot(h, wg_ref[...], preferred_element_type=F32)
    u = jnp.dot(h, wu_ref[...], preferred_element_type=F32)
    acc_scr[...] += jnp.dot((_silu(a) * u).astype(BF16), wd_ref[...], preferred_element_type=F32)

    @pl.when(j == pl.num_programs(1) - 1)
    def _():
        y = x_ref[...] + _rms_rows(acc_scr[...], gpost_ref[...])
        y_ref[...] = y
        hn_ref[...] = _rms_rows(y, gnext_ref[...]).astype(BF16)


def _ffn(h, x, wg, wu, wd, gpost, gnext, tm, tf):
    n, d = x.shape
    dff = wg.shape[1]
    return pl.pallas_call(
        _ffn_kernel,
        grid=(n // tm, dff // tf),
        in_specs=[pl.BlockSpec((tm, d), lambda i, j: (i, 0)),
                  pl.BlockSpec((tm, d), lambda i, j: (i, 0)),
                  pl.BlockSpec((d, tf), lambda i, j: (0, j)),
                  pl.BlockSpec((d, tf), lambda i, j: (0, j)),
                  pl.BlockSpec((tf, d), lambda i, j: (j, 0)),
                  pl.BlockSpec((1, d), lambda i, j: (0, 0)),
                  pl.BlockSpec((1, d), lambda i, j: (0, 0))],
        out_specs=[pl.BlockSpec((tm, d), lambda i, j: (i, 0)),
                   pl.BlockSpec((tm, d), lambda i, j: (i, 0))],
        out_shape=[jax.ShapeDtypeStruct((n, d), F32), jax.ShapeDtypeStruct((n, d), BF16)],
        scratch_shapes=[pltpu.VMEM((tm, d), F32)],
        compiler_params=_compiler_params(("arbitrary", "arbitrary")),
    )(h, x, wg, wu, wd, gpost, gnext)


class _GlrMasks:
    def __init__(self, c):
        self.c = c
        row_a = _iota((c, c), 0)
        col_a = _iota((c, c), 1)
        self.tril = _tril_ones(c)
        self.level = {}
        h = 1
        while 2 * h <= c:
            self.level[h] = (((row_a // (2 * h)) == (col_a // (2 * h)))
                             & ((row_a // h) % 2 == 1) & ((col_a // h) % 2 == 0))
            h *= 2
        self.eye = row_a == col_a


def _small_level_factors(g):
    c = g.shape[0]
    r4 = _iota(g.shape, 0) % 4
    g_prev = pltpu.roll(g, 1, 0)
    g_next = pltpu.roll(g, c - 1, 0)
    e1 = jnp.exp2(jnp.where(r4 % 2 == 1, g, 0.0))
    e2 = jnp.exp2(jnp.where(r4 == 3, g + g_prev, jnp.where(r4 == 2, g, jnp.where(r4 == 0, g_next, 0.0))))
    return e1, e2


def _glr_chunk(q, k, v, b, e1, e2, st, mk):
    c, dk = q.shape
    att = jnp.where(mk.eye, _dot_nt(q, k), 0.0)
    for h, pair_mask in mk.level.items():
        if h == 1:
            e = e1
        elif h == 2:
            e = e2
        else:
            n = c // (2 * h)
            ref = b.reshape(n, 2 * h, dk)[:, h - 1:h, :]
            refb = jnp.broadcast_to(ref, (n, 2 * h, dk)).reshape(c, dk)
            e = jnp.exp2(-jnp.abs(b - refb))
        att = jnp.where(pair_mask, _dot_nt(q * e, k * e), att)
    o = _dot(att, v) + _dot_nt(q * jnp.exp2(b), st)
    b_end = b[c - 1:c, :]
    st_new = st * jnp.exp2(b_end) + _dot_tn(v, k * jnp.exp2(b_end - b))
    return o, st_new


def _gla_kernel(p_ref, s0_ref, gup_ref, gbias_ref, gnorm_ref, o_ref, sout_ref, s_scr, *, c):
    ci = pl.program_id(1)

    @pl.when(ci == 0)
    def _():
        s_scr[...] = s0_ref[0]

    mk = _GlrMasks(c)
    p = p_ref[...]
    q_all = p[:, 0:GLA_QW] * (GLA_DK ** -0.5)
    k_all = p[:, GLA_QW:2 * GLA_QW]
    v_all = p[:, 2 * GLA_QW:2 * GLA_QW + GLA_VW]
    gout = p[:, 2 * GLA_QW + GLA_VW:2 * GLA_QW + 2 * GLA_VW]
    gdown = p[:, 2 * GLA_QW + 2 * GLA_VW:]
    g_all = _log_sigmoid(_dot(gdown, gup_ref[...]) + gbias_ref[...]) * (LOG2_E / GLA_GATE_NORMALIZER)
    b_all = _dot_exact_lhs(mk.tril, g_all)
    e1_all, e2_all = _small_level_factors(g_all)
    gnorm = gnorm_ref[...]
    for h in range(GLA_HEADS):
        ks = slice(h * GLA_DKP, (h + 1) * GLA_DKP)
        vs = slice(h * GLA_DVP, (h + 1) * GLA_DVP)
        o, st = _glr_chunk(q_all[:, ks], k_all[:, ks], v_all[:, vs], b_all[:, ks],
                           e1_all[:, ks], e2_all[:, ks], s_scr[h], mk)
        s_scr[h] = st
        ms = jnp.sum(o * o, axis=-1, keepdims=True) * (1.0 / GLA_DV)
        o = o * lax.rsqrt(ms + EPS) * gnorm * _silu(gout[:, vs])
        o_ref[:, vs] = o.astype(BF16)

    @pl.when(ci == pl.num_programs(1) - 1)
    def _():
        sout_ref[0] = s_scr[...]


def _gla_mixer(p, s0t, gup, gbias, gnorm, b, t, c):
    nc = t // c
    kern = functools.partial(_gla_kernel, c=c)
    full = lambda a: pl.BlockSpec(a.shape, lambda i, j: (0,) * a.ndim)
    st_spec = pl.BlockSpec((1, GLA_HEADS, GLA_DVP, GLA_DKP), lambda i, j: (i, 0, 0, 0))
    return pl.pallas_call(
        kern,
        grid=(b, nc),
        in_specs=[pl.BlockSpec((c, GLA_COLS_P), lambda i, j: (i * nc + j, 0)),
                  st_spec, full(gup), full(gbias), full(gnorm)],
        out_specs=[pl.BlockSpec((c, GLA_VW), lambda i, j: (i * nc + j, 0)), st_spec],
        out_shape=[jax.ShapeDtypeStruct((b * t, GLA_VW), BF16),
                   jax.ShapeDtypeStruct((b, GLA_HEADS, GLA_DVP, GLA_DKP), F32)],
        scratch_shapes=[pltpu.VMEM((GLA_HEADS, GLA_DVP, GLA_DKP), F32)],
        compiler_params=_compiler_params(("arbitrary", "arbitrary")),
    )(p, s0t, gup, gbias, gnorm)


def _hgrn_kernel(p_ref, s0_ref, lbl_ref, hnorm_ref, o_ref, sout_ref, s_scr, *, c, layer):
    ci = pl.program_id(1)

    @pl.when(ci == 0)
    def _():
        s_scr[...] = s0_ref[0]

    lg = lbl_ref[...]
    e = jnp.exp(lg - jnp.max(lg, axis=0, keepdims=True))
    prob = e / jnp.sum(e, axis=0, keepdims=True)
    lb = jnp.zeros((1, HGRN_WIDTH), F32)
    for i in range(1, layer + 1):
        lb = lb + prob[i:i + 1, :]

    mk = _GlrMasks(c)
    p = p_ref[...]
    hq = p[:, 0:HGRN_WIDTH]
    hf = p[:, HGRN_WIDTH:2 * HGRN_WIDTH]
    hi = p[:, 2 * HGRN_WIDTH:3 * HGRN_WIDTH]
    hg = p[:, 3 * HGRN_WIDTH:]
    q_all = _silu(hq)
    f_all = lb + (1.0 - lb) * _sigmoid(hf)
    k_all = 1.0 - f_all
    g_all = jnp.log2(f_all)
    b_all = _dot_exact_lhs(mk.tril, g_all)
    e1_all, e2_all = _small_level_factors(g_all)
    hnorm = hnorm_ref[...]
    for h in range(HGRN_HEADS):
        s = slice(h * HGRN_D, (h + 1) * HGRN_D)
        o, st = _glr_chunk(q_all[:, s], k_all[:, s], hi[:, s], b_all[:, s],
                           e1_all[:, s], e2_all[:, s], s_scr[h], mk)
        s_scr[h] = st
        o = _rms_rows(o, hnorm) * _silu(hg[:, s])
        o_ref[:, s] = o.astype(BF16)

    @pl.when(ci == pl.num_programs(1) - 1)
    def _():
        sout_ref[0] = s_scr[...]


def _hgrn_mixer(p, s0t, lb_logits, hnorm, b, t, c, layer):
    nc = t // c
    kern = functools.partial(_hgrn_kernel, c=c, layer=layer)
    full = lambda a: pl.BlockSpec(a.shape, lambda i, j: (0,) * a.ndim)
    st_spec = pl.BlockSpec((1, HGRN_HEADS, HGRN_D, HGRN_D), lambda i, j: (i, 0, 0, 0))
    return pl.pallas_call(
        kern,
        grid=(b, nc),
        in_specs=[pl.BlockSpec((c, HGRN_COLS), lambda i, j: (i * nc + j, 0)),
                  st_spec, full(lb_logits), full(hnorm)],
        out_specs=[pl.BlockSpec((c, HGRN_WIDTH), lambda i, j: (i * nc + j, 0)), st_spec],
        out_shape=[jax.ShapeDtypeStruct((b * t, HGRN_WIDTH), BF16),
                   jax.ShapeDtypeStruct((b, HGRN_HEADS, HGRN_D, HGRN_D), F32)],
        scratch_shapes=[pltpu.VMEM((HGRN_HEADS, HGRN_D, HGRN_D), F32)],
        compiler_params=_compiler_params(("arbitrary", "arbitrary")),
    )(p, s0t, lb_logits, hnorm)


def _rwkv_kernel(p_ref, shift0_ref, s0_ref, mu_ref, w0_ref, wup_ref, a0_ref, aup_ref, gup_ref,
                 kk_ref, ka_ref, rk_ref, lnw_ref, lnb_ref,
                 o_ref, shift_ref, sout_ref, s_scr, carry_scr, *, c, nsub):
    ci = pl.program_id(1)

    @pl.when(ci == 0)
    def _():
        s_scr[...] = s0_ref[0]
        carry_scr[...] = shift0_ref[0]

    w = RWKV_WIDTH
    rows = c * nsub
    p = p_ref[...]
    row = _iota((rows, RWKV_COLS), 0)
    prev = jnp.where(row == 0, carry_scr[...], pltpu.roll(p, 1, 0))
    last = p[rows - 1:rows, :]
    carry_scr[...] = last
    shift_ref[0] = last
    pm = p + (prev - p) * mu_ref[...]
    r_all = pm[:, 0:w]
    k_all = pm[:, w:2 * w]
    v_all = pm[:, 2 * w:3 * w]
    wa = pm[:, 3 * w:3 * w + LANE]
    gd = pm[:, 3 * w + LANE:]
    lw_all = -jnp.exp(_log_sigmoid(w0_ref[...] + _dot(jnp.tanh(wa), wup_ref[...])) - 0.5)
    a_all = _sigmoid(a0_ref[...] + _dot(wa, aup_ref[...]))
    gate_all = _dot(_sigmoid(gd), gup_ref[...])

    c2 = 2 * c
    lane = _iota((c, LANE), 1)
    lo_half = lane < RWKV_HD
    ri = _iota((c2, c2), 0)
    cj = _iota((c2, c2), 1)
    strict = (ri % c) > (cj % c)
    incl = (ri % c) >= (cj % c)
    eye = jnp.where(ri == cj, 1.0, 0.0)
    inv_levels = []
    blk = 1
    while blk < c:
        inv_levels.append(((ri // (2 * blk)) == (cj // (2 * blk)))
                          & ((ri // blk) % 2 == 1) & ((cj // blk) % 2 == 0))
        blk *= 2
    tr, tc = _iota((rows, rows), 0), _iota((rows, rows), 1)
    tril = jnp.where((tr >= tc) & (tr // c == tc // c), 1.0, 0.0).astype(BF16)
    seg = jnp.where((_iota((LANE, LANE), 0) // RWKV_HD) == (_iota((LANE, LANE), 1) // RWKV_HD),
                    1.0, 0.0).astype(BF16)

    def stack(x):
        return jnp.concatenate([jnp.where(lo_half, x, 0.0), jnp.where(lo_half, 0.0, x)], axis=0)

    def head_sum(x):
        return jnp.concatenate([_dot_exact_rhs(x[:, j * LANE:(j + 1) * LANE], seg)
                                for j in range(RWKV_PAIRS)], axis=1)

    kk_all = k_all * kk_ref[...]
    kk_all = kk_all * lax.rsqrt(jnp.maximum(head_sum(kk_all * kk_all), 1e-24))
    k_all = k_all * (1.0 + (a_all - 1.0) * ka_ref[...])
    beta_all = kk_all * a_all
    bonus_all = head_sum(r_all * k_all * rk_ref[...]) * v_all
    cin_all = _dot_exact_lhs(tril, lw_all)
    ginv = jnp.exp(-cin_all)
    kap_all = kk_all * jnp.exp(cin_all - lw_all)
    rt_all = r_all * jnp.exp(cin_all)
    kt_all = k_all * ginv
    bt_all = beta_all * ginv
    kend_all, bend_all, dec_all = [], [], []
    for i in range(nsub):
        rs = slice(i * c, (i + 1) * c)
        cend = cin_all[(i + 1) * c - 1:(i + 1) * c, :]
        gend = jnp.exp(cend - cin_all[rs])
        kend_all.append(k_all[rs] * gend)
        bend_all.append(beta_all[rs] * gend)
        dec_all.append(jnp.exp(cend))
    pairs = []
    for i, j in [(i, j) for i in range(nsub) for j in range(RWKV_PAIRS)]:
        s = slice(j * LANE, (j + 1) * LANE)
        rs = slice(i * c, (i + 1) * c)
        pairs.append(dict(
            s=s, rs=rs, kap_s=stack(kap_all[rs, s]), r_s=stack(rt_all[rs, s]),
            k_s=stack(kt_all[rs, s]), b_s=stack(bt_all[rs, s]), v_s=stack(v_all[rs, s]),
            kend_s=stack(kend_all[i][:, s]), bend_s=stack(bend_all[i][:, s]), dec=dec_all[i][:, s]))
    def dot_nt2(lhs, r1, r2):
        if r1.shape[0] % LANE:
            return _dot_nt(lhs, r1), _dot_nt(lhs, r2)
        both = _dot_nt(lhs, jnp.concatenate([r1, r2], axis=0))
        return both[:, :r1.shape[0]], both[:, r1.shape[0]:]

    def dot2(lhs, r1, r2):
        both = _dot(lhs, jnp.concatenate([r1, r2], axis=1))
        return both[:, :r1.shape[1]], both[:, r1.shape[1]:]

    for pr in pairs:
        a1, a2 = dot_nt2(pr['kap_s'], pr['k_s'], pr['b_s'])
        pr['a1'], pr['a2'] = jnp.where(strict, a1, 0.0), jnp.where(strict, a2, 0.0)
    for pr in pairs:
        a3, a4 = dot_nt2(pr['r_s'], pr['k_s'], pr['b_s'])
        pr['a3'], pr['a4'] = jnp.where(incl, a3, 0.0), jnp.where(incl, a4, 0.0)
    for pr in pairs:
        pr['a1v'] = _dot(pr['a1'], pr['v_s'])
        pr['a3v'] = _dot(pr['a3'], pr['v_s'])
        pr['kv'] = _dot_tn(pr['v_s'], pr['kend_s'])
    tinv = [eye - jnp.where(inv_levels[0], pr['a2'], 0.0) for pr in pairs]
    for m in inv_levels[1:]:
        half = [_dot(t, jnp.where(m, pr['a2'], 0.0)) for t, pr in zip(tinv, pairs)]
        tinv = [t - _dot(hf, t) for t, hf in zip(tinv, half)]
    for pr, tj in zip(pairs, tinv):
        pr['tk'], pr['tv'] = dot2(tj, pr['kap_s'], pr['a1v'])
    for pr in pairs:
        a4tk, a4tv = dot2(pr['a4'], pr['tk'], pr['tv'])
        pr['reff'] = pr['r_s'] - a4tk
        pr['oc'] = pr['a3v'] - a4tv
        pr['m'] = _dot_tn(pr['tk'], pr['bend_s'])
        pr['q'] = pr['kv'] - _dot_tn(pr['tv'], pr['bend_s'])
    sts = [s_scr[j] for j in range(RWKV_PAIRS)]
    os_ = []
    for i in range(nsub):
        sub = list(zip(pairs[i * RWKV_PAIRS:(i + 1) * RWKV_PAIRS], sts))
        o2s = [_dot_nt(pr['reff'], st) + pr['oc'] for pr, st in sub]
        sts = [st * pr['dec'] - _dot(st, pr['m']) + pr['q'] for pr, st in sub]
        os_ += [o2[0:c] + o2[c:c2] for o2 in o2s]
    for j, st in enumerate(sts):
        s_scr[j] = st
    ds = [o - _dot_exact_rhs(o, seg) * (1.0 / RWKV_HD) for o in os_]
    vars_ = [_dot_exact_rhs(d * d, seg) * (1.0 / RWKV_HD) for d in ds]
    for pr, d, var in zip(pairs, ds, vars_):
        s, rs = pr['s'], pr['rs']
        on = d * lax.rsqrt(var + RWKV_GN_EPS) * lnw_ref[:, s] + lnb_ref[:, s]
        o_ref[rs, s] = ((on + bonus_all[rs, s]) * gate_all[rs, s]).astype(BF16)

    @pl.when(ci == pl.num_programs(1) - 1)
    def _():
        sout_ref[0] = s_scr[...]


def _rwkv_mixer(p, shift0, s0, wts, b, t, c, nsub):
    rows = c * nsub
    nc = t // rows
    kern = functools.partial(_rwkv_kernel, c=c, nsub=nsub)
    full = lambda a: pl.BlockSpec(a.shape, lambda i, j: (0,) * a.ndim)
    st_spec = pl.BlockSpec((1, RWKV_PAIRS, LANE, LANE), lambda i, j: (i, 0, 0, 0))
    sh_spec = pl.BlockSpec((1, 1, RWKV_COLS), lambda i, j: (i, 0, 0))
    return pl.pallas_call(
        kern,
        grid=(b, nc),
        in_specs=[pl.BlockSpec((rows, RWKV_COLS), lambda i, j: (i * nc + j, 0)), sh_spec, st_spec]
                 + [full(a) for a in wts],
        out_specs=[pl.BlockSpec((rows, RWKV_WIDTH), lambda i, j: (i * nc + j, 0)), sh_spec, st_spec],
        out_shape=[jax.ShapeDtypeStruct((b * t, RWKV_WIDTH), BF16),
                   jax.ShapeDtypeStruct((b, 1, RWKV_COLS), F32),
                   jax.ShapeDtypeStruct((b, RWKV_PAIRS, LANE, LANE), F32)],
        scratch_shapes=[pltpu.VMEM((RWKV_PAIRS, LANE, LANE), F32), pltpu.VMEM((1, RWKV_COLS), F32)],
        compiler_params=_compiler_params(("arbitrary", "arbitrary")),
    )(p, shift0, s0, *wts)


def _rwkv_cols_in(a):
    return jnp.concatenate([a[..., 0:640], a[..., 704:1984], a[..., 640:704], a[..., 1984:2176]], axis=-1)


def _rwkv_cols_out(a):
    return jnp.concatenate([a[..., 0:640], a[..., 1920:1984], a[..., 640:1920], a[..., 1984:2176]], axis=-1)


def _pad_heads(w, heads, d, dp):
    lead = w.shape[:-1]
    w = w.reshape(lead + (heads, d))
    w = jnp.pad(w, [(0, 0)] * len(lead) + [(0, 0), (0, dp - d)])
    return w.reshape(lead + (heads * dp,))


def _prep_weights(w_in, gla_gate_up, gla_gate_bias, gla_norm, rwkv_mu, rwkv_w_up, rwkv_a_up,
                  rwkv_r_k, w_out):
    gq, gk, gv, gdn, gout, rw, hg = jnp.split(
        w_in, [384, 768, 1536, 1552, 2320, 2320 + RWKV_COLS], axis=-1)
    w_gla = jnp.concatenate([
        _pad_heads(gq, GLA_HEADS, GLA_DK, GLA_DKP), _pad_heads(gk, GLA_HEADS, GLA_DK, GLA_DKP),
        _pad_heads(gv, GLA_HEADS, GLA_DV, GLA_DVP), _pad_heads(gout, GLA_HEADS, GLA_DV, GLA_DVP),
        jnp.pad(gdn, ((0, 0), (0, 0), (0, GLA_RANKP - GLA_RANK)))], axis=-1).astype(BF16)
    w_rwkv = _rwkv_cols_in(rw).astype(BF16)
    w_hgrn = hg.astype(BF16)
    gup = jnp.pad(_pad_heads(gla_gate_up, GLA_HEADS, GLA_DK, GLA_DKP),
                  ((0, 0), (0, GLA_RANKP - GLA_RANK), (0, 0))).astype(BF16)
    gbias = _pad_heads(gla_gate_bias, GLA_HEADS, GLA_DK, GLA_DKP)[:, None, :]
    gnorm = jnp.pad(gla_norm, ((0, 0), (0, GLA_DVP - GLA_DV)))[:, None, :]
    mu = _rwkv_cols_in(rwkv_mu)[:, None, :]
    wup = jnp.pad(rwkv_w_up, ((0, 0), (0, 64), (0, 0))).astype(BF16)
    aup = jnp.pad(rwkv_a_up, ((0, 0), (64, 0), (0, 0))).astype(BF16)
    rk = rwkv_r_k.reshape(DEPTH, 1, RWKV_WIDTH)
    wo_a = w_out[:, :768].reshape(DEPTH, GLA_HEADS, GLA_DV, D_MODEL)
    wo_a = jnp.pad(wo_a, ((0, 0), (0, 0), (0, GLA_DVP - GLA_DV), (0, 0)))
    wo_a = wo_a.reshape(DEPTH, GLA_VW, D_MODEL).astype(BF16)
    wo_b = w_out[:, 768:1408].astype(BF16)
    wo_c = w_out[:, 1408:].astype(BF16)
    return w_gla, w_rwkv, w_hgrn, gup, gbias, gnorm, mu, wup, aup, rk, wo_a, wo_b, wo_c


def _gla_state_in(s):
    s = jnp.swapaxes(s, -1, -2)
    return jnp.pad(s, [(0, 0)] * 3 + [(0, GLA_DVP - GLA_DV), (0, GLA_DKP - GLA_DK)])


def _gla_state_out(s):
    return jnp.swapaxes(s[..., :GLA_DV, :GLA_DK], -1, -2)


def _rwkv_state_in(s):
    d, b = s.shape[:2]
    s = s.reshape(d, b, RWKV_PAIRS, 2, RWKV_HD, RWKV_HD)
    out = jnp.einsum('dbjhvk,hg->dbjhvgk', s, jnp.eye(2, dtype=s.dtype))
    return out.reshape(d, b, RWKV_PAIRS, LANE, LANE)


def _rwkv_state_out(s):
    d, b = s.shape[:2]
    s = s.reshape(d, b, RWKV_PAIRS, 2, RWKV_HD, 2, RWKV_HD)
    out = jnp.stack([s[:, :, :, 0, :, 0, :], s[:, :, :, 1, :, 1, :]], axis=3)
    return out.reshape(d, b, RWKV_HEADS, RWKV_HD, RWKV_HD)


def _row_tile(n, cap):
    t = cap
    while n % t:
        t //= 2
    return t


def _trunk(x, s_gla, s_rwkv, s_shift, s_hgrn, wt, b, t):
    n = b * t
    tm = _row_tile(n, 512)
    c_glr = min(128, t)
    c_rwkv = min(64, t)
    n_rwkv = 2 if t % (2 * c_rwkv) == 0 else 1
    new_gla, new_rwkv, new_shift, new_hgrn = [], [], [], []
    h = _norm(x, wt['norm_mix_pre'][0], tm)
    for l in range(DEPTH):
        pg = _matmul(h, wt['w_gla'][l], tm)
        pr = _matmul(h, wt['w_rwkv'][l], tm)
        ph = _matmul(h, wt['w_hgrn'][l], tm)
        oa, g1 = _gla_mixer(pg, s_gla[l], wt['gup'][l], wt['gbias'][l], wt['gnorm'][l], b, t, c_glr)
        rw = [wt[k][l] for k in ('mu', 'w0', 'wup', 'a0', 'aup', 'g_up', 'k_k', 'k_a', 'rk',
                                 'ln_w', 'ln_b')]
        ob, sh1, r1 = _rwkv_mixer(pr, s_shift[l], s_rwkv[l], rw, b, t, c_rwkv, n_rwkv)
        oc, h1 = _hgrn_mixer(ph, s_hgrn[l], wt['lb_logits'], wt['hnorm'][l], b, t, c_glr, l)
        x, h = _out_proj(oa, ob, oc, wt['wo_a'][l], wt['wo_b'][l], wt['wo_c'][l], x,
                         wt['norm_mix_post'][l], wt['norm_ffn_pre'][l], tm)
        x, h = _ffn(h, x, wt['ffn_w_gate'][l], wt['ffn_w_up'][l], wt['ffn_w_down'][l],
                    wt['norm_ffn_post'][l], wt['norm_mix_pre'][(l + 1) % DEPTH], tm, 512)
        new_gla.append(g1)
        new_rwkv.append(r1)
        new_shift.append(sh1)
        new_hgrn.append(h1)
    return x, jnp.stack(new_gla), jnp.stack(new_rwkv), jnp.stack(new_shift), jnp.stack(new_hgrn)


def _run_path(x, state_gla, state_rwkv, state_shift, state_hgrn, wt):
    b, t, d = x.shape
    y, g, r, sh, h = _trunk(
        x.reshape(b * t, d), _gla_state_in(state_gla), _rwkv_state_in(state_rwkv),
        _rwkv_cols_in(state_shift)[:, :, None, :], jnp.swapaxes(state_hgrn, -1, -2), wt, b, t)
    return (y.reshape(b, t, d), _gla_state_out(g), _rwkv_state_out(r),
            _rwkv_cols_out(sh[:, :, 0, :]), jnp.swapaxes(h, -1, -2))


def kernel(x_prompt, x_sample, state_gla, state_rwkv, state_rwkv_shift, state_hgrn, norm_mix_pre, norm_mix_post, norm_ffn_pre, norm_ffn_post, w_in, gla_gate_up, gla_gate_bias, gla_norm, rwkv_mu, rwkv_w0, rwkv_w_up, rwkv_a0, rwkv_a_up, rwkv_g_up, rwkv_k_k, rwkv_k_a, rwkv_r_k, rwkv_ln_w, rwkv_ln_b, hgrn_lb_logits, hgrn_norm, w_out, ffn_w_gate, ffn_w_up, ffn_w_down):
    (w_gla, w_rwkv, w_hgrn, gup, gbias, gnorm, mu, wup, aup, rk, wo_a, wo_b, wo_c) = _prep_weights(
        w_in, gla_gate_up, gla_gate_bias, gla_norm, rwkv_mu, rwkv_w_up, rwkv_a_up, rwkv_r_k, w_out)
    vec = lambda a: a[:, None, :]
    wt = dict(
        norm_mix_pre=vec(norm_mix_pre), norm_mix_post=vec(norm_mix_post),
        norm_ffn_pre=vec(norm_ffn_pre), norm_ffn_post=vec(norm_ffn_post),
        w_gla=w_gla, w_rwkv=w_rwkv, w_hgrn=w_hgrn, gup=gup, gbias=gbias, gnorm=gnorm,
        mu=mu, w0=vec(rwkv_w0), wup=wup, a0=vec(rwkv_a0), aup=aup, g_up=rwkv_g_up.astype(BF16),
        k_k=vec(rwkv_k_k), k_a=vec(rwkv_k_a), rk=rk, ln_w=vec(rwkv_ln_w), ln_b=vec(rwkv_ln_b),
        lb_logits=hgrn_lb_logits, hnorm=vec(hgrn_norm),
        wo_a=wo_a, wo_b=wo_b, wo_c=wo_c,
        ffn_w_gate=ffn_w_gate.astype(BF16), ffn_w_up=ffn_w_up.astype(BF16),
        ffn_w_down=ffn_w_down.astype(BF16))

    bp = x_prompt.shape[0]
    zeros = lambda s: jnp.zeros((DEPTH, bp) + s.shape[2:], x_prompt.dtype)
    y_p, gla_p, rwkv_p, shift_p, hgrn_p = _run_path(
        x_prompt, zeros(state_gla), zeros(state_rwkv), zeros(state_rwkv_shift), zeros(state_hgrn), wt)
    y_s, gla_s, rwkv_s, shift_s, hgrn_s = _run_path(
        x_sample, state_gla, state_rwkv, state_rwkv_shift, state_hgrn, wt)
    return (y_p, y_s, gla_p, rwkv_p, shift_p, hgrn_p, gla_s, rwkv_s, shift_s, hgrn_s)
```

```python
import functools

import jax
import jax.numpy as jnp
from jax import lax
from jax.experimental import pallas as pl
from jax.experimental.pallas import tpu as pltpu

F32 = jnp.float32
BF16 = jnp.bfloat16

D_MODEL = 2048
DEPTH = 4
EPS = 1e-6

GLA_HEADS, GLA_DK, GLA_DV = 4, 96, 192
GLA_DKP, GLA_DVP = 128, 256
GLA_RANK, GLA_RANKP = 16, 128
GLA_GATE_NORMALIZER = 16.0
GLA_QW = GLA_HEADS * GLA_DKP
GLA_VW = GLA_HEADS * GLA_DVP
GLA_COLS_P = 2 * GLA_QW + 2 * GLA_VW + GLA_RANKP

RWKV_HEADS, RWKV_HD = 10, 64
RWKV_WIDTH = RWKV_HEADS * RWKV_HD
RWKV_PAIRS = RWKV_HEADS // 2
RWKV_COLS = 2176
RWKV_GN_EPS = 64e-5

HGRN_HEADS, HGRN_D = 5, 128
HGRN_WIDTH = HGRN_HEADS * HGRN_D
HGRN_COLS = 4 * HGRN_WIDTH

D_FF = 5632
LANE = 128
LOG2_E = 1.4426950408889634

VMEM_LIMIT = 56 * 1024 * 1024


def _sigmoid(x):
    return 1.0 / (1.0 + jnp.exp(-x))


def _silu(x):
    return x * _sigmoid(x)


def _log_sigmoid(x):
    return jnp.minimum(x, 0.0) - jnp.log(1.0 + jnp.exp(-jnp.abs(x)))


def _dot(a, b):
    return jnp.dot(a.astype(BF16), b.astype(BF16), preferred_element_type=F32)


def _dot_nt(a, b):
    return lax.dot_general(a.astype(BF16), b.astype(BF16), (((1,), (1,)), ((), ())),
                           preferred_element_type=F32)


def _dot_tn(a, b):
    return lax.dot_general(a.astype(BF16), b.astype(BF16), (((0,), (0,)), ((), ())),
                           preferred_element_type=F32)


def _split_bf16(x, parts):
    out = []
    r = x
    for i in range(parts):
        p = r.astype(BF16)
        out.append(p)
        if i + 1 < parts:
            r = r - p.astype(F32)
    return out


def _dot_exact_lhs(m_bf16, x, parts=3):
    acc = None
    for p in _split_bf16(x, parts):
        t = jnp.dot(m_bf16, p, preferred_element_type=F32)
        acc = t if acc is None else acc + t
    return acc


def _iota(shape, dim):
    return lax.broadcasted_iota(jnp.int32, shape, dim)


def _tril_ones(c):
    return jnp.where(_iota((c, c), 0) >= _iota((c, c), 1), 1.0, 0.0).astype(BF16)


def _rms_rows(x, g):
    ms = jnp.mean(x * x, axis=-1, keepdims=True)
    return x * lax.rsqrt(ms + EPS) * g


def _compiler_params(sem):
    return pltpu.CompilerParams(dimension_semantics=sem, vmem_limit_bytes=VMEM_LIMIT)


def _norm_kernel(x_ref, g_ref, h_ref):
    h_ref[...] = _rms_rows(x_ref[...], g_ref[...]).astype(BF16)


def _norm(x, g, tm):
    n, d = x.shape
    return pl.pallas_call(
        _norm_kernel,
        grid=(n // tm,),
        in_specs=[pl.BlockSpec((tm, d), lambda i: (i, 0)), pl.BlockSpec((1, d), lambda i: (0, 0))],
        out_specs=pl.BlockSpec((tm, d), lambda i: (i, 0)),
        out_shape=jax.ShapeDtypeStruct((n, d), BF16),
        compiler_params=_compiler_params(("arbitrary",)),
    )(x, g)


def _matmul_kernel(h_ref, w_ref, o_ref):
    o_ref[...] = jnp.dot(h_ref[...], w_ref[...], preferred_element_type=F32)


def _matmul(h, w, tm):
    n, d = h.shape
    cout = w.shape[1]
    return pl.pallas_call(
        _matmul_kernel,
        grid=(n // tm,),
        in_specs=[pl.BlockSpec((tm, d), lambda i: (i, 0)),
                  pl.BlockSpec((d, cout), lambda i: (0, 0), pipeline_mode=pl.Buffered(1))],
        out_specs=pl.BlockSpec((tm, cout), lambda i: (i, 0)),
        out_shape=jax.ShapeDtypeStruct((n, cout), F32),
        compiler_params=_compiler_params(("arbitrary",)),
    )(h, w)


def _out_proj_kernel(oa_ref, ob_ref, oc_ref, wa_ref, wb_ref, wc_ref, x_ref, g_ref, gnext_ref,
                     y_ref, hn_ref):
    m = (jnp.dot(oa_ref[...], wa_ref[...], preferred_element_type=F32)
         + jnp.dot(ob_ref[...], wb_ref[...], preferred_element_type=F32)
         + jnp.dot(oc_ref[...], wc_ref[...], preferred_element_type=F32))
    y = x_ref[...] + _rms_rows(m, g_ref[...])
    y_ref[...] = y
    hn_ref[...] = _rms_rows(y, gnext_ref[...]).astype(BF16)


def _out_proj(oa, ob, oc, wa, wb, wc, x, g, gnext, tm):
    n, d = x.shape
    row = lambda w: pl.BlockSpec((tm, w), lambda i: (i, 0))
    full = lambda a: pl.BlockSpec(a.shape, lambda i: (0, 0))
    return pl.pallas_call(
        _out_proj_kernel,
        grid=(n // tm,),
        in_specs=[row(oa.shape[1]), row(ob.shape[1]), row(oc.shape[1]),
                  full(wa), full(wb), full(wc), row(d), full(g), full(gnext)],
        out_specs=[row(d), row(d)],
        out_shape=[jax.ShapeDtypeStruct((n, d), F32), jax.ShapeDtypeStruct((n, d), BF16)],
        compiler_params=_compiler_params(("arbitrary",)),
    )(oa, ob, oc, wa, wb, wc, x, g, gnext)


def _ffn_kernel(h_ref, x_ref, wg_ref, wu_ref, wd_ref, gpost_ref, gnext_ref, y_ref, hn_ref, acc_scr):
    j = pl.program_id(1)

    @pl.when(j == 0)
    def _():
        acc_scr[...] = jnp.zeros_like(acc_scr)

    h = h_ref[...]
    a = jnp.dot(h, wg_ref[...], preferred_element_type=F32)
    u = jnp.dot(h, wu_ref[...], preferred_element_type=F32)
    acc_scr[...] += jnp.dot((_silu(a) * u).astype(BF16), wd_ref[...], preferred_element_type=F32)

    @pl.when(j == pl.num_programs(1) - 1)
    def _():
        y = x_ref[...] + _rms_rows(acc_scr[...], gpost_ref[...])
        y_ref[...] = y
        hn_ref[...] = _rms_rows(y, gnext_ref[...]).astype(BF16)


def _ffn(h, x, wg, wu, wd, gpost, gnext, tm, tf):
    n, d = x.shape
    dff = wg.shape[1]
    return pl.pallas_call(
        _ffn_kernel,
        grid=(n // tm, dff // tf),
        in_specs=[pl.BlockSpec((tm, d), lambda i, j: (i, 0)),
                  pl.BlockSpec((tm, d), lambda i, j: (i, 0)),
                  pl.BlockSpec((d, tf), lambda i, j: (0, j)),
                  pl.BlockSpec((d, tf), lambda i, j: (0, j)),
                  pl.BlockSpec((tf, d), lambda i, j: (j, 0)),
                  pl.BlockSpec((1, d), lambda i, j: (0, 0)),
                  pl.BlockSpec((1, d), lambda i, j: (0, 0))],
        out_specs=[pl.BlockSpec((tm, d), lambda i, j: (i, 0)),
                   pl.BlockSpec((tm, d), lambda i, j: (i, 0))],
        out_shape=[jax.ShapeDtypeStruct((n, d), F32), jax.ShapeDtypeStruct((n, d), BF16)],
        scratch_shapes=[pltpu.VMEM((tm, d), F32)],
        compiler_params=_compiler_params(("arbitrary", "arbitrary")),
    )(h, x, wg, wu, wd, gpost, gnext)


class _GlrMasks:
    def __init__(self, c):
        self.c = c
        row_a = _iota((c, c), 0)
        col_a = _iota((c, c), 1)
        self.tril = _tril_ones(c)
        self.level = {}
        h = 1
        while 2 * h <= c:
            self.level[h] = (((row_a // (2 * h)) == (col_a // (2 * h)))
                             & ((row_a // h) % 2 == 1) & ((col_a // h) % 2 == 0))
            h *= 2
        self.eye = row_a == col_a


def _small_level_factors(g):
    c = g.shape[0]
    r4 = _iota(g.shape, 0) % 4
    g_prev = pltpu.roll(g, 1, 0)
    g_next = pltpu.roll(g, c - 1, 0)
    e1 = jnp.exp2(jnp.where(r4 % 2 == 1, g, 0.0))
    e2 = jnp.exp2(jnp.where(r4 == 3, g + g_prev, jnp.where(r4 == 2, g, jnp.where(r4 == 0, g_next, 0.0))))
    return e1, e2


def _glr_chunk(q, k, v, b, e1, e2, st, mk):
    c, dk = q.shape
    att = jnp.where(mk.eye, _dot_nt(q, k), 0.0)
    for h, pair_mask in mk.level.items():
        if h == 1:
            e = e1
        elif h == 2:
            e = e2
        else:
            n = c // (2 * h)
            ref = b.reshape(n, 2 * h, dk)[:, h - 1:h, :]
            refb = jnp.broadcast_to(ref, (n, 2 * h, dk)).reshape(c, dk)
            e = jnp.exp2(-jnp.abs(b - refb))
        att = jnp.where(pair_mask, _dot_nt(q * e, k * e), att)
    o = _dot(att, v) + _dot_nt(q * jnp.exp2(b), st)
    b_end = b[c - 1:c, :]
    st_new = st * jnp.exp2(b_end) + _dot_tn(v, k * jnp.exp2(b_end - b))
    return o, st_new


def _gla_kernel(p_ref, s0_ref, gup_ref, gbias_ref, gnorm_ref, o_ref, sout_ref, s_scr, *, c):
    ci = pl.program_id(1)

    @pl.when(ci == 0)
    def _():
        s_scr[...] = s0_ref[0]

    mk = _GlrMasks(c)
    p = p_ref[...]
    q_all = p[:, 0:GLA_QW] * (GLA_DK ** -0.5)
    k_all = p[:, GLA_QW:2 * GLA_QW]
    v_all = p[:, 2 * GLA_QW:2 * GLA_QW + GLA_VW]
    gout = p[:, 2 * GLA_QW + GLA_VW:2 * GLA_QW + 2 * GLA_VW]
    gdown = p[:, 2 * GLA_QW + 2 * GLA_VW:]
    g_all = _log_sigmoid(_dot(gdown, gup_ref[...]) + gbias_ref[...]) * (LOG2_E / GLA_GATE_NORMALIZER)
    b_all = _dot_exact_lhs(mk.tril, g_all)
    e1_all, e2_all = _small_level_factors(g_all)
    gnorm = gnorm_ref[...]
    for h in range(GLA_HEADS):
        ks = slice(h * GLA_DKP, (h + 1) * GLA_DKP)
        vs = slice(h * GLA_DVP, (h + 1) * GLA_DVP)
        o, st = _glr_chunk(q_all[:, ks], k_all[:, ks], v_all[:, vs], b_all[:, ks],
                           e1_all[:, ks], e2_all[:, ks], s_scr[h], mk)
        s_scr[h] = st
        ms = jnp.sum(o * o, axis=-1, keepdims=True) * (1.0 / GLA_DV)
        o = o * lax.rsqrt(ms + EPS) * gnorm * _silu(gout[:, vs])
        o_ref[:, vs] = o.astype(BF16)

    @pl.when(ci == pl.num_programs(1) - 1)
    def _():
        sout_ref[0] = s_scr[...]


def _gla_mixer(p, s0t, gup, gbias, gnorm, b, t, c):
    nc = t // c
    kern = functools.partial(_gla_kernel, c=c)
    full = lambda a: pl.BlockSpec(a.shape, lambda i, j: (0,) * a.ndim)
    st_spec = pl.BlockSpec((1, GLA_HEADS, GLA_DVP, GLA_DKP), lambda i, j: (i, 0, 0, 0))
    return pl.pallas_call(
        kern,
        grid=(b, nc),
        in_specs=[pl.BlockSpec((c, GLA_COLS_P), lambda i, j: (i * nc + j, 0)),
                  st_spec, full(gup), full(gbias), full(gnorm)],
        out_specs=[pl.BlockSpec((c, GLA_VW), lambda i, j: (i * nc + j, 0)), st_spec],
        out_shape=[jax.ShapeDtypeStruct((b * t, GLA_VW), BF16),
                   jax.ShapeDtypeStruct((b, GLA_HEADS, GLA_DVP, GLA_DKP), F32)],
        scratch_shapes=[pltpu.VMEM((GLA_HEADS, GLA_DVP, GLA_DKP), F32)],
        compiler_params=_compiler_params(("arbitrary", "arbitrary")),
    )(p, s0t, gup, gbias, gnorm)


def _hgrn_kernel(p_ref, s0_ref, lbl_ref, hnorm_ref, o_ref, sout_ref, s_scr, *, c, layer):
    ci = pl.program_id(1)

    @pl.when(ci == 0)
    def _():
        s_scr[...] = s0_ref[0]

    lg = lbl_ref[...]
    e = jnp.exp(lg - jnp.max(lg, axis=0, keepdims=True))
    prob = e / jnp.sum(e, axis=0, keepdims=True)
    lb = jnp.zeros((1, HGRN_WIDTH), F32)
    for i in range(1, layer + 1):
        lb = lb + prob[i:i + 1, :]

    mk = _GlrMasks(c)
    p = p_ref[...]
    hq = p[:, 0:HGRN_WIDTH]
    hf = p[:, HGRN_WIDTH:2 * HGRN_WIDTH]
    hi = p[:, 2 * HGRN_WIDTH:3 * HGRN_WIDTH]
    hg = p[:, 3 * HGRN_WIDTH:]
    q_all = _silu(hq)
    f_all = lb + (1.0 - lb) * _sigmoid(hf)
    k_all = 1.0 - f_all
    g_all = jnp.log2(f_all)
    b_all = _dot_exact_lhs(mk.tril, g_all)
    e1_all, e2_all = _small_level_factors(g_all)
    hnorm = hnorm_ref[...]
    for h in range(HGRN_HEADS):
        s = slice(h * HGRN_D, (h + 1) * HGRN_D)
        o, st = _glr_chunk(q_all[:, s], k_all[:, s], hi[:, s], b_all[:, s],
                           e1_all[:, s], e2_all[:, s], s_scr[h], mk)
        s_scr[h] = st
        o = _rms_rows(o, hnorm) * _silu(hg[:, s])
        o_ref[:, s] = o.astype(BF16)

    @pl.when(ci == pl.num_programs(1) - 1)
    def _():
        sout_ref[0] = s_scr[...]


def _hgrn_mixer(p, s0t, lb_logits, hnorm, b, t, c, layer):
    nc = t // c
    kern = functools.partial(_hgrn_kernel, c=c, layer=layer)
    full = lambda a: pl.BlockSpec(a.shape, lambda i, j: (0,) * a.ndim)
    st_spec = pl.BlockSpec((1, HGRN_HEADS, HGRN_D, HGRN_D), lambda i, j: (i, 0, 0, 0))
    return pl.pallas_call(
        kern,
        grid=(b, nc),
        in_specs=[pl.BlockSpec((c, HGRN_COLS), lambda i, j: (i * nc + j, 0)),
                  st_spec, full(lb_logits), full(hnorm)],
        out_specs=[pl.BlockSpec((c, HGRN_WIDTH), lambda i, j: (i * nc + j, 0)), st_spec],
        out_shape=[jax.ShapeDtypeStruct((b * t, HGRN_WIDTH), BF16),
                   jax.ShapeDtypeStruct((b, HGRN_HEADS, HGRN_D, HGRN_D), F32)],
        scratch_shapes=[pltpu.VMEM((HGRN_HEADS, HGRN_D, HGRN_D), F32)],
        compiler_params=_compiler_params(("arbitrary", "arbitrary")),
    )(p, s0t, lb_logits, hnorm)


def _rwkv_kernel(p_ref, shift0_ref, s0_ref, mu_ref, w0_ref, wup_ref, a0_ref, aup_ref, gup_ref,
                 kk_ref, ka_ref, rk_ref, lnw_ref, lnb_ref,
                 o_ref, shift_ref, sout_ref, s_scr, carry_scr, *, c, nsub):
    ci = pl.program_id(1)

    @pl.when(ci == 0)
    def _():
        s_scr[...] = s0_ref[0]
        carry_scr[...] = shift0_ref[0]

    w = RWKV_WIDTH
    rows = c * nsub
    grp = min(nsub, 2)
    grows = c * grp
    first_prev = carry_scr[...]
    last = p_ref[rows - 1:rows, :]
    carry_scr[...] = last
    shift_ref[0] = last

    c2 = 2 * c
    lane = _iota((c, LANE), 1)
    lo_half = lane < RWKV_HD
    ri = _iota((c2, c2), 0)
    cj = _iota((c2, c2), 1)
    strict = (ri % c) > (cj % c)
    incl = (ri % c) >= (cj % c)
    eye = jnp.where(ri == cj, 1.0, 0.0)
    inv_levels = []
    blk = 1
    while blk < c:
        inv_levels.append(((ri // (2 * blk)) == (cj // (2 * blk)))
                          & ((ri // blk) % 2 == 1) & ((cj // blk) % 2 == 0))
        blk *= 2
    tr, tc = _iota((grows, grows), 0), _iota((grows, grows), 1)
    tril = jnp.where((tr >= tc) & (tr // c == tc // c), 1.0, 0.0).astype(BF16)
    row = _iota((grows, RWKV_COLS), 0)
    seg = jnp.where((_iota((LANE, LANE), 0) // RWKV_HD) == (_iota((LANE, LANE), 1) // RWKV_HD),
                    1.0, 0.0).astype(BF16)

    def stack(x):
        return jnp.concatenate([jnp.where(lo_half, x, 0.0), jnp.where(lo_half, 0.0, x)], axis=0)

    def head_sum(x):
        return jnp.concatenate([_dot(x[:, j * LANE:(j + 1) * LANE], seg)
                                for j in range(RWKV_PAIRS)], axis=1)

    def prepare(g0):
        p = p_ref[g0:g0 + grows, :]
        before = first_prev if g0 == 0 else p_ref[g0 - 1:g0, :]
        prev = jnp.where(row == 0, before, pltpu.roll(p, 1, 0))
        pm = p + (prev - p) * mu_ref[...]
        r_all = pm[:, 0:w]
        k_all = pm[:, w:2 * w]
        v_all = pm[:, 2 * w:3 * w]
        wa = pm[:, 3 * w:3 * w + LANE]
        gd = pm[:, 3 * w + LANE:]
        lw_all = -jnp.exp(_log_sigmoid(w0_ref[...] + _dot(jnp.tanh(wa), wup_ref[...])) - 0.5)
        a_all = _sigmoid(a0_ref[...] + _dot(wa, aup_ref[...]))
        gate_all = _dot(_sigmoid(gd), gup_ref[...])
        kk_all = k_all * kk_ref[...]
        kk_all = kk_all * lax.rsqrt(jnp.maximum(head_sum(kk_all * kk_all), 1e-24))
        k_all = k_all * (1.0 + (a_all - 1.0) * ka_ref[...])
        beta_all = kk_all * a_all
        bonus_all = head_sum(r_all * k_all * rk_ref[...]) * v_all
        cin_all = _dot_exact_lhs(tril, lw_all, parts=2)
        ginv = jnp.exp(-cin_all)
        kap_all = kk_all * jnp.exp(cin_all - lw_all)
        rt_all = r_all * jnp.exp(cin_all)
        kt_all = k_all * ginv
        bt_all = beta_all * ginv
        out = []
        for i in range(grp):
            rs = slice(i * c, (i + 1) * c)
            cend = cin_all[(i + 1) * c - 1:(i + 1) * c, :]
            gend = jnp.exp(cend - cin_all[rs])
            kend, bend, dec = k_all[rs] * gend, beta_all[rs] * gend, jnp.exp(cend)
            for j in range(RWKV_PAIRS):
                s = slice(j * LANE, (j + 1) * LANE)
                out.append(dict(
                    s=s, rs=slice(g0 + i * c, g0 + (i + 1) * c),
                    kap_s=stack(kap_all[rs, s]), r_s=stack(rt_all[rs, s]), k_s=stack(kt_all[rs, s]),
                    b_s=stack(bt_all[rs, s]), v_s=stack(v_all[rs, s]), kend_s=stack(kend[:, s]),
                    bend_s=stack(bend[:, s]), dec=dec[:, s], bonus=bonus_all[rs, s],
                    gate=gate_all[rs, s]))
        return out

    def dot_nt2(lhs, r1, r2):
        if r1.shape[0] % LANE:
            return _dot_nt(lhs, r1), _dot_nt(lhs, r2)
        both = _dot_nt(lhs, jnp.concatenate([r1, r2], axis=0))
        return both[:, :r1.shape[0]], both[:, r1.shape[0]:]

    def dot2(lhs, r1, r2):
        both = _dot(lhs, jnp.concatenate([r1, r2], axis=1))
        return both[:, :r1.shape[1]], both[:, r1.shape[1]:]

    pairs = []
    for g0 in range(0, rows, grows):
        group = prepare(g0)
        for pr in group:
            a1, a2 = dot_nt2(pr['kap_s'], pr['k_s'], pr['b_s'])
            pr['a1'], pr['a2'] = jnp.where(strict, a1, 0.0), jnp.where(strict, a2, 0.0)
        for pr in group:
            a3, a4 = dot_nt2(pr['r_s'], pr['k_s'], pr['b_s'])
            pr['a3'], pr['a4'] = jnp.where(incl, a3, 0.0), jnp.where(incl, a4, 0.0)
        for pr in group:
            pr['a1v'] = _dot(pr['a1'], pr['v_s'])
            pr['a3v'] = _dot(pr['a3'], pr['v_s'])
            pr['kv'] = _dot_tn(pr['v_s'], pr['kend_s'])
        pairs += group
    if c2 % LANE == 0 and len(pairs) % 2 == 0:
        left = _iota((c2, 2 * c2), 1) < c2
        tile2 = lambda x: jnp.concatenate([x, x], axis=1)

        def block_diag(x, keep=None):
            top = left if keep is None else keep & left
            bot = ~left if keep is None else keep & ~left
            x = x.astype(BF16)
            return jnp.concatenate([jnp.where(top, x, 0.0), jnp.where(bot, x, 0.0)], axis=0)

        a2w = [jnp.concatenate([pa['a2'], pb['a2']], axis=1) for pa, pb in zip(pairs[0::2], pairs[1::2])]
        tw = [tile2(eye) - jnp.where(tile2(inv_levels[0]), a, 0.0) for a in a2w]
        for m in inv_levels[1:]:
            mw = tile2(m)
            half = [_dot(t, block_diag(a, mw)) for t, a in zip(tw, a2w)]
            tw = [t - _dot(hf, block_diag(t)) for t, hf in zip(tw, half)]
        tinv = [x for t in tw for x in (t[:, :c2], t[:, c2:])]
    else:
        tinv = [eye - jnp.where(inv_levels[0], pr['a2'], 0.0) for pr in pairs]
        for m in inv_levels[1:]:
            half = [_dot(t, jnp.where(m, pr['a2'], 0.0)) for t, pr in zip(tinv, pairs)]
            tinv = [t - _dot(hf, t) for t, hf in zip(tinv, half)]
    for pr, tj in zip(pairs, tinv):
        pr['tk'], pr['tv'] = dot2(tj, pr['kap_s'], pr['a1v'])
    for pr in pairs:
        a4tk, a4tv = dot2(pr['a4'], pr['tk'], pr['tv'])
        pr['reff'] = pr['r_s'] - a4tk
        pr['oc'] = pr['a3v'] - a4tv
        pr['m'] = _dot_tn(pr['tk'], pr['bend_s'])
        pr['q'] = pr['kv'] - _dot_tn(pr['tv'], pr['bend_s'])
    sts = [s_scr[j] for j in range(RWKV_PAIRS)]
    os_ = []
    for i in range(nsub):
        sub = list(zip(pairs[i * RWKV_PAIRS:(i + 1) * RWKV_PAIRS], sts))
        o2s = [_dot_nt(pr['reff'], st) + pr['oc'] for pr, st in sub]
        sts = [st * pr['dec'] - _dot(st, pr['m']) + pr['q'] for pr, st in sub]
        os_ += [o2[0:c] + o2[c:c2] for o2 in o2s]
    for j, st in enumerate(sts):
        s_scr[j] = st
    ds = [o - _dot(o, seg) * (1.0 / RWKV_HD) for o in os_]
    vars_ = [_dot(d * d, seg) * (1.0 / RWKV_HD) for d in ds]
    for pr, d, var in zip(pairs, ds, vars_):
        s, rs = pr['s'], pr['rs']
        on = d * lax.rsqrt(var + RWKV_GN_EPS) * lnw_ref[:, s] + lnb_ref[:, s]
        o_ref[rs, s] = ((on + pr['bonus']) * pr['gate']).astype(BF16)

    @pl.when(ci == pl.num_programs(1) - 1)
    def _():
        sout_ref[0] = s_scr[...]


def _rwkv_mixer(p, shift0, s0, wts, b, t, c, nsub):
    rows = c * nsub
    nc = t // rows
    kern = functools.partial(_rwkv_kernel, c=c, nsub=nsub)
    full = lambda a: pl.BlockSpec(a.shape, lambda i, j: (0,) * a.ndim)
    st_spec = pl.BlockSpec((1, RWKV_PAIRS, LANE, LANE), lambda i, j: (i, 0, 0, 0))
    sh_spec = pl.BlockSpec((1, 1, RWKV_COLS), lambda i, j: (i, 0, 0))
    return pl.pallas_call(
        kern,
        grid=(b, nc),
        in_specs=[pl.BlockSpec((rows, RWKV_COLS), lambda i, j: (i * nc + j, 0)), sh_spec, st_spec]
                 + [full(a) for a in wts],
        out_specs=[pl.BlockSpec((rows, RWKV_WIDTH), lambda i, j: (i * nc + j, 0)), sh_spec, st_spec],
        out_shape=[jax.ShapeDtypeStruct((b * t, RWKV_WIDTH), BF16),
                   jax.ShapeDtypeStruct((b, 1, RWKV_COLS), F32),
                   jax.ShapeDtypeStruct((b, RWKV_PAIRS, LANE, LANE), F32)],
        scratch_shapes=[pltpu.VMEM((RWKV_PAIRS, LANE, LANE), F32), pltpu.VMEM((1, RWKV_COLS), F32)],
        compiler_params=_compiler_params(("arbitrary", "arbitrary")),
    )(p, shift0, s0, *wts)


def _rwkv_cols_in(a):
    return jnp.concatenate([a[..., 0:640], a[..., 704:1984], a[..., 640:704], a[..., 1984:2176]], axis=-1)


def _rwkv_cols_out(a):
    return jnp.concatenate([a[..., 0:640], a[..., 1920:1984], a[..., 640:1920], a[..., 1984:2176]], axis=-1)


def _pad_heads(w, heads, d, dp):
    lead = w.shape[:-1]
    w = w.reshape(lead + (heads, d))
    w = jnp.pad(w, [(0, 0)] * len(lead) + [(0, 0), (0, dp - d)])
    return w.reshape(lead + (heads * dp,))


def _prep_weights(w_in, gla_gate_up, gla_gate_bias, gla_norm, rwkv_mu, rwkv_w_up, rwkv_a_up,
                  rwkv_r_k, w_out):
    gq, gk, gv, gdn, gout, rw, hg = jnp.split(
        w_in, [384, 768, 1536, 1552, 2320, 2320 + RWKV_COLS], axis=-1)
    w_gla = jnp.concatenate([
        _pad_heads(gq, GLA_HEADS, GLA_DK, GLA_DKP), _pad_heads(gk, GLA_HEADS, GLA_DK, GLA_DKP),
        _pad_heads(gv, GLA_HEADS, GLA_DV, GLA_DVP), _pad_heads(gout, GLA_HEADS, GLA_DV, GLA_DVP),
        jnp.pad(gdn, ((0, 0), (0, 0), (0, GLA_RANKP - GLA_RANK)))], axis=-1).astype(BF16)
    w_rwkv = _rwkv_cols_in(rw).astype(BF16)
    w_hgrn = hg.astype(BF16)
    gup = jnp.pad(_pad_heads(gla_gate_up, GLA_HEADS, GLA_DK, GLA_DKP),
                  ((0, 0), (0, GLA_RANKP - GLA_RANK), (0, 0))).astype(BF16)
    gbias = _pad_heads(gla_gate_bias, GLA_HEADS, GLA_DK, GLA_DKP)[:, None, :]
    gnorm = jnp.pad(gla_norm, ((0, 0), (0, GLA_DVP - GLA_DV)))[:, None, :]
    mu = _rwkv_cols_in(rwkv_mu)[:, None, :]
    wup = jnp.pad(rwkv_w_up, ((0, 0), (0, 64), (0, 0))).astype(BF16)
    aup = jnp.pad(rwkv_a_up, ((0, 0), (64, 0), (0, 0))).astype(BF16)
    rk = rwkv_r_k.reshape(DEPTH, 1, RWKV_WIDTH)
    wo_a = w_out[:, :768].reshape(DEPTH, GLA_HEADS, GLA_DV, D_MODEL)
    wo_a = jnp.pad(wo_a, ((0, 0), (0, 0), (0, GLA_DVP - GLA_DV), (0, 0)))
    wo_a = wo_a.reshape(DEPTH, GLA_VW, D_MODEL).astype(BF16)
    wo_b = w_out[:, 768:1408].astype(BF16)
    wo_c = w_out[:, 1408:].astype(BF16)
    return w_gla, w_rwkv, w_hgrn, gup, gbias, gnorm, mu, wup, aup, rk, wo_a, wo_b, wo_c


def _gla_state_in(s):
    s = jnp.swapaxes(s, -1, -2)
    return jnp.pad(s, [(0, 0)] * 3 + [(0, GLA_DVP - GLA_DV), (0, GLA_DKP - GLA_DK)])


def _gla_state_out(s):
    return jnp.swapaxes(s[..., :GLA_DV, :GLA_DK], -1, -2)


def _rwkv_state_in(s):
    d, b = s.shape[:2]
    s = s.reshape(d, b, RWKV_PAIRS, 2, RWKV_HD, RWKV_HD)
    out = jnp.einsum('dbjhvk,hg->dbjhvgk', s, jnp.eye(2, dtype=s.dtype))
    return out.reshape(d, b, RWKV_PAIRS, LANE, LANE)


def _rwkv_state_out(s):
    d, b = s.shape[:2]
    s = s.reshape(d, b, RWKV_PAIRS, 2, RWKV_HD, 2, RWKV_HD)
    out = jnp.stack([s[:, :, :, 0, :, 0, :], s[:, :, :, 1, :, 1, :]], axis=3)
    return out.reshape(d, b, RWKV_HEADS, RWKV_HD, RWKV_HD)


def _row_tile(n, cap):
    t = cap
    while n % t:
        t //= 2
    return t


def _trunk(x, s_gla, s_rwkv, s_shift, s_hgrn, wt, b, t):
    n = b * t
    tm = _row_tile(n, 512)
    c_glr = min(128, t)
    c_rwkv = min(64, t)
    n_rwkv = 4 if t % (4 * c_rwkv) == 0 else 1
    new_gla, new_rwkv, new_shift, new_hgrn = [], [], [], []
    h = _norm(x, wt['norm_mix_pre'][0], tm)
    for l in range(DEPTH):
        pg = _matmul(h, wt['w_gla'][l], tm)
        pr = _matmul(h, wt['w_rwkv'][l], tm)
        ph = _matmul(h, wt['w_hgrn'][l], tm)
        oa, g1 = _gla_mixer(pg, s_gla[l], wt['gup'][l], wt['gbias'][l], wt['gnorm'][l], b, t, c_glr)
        rw = [wt[k][l] for k in ('mu', 'w0', 'wup', 'a0', 'aup', 'g_up', 'k_k', 'k_a', 'rk',
                                 'ln_w', 'ln_b')]
        ob, sh1, r1 = _rwkv_mixer(pr, s_shift[l], s_rwkv[l], rw, b, t, c_rwkv, n_rwkv)
        oc, h1 = _hgrn_mixer(ph, s_hgrn[l], wt['lb_logits'], wt['hnorm'][l], b, t, c_glr, l)
        x, h = _out_proj(oa, ob, oc, wt['wo_a'][l], wt['wo_b'][l], wt['wo_c'][l], x,
                         wt['norm_mix_post'][l], wt['norm_ffn_pre'][l], tm)
        x, h = _ffn(h, x, wt['ffn_w_gate'][l], wt['ffn_w_up'][l], wt['ffn_w_down'][l],
                    wt['norm_ffn_post'][l], wt['norm_mix_pre'][(l + 1) % DEPTH], tm, 512)
        new_gla.append(g1)
        new_rwkv.append(r1)
        new_shift.append(sh1)
        new_hgrn.append(h1)
    return x, jnp.stack(new_gla), jnp.stack(new_rwkv), jnp.stack(new_shift), jnp.stack(new_hgrn)


def _run_path(x, state_gla, state_rwkv, state_shift, state_hgrn, wt):
    b, t, d = x.shape
    y, g, r, sh, h = _trunk(
        x.reshape(b * t, d), _gla_state_in(state_gla), _rwkv_state_in(state_rwkv),
        _rwkv_cols_in(state_shift)[:, :, None, :], jnp.swapaxes(state_hgrn, -1, -2), wt, b, t)
    return (y.reshape(b, t, d), _gla_state_out(g), _rwkv_state_out(r),
            _rwkv_cols_out(sh[:, :, 0, :]), jnp.swapaxes(h, -1, -2))


def kernel(x_prompt, x_sample, state_gla, state_rwkv, state_rwkv_shift, state_hgrn, norm_mix_pre, norm_mix_post, norm_ffn_pre, norm_ffn_post, w_in, gla_gate_up, gla_gate_bias, gla_norm, rwkv_mu, rwkv_w0, rwkv_w_up, rwkv_a0, rwkv_a_up, rwkv_g_up, rwkv_k_k, rwkv_k_a, rwkv_r_k, rwkv_ln_w, rwkv_ln_b, hgrn_lb_logits, hgrn_norm, w_out, ffn_w_gate, ffn_w_up, ffn_w_down):
    (w_gla, w_rwkv, w_hgrn, gup, gbias, gnorm, mu, wup, aup, rk, wo_a, wo_b, wo_c) = _prep_weights(
        w_in, gla_gate_up, gla_gate_bias, gla_norm, rwkv_mu, rwkv_w_up, rwkv_a_up, rwkv_r_k, w_out)
    vec = lambda a: a[:, None, :]
    wt = dict(
        norm_mix_pre=vec(norm_mix_pre), norm_mix_post=vec(norm_mix_post),
        norm_ffn_pre=vec(norm_ffn_pre), norm_ffn_post=vec(norm_ffn_post),
        w_gla=w_gla, w_rwkv=w_rwkv, w_hgrn=w_hgrn, gup=gup, gbias=gbias, gnorm=gnorm,
        mu=mu, w0=vec(rwkv_w0), wup=wup, a0=vec(rwkv_a0), aup=aup, g_up=rwkv_g_up.astype(BF16),
        k_k=vec(rwkv_k_k), k_a=vec(rwkv_k_a), rk=rk, ln_w=vec(rwkv_ln_w), ln_b=vec(rwkv_ln_b),
        lb_logits=hgrn_lb_logits, hnorm=vec(hgrn_norm),
        wo_a=wo_a, wo_b=wo_b, wo_c=wo_c,
        ffn_w_gate=ffn_w_gate.astype(BF16), ffn_w_up=ffn_w_up.astype(BF16),
        ffn_w_down=ffn_w_down.astype(BF16))

    bp = x_prompt.shape[0]
    zeros = lambda s: jnp.zeros((DEPTH, bp) + s.shape[2:], x_prompt.dtype)
    y_p, gla_p, rwkv_p, shift_p, hgrn_p = _run_path(
        x_prompt, zeros(state_gla), zeros(state_rwkv), zeros(state_rwkv_shift), zeros(state_hgrn), wt)
    y_s, gla_s, rwkv_s, shift_s, hgrn_s = _run_path(
        x_sample, state_gla, state_rwkv, state_rwkv_shift, state_hgrn, wt)
    return (y_p, y_s, gla_p, rwkv_p, shift_p, hgrn_p, gla_s, rwkv_s, shift_s, hgrn_s)
```

```python
import functools

import jax
import jax.numpy as jnp
from jax import lax
from jax.experimental import pallas as pl
from jax.experimental.pallas import tpu as pltpu

F32 = jnp.float32
BF16 = jnp.bfloat16

D_MODEL = 2048
DEPTH = 4
EPS = 1e-6

GLA_HEADS, GLA_DK, GLA_DV = 4, 96, 192
GLA_DKP, GLA_DVP = 128, 256
GLA_RANK, GLA_RANKP = 16, 128
GLA_GATE_NORMALIZER = 16.0
GLA_QW = GLA_HEADS * GLA_DKP
GLA_VW = GLA_HEADS * GLA_DVP
GLA_COLS_P = 2 * GLA_QW + 2 * GLA_VW + GLA_RANKP

RWKV_HEADS, RWKV_HD = 10, 64
RWKV_WIDTH = RWKV_HEADS * RWKV_HD
RWKV_PAIRS = RWKV_HEADS // 2
RWKV_COLS = 2176
RWKV_GN_EPS = 64e-5

HGRN_HEADS, HGRN_D = 5, 128
HGRN_WIDTH = HGRN_HEADS * HGRN_D
HGRN_COLS = 4 * HGRN_WIDTH

D_FF = 5632
LANE = 128
LOG2_E = 1.4426950408889634

VMEM_LIMIT = 56 * 1024 * 1024


def _sigmoid(x):
    return 1.0 / (1.0 + jnp.exp(-x))


def _silu(x):
    return x * _sigmoid(x)


def _log_sigmoid(x):
    return jnp.minimum(x, 0.0) - jnp.log(1.0 + jnp.exp(-jnp.abs(x)))


def _dot(a, b):
    return jnp.dot(a.astype(BF16), b.astype(BF16), preferred_element_type=F32)


def _dot_nt(a, b):
    return lax.dot_general(a.astype(BF16), b.astype(BF16), (((1,), (1,)), ((), ())),
                           preferred_element_type=F32)


def _dot_tn(a, b):
    return lax.dot_general(a.astype(BF16), b.astype(BF16), (((0,), (0,)), ((), ())),
                           preferred_element_type=F32)


def _split_bf16(x, parts):
    out = []
    r = x
    for i in range(parts):
        p = r.astype(BF16)
        out.append(p)
        if i + 1 < parts:
            r = r - p.astype(F32)
    return out


def _dot_exact_lhs(m_bf16, x, parts=3):
    acc = None
    for p in _split_bf16(x, parts):
        t = jnp.dot(m_bf16, p, preferred_element_type=F32)
        acc = t if acc is None else acc + t
    return acc


def _iota(shape, dim):
    return lax.broadcasted_iota(jnp.int32, shape, dim)


def _tril_ones(c):
    return jnp.where(_iota((c, c), 0) >= _iota((c, c), 1), 1.0, 0.0).astype(BF16)


def _rms_rows(x, g):
    ms = jnp.mean(x * x, axis=-1, keepdims=True)
    return x * lax.rsqrt(ms + EPS) * g


def _compiler_params(sem, flags=None):
    return pltpu.CompilerParams(dimension_semantics=sem, vmem_limit_bytes=VMEM_LIMIT, flags=flags)


def _norm_kernel(x_ref, g_ref, h_ref):
    h_ref[...] = _rms_rows(x_ref[...], g_ref[...]).astype(BF16)


def _norm(x, g, tm):
    n, d = x.shape
    return pl.pallas_call(
        _norm_kernel,
        grid=(n // tm,),
        in_specs=[pl.BlockSpec((tm, d), lambda i: (i, 0)), pl.BlockSpec((1, d), lambda i: (0, 0))],
        out_specs=pl.BlockSpec((tm, d), lambda i: (i, 0)),
        out_shape=jax.ShapeDtypeStruct((n, d), BF16),
        compiler_params=_compiler_params(("arbitrary",)),
    )(x, g)


def _matmul_kernel(h_ref, w_ref, o_ref):
    o_ref[...] = jnp.dot(h_ref[...], w_ref[...], preferred_element_type=F32)


def _matmul(h, w, tm):
    n, d = h.shape
    cout = w.shape[1]
    return pl.pallas_call(
        _matmul_kernel,
        grid=(n // tm,),
        in_specs=[pl.BlockSpec((tm, d), lambda i: (i, 0)),
                  pl.BlockSpec((d, cout), lambda i: (0, 0), pipeline_mode=pl.Buffered(1))],
        out_specs=pl.BlockSpec((tm, cout), lambda i: (i, 0)),
        out_shape=jax.ShapeDtypeStruct((n, cout), F32),
        compiler_params=_compiler_params(("arbitrary",)),
    )(h, w)


def _out_proj_kernel(oa_ref, ob_ref, oc_ref, wa_ref, wb_ref, wc_ref, x_ref, g_ref, gnext_ref,
                     y_ref, hn_ref):
    m = (jnp.dot(oa_ref[...], wa_ref[...], preferred_element_type=F32)
         + jnp.dot(ob_ref[...], wb_ref[...], preferred_element_type=F32)
         + jnp.dot(oc_ref[...], wc_ref[...], preferred_element_type=F32))
    y = x_ref[...] + _rms_rows(m, g_ref[...])
    y_ref[...] = y
    hn_ref[...] = _rms_rows(y, gnext_ref[...]).astype(BF16)


def _out_proj(oa, ob, oc, wa, wb, wc, x, g, gnext, tm):
    n, d = x.shape
    row = lambda w: pl.BlockSpec((tm, w), lambda i: (i, 0))
    full = lambda a: pl.BlockSpec(a.shape, lambda i: (0, 0))
    return pl.pallas_call(
        _out_proj_kernel,
        grid=(n // tm,),
        in_specs=[row(oa.shape[1]), row(ob.shape[1]), row(oc.shape[1]),
                  full(wa), full(wb), full(wc), row(d), full(g), full(gnext)],
        out_specs=[row(d), row(d)],
        out_shape=[jax.ShapeDtypeStruct((n, d), F32), jax.ShapeDtypeStruct((n, d), BF16)],
        compiler_params=_compiler_params(("arbitrary",)),
    )(oa, ob, oc, wa, wb, wc, x, g, gnext)


def _ffn_kernel(h_ref, x_ref, wg_ref, wu_ref, wd_ref, gpost_ref, gnext_ref, y_ref, hn_ref, acc_scr):
    j = pl.program_id(1)

    @pl.when(j == 0)
    def _():
        acc_scr[...] = jnp.zeros_like(acc_scr)

    h = h_ref[...]
    a = jnp.dot(h, wg_ref[...], preferred_element_type=F32)
    u = jnp.dot(h, wu_ref[...], preferred_element_type=F32)
    acc_scr[...] += jnp.dot((_silu(a) * u).astype(BF16), wd_ref[...], preferred_element_type=F32)

    @pl.when(j == pl.num_programs(1) - 1)
    def _():
        y = x_ref[...] + _rms_rows(acc_scr[...], gpost_ref[...])
        y_ref[...] = y
        hn_ref[...] = _rms_rows(y, gnext_ref[...]).astype(BF16)


def _ffn(h, x, wg, wu, wd, gpost, gnext, tm, tf):
    n, d = x.shape
    dff = wg.shape[1]
    return pl.pallas_call(
        _ffn_kernel,
        grid=(n // tm, dff // tf),
        in_specs=[pl.BlockSpec((tm, d), lambda i, j: (i, 0)),
                  pl.BlockSpec((tm, d), lambda i, j: (i, 0)),
                  pl.BlockSpec((d, tf), lambda i, j: (0, j)),
                  pl.BlockSpec((d, tf), lambda i, j: (0, j)),
                  pl.BlockSpec((tf, d), lambda i, j: (j, 0)),
                  pl.BlockSpec((1, d), lambda i, j: (0, 0)),
                  pl.BlockSpec((1, d), lambda i, j: (0, 0))],
        out_specs=[pl.BlockSpec((tm, d), lambda i, j: (i, 0)),
                   pl.BlockSpec((tm, d), lambda i, j: (i, 0))],
        out_shape=[jax.ShapeDtypeStruct((n, d), F32), jax.ShapeDtypeStruct((n, d), BF16)],
        scratch_shapes=[pltpu.VMEM((tm, d), F32)],
        compiler_params=_compiler_params(("arbitrary", "arbitrary")),
    )(h, x, wg, wu, wd, gpost, gnext)


class _GlrMasks:
    def __init__(self, c):
        self.c = c
        row_a = _iota((c, c), 0)
        col_a = _iota((c, c), 1)
        self.tril = _tril_ones(c)
        self.level = {}
        h = 1
        while 2 * h <= c:
            self.level[h] = (((row_a // (2 * h)) == (col_a // (2 * h)))
                             & ((row_a // h) % 2 == 1) & ((col_a // h) % 2 == 0))
            h *= 2
        self.eye = row_a == col_a


def _small_level_factors(g):
    c = g.shape[0]
    r4 = _iota(g.shape, 0) % 4
    g_prev = pltpu.roll(g, 1, 0)
    g_next = pltpu.roll(g, c - 1, 0)
    e1 = jnp.exp2(jnp.where(r4 % 2 == 1, g, 0.0))
    e2 = jnp.exp2(jnp.where(r4 == 3, g + g_prev, jnp.where(r4 == 2, g, jnp.where(r4 == 0, g_next, 0.0))))
    return e1, e2


def _glr_chunk(q, k, v, b, e1, e2, st, mk):
    c, dk = q.shape
    att = jnp.where(mk.eye, _dot_nt(q, k), 0.0)
    for h, pair_mask in mk.level.items():
        if h == 1:
            e = e1
        elif h == 2:
            e = e2
        else:
            n = c // (2 * h)
            ref = b.reshape(n, 2 * h, dk)[:, h - 1:h, :]
            refb = jnp.broadcast_to(ref, (n, 2 * h, dk)).reshape(c, dk)
            e = jnp.exp2(-jnp.abs(b - refb))
        att = jnp.where(pair_mask, _dot_nt(q * e, k * e), att)
    o = _dot(att, v) + _dot_nt(q * jnp.exp2(b), st)
    b_end = b[c - 1:c, :]
    st_new = st * jnp.exp2(b_end) + _dot_tn(v, k * jnp.exp2(b_end - b))
    return o, st_new


def _ahead_groups(cols, slots):
    tiles = -(-cols // (2 * LANE))
    bounds = [min(cols, 2 * LANE * ((k * tiles) // slots)) for k in range(slots)] + [cols]
    return [slice(bounds[k], bounds[k + 1]) for k in range(slots)]


def _gla_kernel(hc_ref, hn_ref, w_ref, s0_ref, gup_ref, gbias_ref, gnorm_ref, o_ref, sout_ref,
                s_scr, pa_scr, pb_scr, *, c, nsub):
    ci = pl.program_id(1)
    half = max(nsub // 2, 1)
    hrows = c * half
    groups = _ahead_groups(GLA_COLS_P, GLA_HEADS * half)

    @pl.when(ci == 0)
    def _():
        s_scr[...] = s0_ref[0]

    @pl.when(jnp.logical_or(ci == 0, nsub == 1))
    def _():
        pa_scr[...] = jnp.dot(hc_ref[0:hrows, :], w_ref[...], preferred_element_type=F32)

    mk = _GlrMasks(c)
    gnorm = gnorm_ref[...]
    for i in range(nsub):
        rs = slice(i * c, (i + 1) * c)
        src = pa_scr if i < half else pb_scr
        p = src[(i % half) * c:(i % half + 1) * c, :]
        q_all = p[:, 0:GLA_QW] * (GLA_DK ** -0.5)
        k_all = p[:, GLA_QW:2 * GLA_QW]
        v_all = p[:, 2 * GLA_QW:2 * GLA_QW + GLA_VW]
        gout = p[:, 2 * GLA_QW + GLA_VW:2 * GLA_QW + 2 * GLA_VW]
        gdown = p[:, 2 * GLA_QW + 2 * GLA_VW:]
        g_all = (_log_sigmoid(_dot(gdown, gup_ref[...]) + gbias_ref[...])
                 * (LOG2_E / GLA_GATE_NORMALIZER))
        b_all = _dot_exact_lhs(mk.tril, g_all)
        e1_all, e2_all = _small_level_factors(g_all)
        for h in range(GLA_HEADS):
            ks = slice(h * GLA_DKP, (h + 1) * GLA_DKP)
            vs = slice(h * GLA_DVP, (h + 1) * GLA_DVP)
            o, st = _glr_chunk(q_all[:, ks], k_all[:, ks], v_all[:, vs], b_all[:, ks],
                               e1_all[:, ks], e2_all[:, ks], s_scr[h], mk)
            s_scr[h] = st
            ms = jnp.sum(o * o, axis=-1, keepdims=True) * (1.0 / GLA_DV)
            o = o * lax.rsqrt(ms + EPS) * gnorm * _silu(gout[:, vs])
            o_ref[rs, vs] = o.astype(BF16)
            if nsub > 1:
                g = groups[(i % half) * GLA_HEADS + h]
                ahead, dst = (hc_ref[hrows:, :], pb_scr) if i < half else (hn_ref[0:hrows, :], pa_scr)
                dst[:, g] = jnp.dot(ahead, w_ref[:, g], preferred_element_type=F32)

    @pl.when(ci == pl.num_programs(1) - 1)
    def _():
        sout_ref[0] = s_scr[...]


def _gla_mixer(hx, w, s0t, gup, gbias, gnorm, b, t, c):
    nsub = 4 if t % (4 * c) == 0 else 1
    rows = c * nsub
    hrows = c * max(nsub // 2, 1)
    nc = t // rows
    nblk = b * nc
    kern = functools.partial(_gla_kernel, c=c, nsub=nsub)
    full = lambda a: pl.BlockSpec(a.shape, lambda i, j: (0,) * a.ndim)
    st_spec = pl.BlockSpec((1, GLA_HEADS, GLA_DVP, GLA_DKP), lambda i, j: (i, 0, 0, 0))
    d = hx.shape[1]
    return pl.pallas_call(
        kern,
        grid=(b, nc),
        in_specs=[pl.BlockSpec((rows, d), lambda i, j: (i * nc + j, 0)),
                  pl.BlockSpec((rows, d), lambda i, j: (jnp.minimum(i * nc + j + 1, nblk - 1), 0)),
                  pl.BlockSpec(w.shape, lambda i, j: (0, 0), pipeline_mode=pl.Buffered(1)),
                  st_spec, full(gup), full(gbias), full(gnorm)],
        out_specs=[pl.BlockSpec((rows, GLA_VW), lambda i, j: (i * nc + j, 0)), st_spec],
        out_shape=[jax.ShapeDtypeStruct((b * t, GLA_VW), BF16),
                   jax.ShapeDtypeStruct((b, GLA_HEADS, GLA_DVP, GLA_DKP), F32)],
        scratch_shapes=[pltpu.VMEM((GLA_HEADS, GLA_DVP, GLA_DKP), F32),
                        pltpu.VMEM((hrows, GLA_COLS_P), F32), pltpu.VMEM((hrows, GLA_COLS_P), F32)],
        compiler_params=_compiler_params(("arbitrary", "arbitrary")),
    )(hx, hx, w, s0t, gup, gbias, gnorm)


def _hgrn_kernel(hc_ref, hn_ref, w_ref, s0_ref, lbl_ref, hnorm_ref, o_ref, sout_ref,
                 s_scr, pa_scr, pb_scr, *, c, nsub, layer):
    ci = pl.program_id(1)
    half = max(nsub // 2, 1)
    hrows = c * half
    gw = HGRN_COLS // (HGRN_HEADS * half)

    @pl.when(ci == 0)
    def _():
        s_scr[...] = s0_ref[0]

    @pl.when(jnp.logical_or(ci == 0, nsub == 1))
    def _():
        pa_scr[...] = jnp.dot(hc_ref[0:hrows, :], w_ref[...], preferred_element_type=F32)

    lg = lbl_ref[...]
    e = jnp.exp(lg - jnp.max(lg, axis=0, keepdims=True))
    prob = e / jnp.sum(e, axis=0, keepdims=True)
    lb = jnp.zeros((1, HGRN_WIDTH), F32)
    for i in range(1, layer + 1):
        lb = lb + prob[i:i + 1, :]

    mk = _GlrMasks(c)
    hnorm = hnorm_ref[...]

    for i in range(nsub):
        rs = slice(i * c, (i + 1) * c)
        src = pa_scr if i < half else pb_scr
        p = src[(i % half) * c:(i % half + 1) * c, :]
        hq = p[:, 0:HGRN_WIDTH]
        hf = p[:, HGRN_WIDTH:2 * HGRN_WIDTH]
        hi = p[:, 2 * HGRN_WIDTH:3 * HGRN_WIDTH]
        hg = p[:, 3 * HGRN_WIDTH:]
        q_all = _silu(hq)
        f_all = lb + (1.0 - lb) * _sigmoid(hf)
        k_all = 1.0 - f_all
        g_all = jnp.log2(f_all)
        b_all = _dot_exact_lhs(mk.tril, g_all)
        e1_all, e2_all = _small_level_factors(g_all)
        for h in range(HGRN_HEADS):
            s = slice(h * HGRN_D, (h + 1) * HGRN_D)
            o, st = _glr_chunk(q_all[:, s], k_all[:, s], hi[:, s], b_all[:, s],
                               e1_all[:, s], e2_all[:, s], s_scr[h], mk)
            s_scr[h] = st
            o = _rms_rows(o, hnorm) * _silu(hg[:, s])
            o_ref[rs, s] = o.astype(BF16)
            if nsub > 1:
                k = (i % half) * HGRN_HEADS + h
                g = slice(k * gw, (k + 1) * gw)
                ahead, dst = (hc_ref[hrows:, :], pb_scr) if i < half else (hn_ref[0:hrows, :], pa_scr)
                dst[:, g] = jnp.dot(ahead, w_ref[:, g], preferred_element_type=F32)

    @pl.when(ci == pl.num_programs(1) - 1)
    def _():
        sout_ref[0] = s_scr[...]


def _hgrn_mixer(hx, w, s0t, lb_logits, hnorm, b, t, c, layer):
    nsub = 4 if t % (4 * c) == 0 else 1
    rows = c * nsub
    hrows = c * max(nsub // 2, 1)
    nc = t // rows
    nblk = b * nc
    kern = functools.partial(_hgrn_kernel, c=c, nsub=nsub, layer=layer)
    full = lambda a: pl.BlockSpec(a.shape, lambda i, j: (0,) * a.ndim)
    st_spec = pl.BlockSpec((1, HGRN_HEADS, HGRN_D, HGRN_D), lambda i, j: (i, 0, 0, 0))
    d = hx.shape[1]
    return pl.pallas_call(
        kern,
        grid=(b, nc),
        in_specs=[pl.BlockSpec((rows, d), lambda i, j: (i * nc + j, 0)),
                  pl.BlockSpec((rows, d), lambda i, j: (jnp.minimum(i * nc + j + 1, nblk - 1), 0)),
                  pl.BlockSpec(w.shape, lambda i, j: (0, 0), pipeline_mode=pl.Buffered(1)),
                  st_spec, full(lb_logits), full(hnorm)],
        out_specs=[pl.BlockSpec((rows, HGRN_WIDTH), lambda i, j: (i * nc + j, 0)), st_spec],
        out_shape=[jax.ShapeDtypeStruct((b * t, HGRN_WIDTH), BF16),
                   jax.ShapeDtypeStruct((b, HGRN_HEADS, HGRN_D, HGRN_D), F32)],
        scratch_shapes=[pltpu.VMEM((HGRN_HEADS, HGRN_D, HGRN_D), F32),
                        pltpu.VMEM((hrows, HGRN_COLS), F32), pltpu.VMEM((hrows, HGRN_COLS), F32)],
        compiler_params=_compiler_params(("arbitrary", "arbitrary")),
    )(hx, hx, w, s0t, lb_logits, hnorm)


def _rwkv_kernel(p_ref, shift0_ref, s0_ref, mu_ref, w0_ref, wup_ref, a0_ref, aup_ref, gup_ref,
                 kk_ref, ka_ref, rk_ref, lnw_ref, lnb_ref,
                 o_ref, shift_ref, sout_ref, s_scr, carry_scr, *, c, nsub):
    ci = pl.program_id(1)

    @pl.when(ci == 0)
    def _():
        s_scr[...] = s0_ref[0]
        carry_scr[...] = shift0_ref[0]

    w = RWKV_WIDTH
    rows = c * nsub
    grp = min(nsub, 2)
    grows = c * grp
    first_prev = carry_scr[...]
    last = p_ref[rows - 1:rows, :]
    carry_scr[...] = last
    shift_ref[0] = last

    c2 = 2 * c
    lane = _iota((c, LANE), 1)
    lo_half = lane < RWKV_HD
    ri = _iota((c2, c2), 0)
    cj = _iota((c2, c2), 1)
    strict = (ri % c) > (cj % c)
    incl = (ri % c) >= (cj % c)
    eye = jnp.where(ri == cj, 1.0, 0.0)
    inv_levels = []
    blk = 1
    while blk < c:
        inv_levels.append(((ri // (2 * blk)) == (cj // (2 * blk)))
                          & ((ri // blk) % 2 == 1) & ((cj // blk) % 2 == 0))
        blk *= 2
    tr, tc = _iota((grows, grows), 0), _iota((grows, grows), 1)
    tril = jnp.where((tr >= tc) & (tr // c == tc // c), 1.0, 0.0).astype(BF16)
    row = _iota((grows, RWKV_COLS), 0)
    seg = jnp.where((_iota((LANE, LANE), 0) // RWKV_HD) == (_iota((LANE, LANE), 1) // RWKV_HD),
                    1.0, 0.0).astype(BF16)

    def stack(x):
        return jnp.concatenate([jnp.where(lo_half, x, 0.0), jnp.where(lo_half, 0.0, x)], axis=0)

    def head_sum(x):
        return jnp.concatenate([_dot(x[:, j * LANE:(j + 1) * LANE], seg)
                                for j in range(RWKV_PAIRS)], axis=1)

    def prepare(g0):
        p = p_ref[g0:g0 + grows, :]
        before = first_prev if g0 == 0 else p_ref[g0 - 1:g0, :]
        prev = jnp.where(row == 0, before, pltpu.roll(p, 1, 0))
        pm = p + (prev - p) * mu_ref[...]
        r_all = pm[:, 0:w]
        k_all = pm[:, w:2 * w]
        v_all = pm[:, 2 * w:3 * w]
        wa = pm[:, 3 * w:3 * w + LANE]
        gd = pm[:, 3 * w + LANE:]
        lw_all = -jnp.exp(_log_sigmoid(w0_ref[...] + _dot(jnp.tanh(wa), wup_ref[...])) - 0.5)
        a_all = _sigmoid(a0_ref[...] + _dot(wa, aup_ref[...]))
        gate_all = _dot(_sigmoid(gd), gup_ref[...])
        kk_all = k_all * kk_ref[...]
        kk_all = kk_all * lax.rsqrt(jnp.maximum(head_sum(kk_all * kk_all), 1e-24))
        k_all = k_all * (1.0 + (a_all - 1.0) * ka_ref[...])
        beta_all = kk_all * a_all
        bonus_all = head_sum(r_all * k_all * rk_ref[...]) * v_all
        cin_all = _dot_exact_lhs(tril, lw_all, parts=2)
        ginv = jnp.exp(-cin_all)
        kap_all = kk_all * jnp.exp(cin_all - lw_all)
        rt_all = r_all * jnp.exp(cin_all)
        kt_all = k_all * ginv
        bt_all = beta_all * ginv
        out = []
        for i in range(grp):
            rs = slice(i * c, (i + 1) * c)
            cend = cin_all[(i + 1) * c - 1:(i + 1) * c, :]
            gend = jnp.exp(cend - cin_all[rs])
            kend, bend, dec = k_all[rs] * gend, beta_all[rs] * gend, jnp.exp(cend)
            for j in range(RWKV_PAIRS):
                s = slice(j * LANE, (j + 1) * LANE)
                out.append(dict(
                    s=s, rs=slice(g0 + i * c, g0 + (i + 1) * c),
                    kap_s=stack(kap_all[rs, s]), r_s=stack(rt_all[rs, s]), k_s=stack(kt_all[rs, s]),
                    b_s=stack(bt_all[rs, s]), v_s=stack(v_all[rs, s]), kend_s=stack(kend[:, s]),
                    bend_s=stack(bend[:, s]), dec=dec[:, s], bonus=bonus_all[rs, s],
                    gate=gate_all[rs, s]))
        return out

    def dot_nt2(lhs, r1, r2):
        if r1.shape[0] % LANE:
            return _dot_nt(lhs, r1), _dot_nt(lhs, r2)
        both = _dot_nt(lhs, jnp.concatenate([r1, r2], axis=0))
        return both[:, :r1.shape[0]], both[:, r1.shape[0]:]

    def dot2(lhs, r1, r2):
        both = _dot(lhs, jnp.concatenate([r1, r2], axis=1))
        return both[:, :r1.shape[1]], both[:, r1.shape[1]:]

    pairs = []
    for g0 in range(0, rows, grows):
        group = prepare(g0)
        for pr in group:
            a1, a2 = dot_nt2(pr['kap_s'], pr['k_s'], pr['b_s'])
            pr['a1'], pr['a2'] = jnp.where(strict, a1, 0.0), jnp.where(strict, a2, 0.0)
        for pr in group:
            a3, a4 = dot_nt2(pr['r_s'], pr['k_s'], pr['b_s'])
            pr['a3'], pr['a4'] = jnp.where(incl, a3, 0.0), jnp.where(incl, a4, 0.0)
        for pr in group:
            pr['a1v'] = _dot(pr['a1'], pr['v_s'])
            pr['a3v'] = _dot(pr['a3'], pr['v_s'])
            pr['kv'] = _dot_tn(pr['v_s'], pr['kend_s'])
        pairs += group
    if c2 % LANE == 0 and len(pairs) % 2 == 0:
        left = _iota((c2, 2 * c2), 1) < c2
        tile2 = lambda x: jnp.concatenate([x, x], axis=1)

        def block_diag(x, keep=None):
            top = left if keep is None else keep & left
            bot = ~left if keep is None else keep & ~left
            x = x.astype(BF16)
            return jnp.concatenate([jnp.where(top, x, 0.0), jnp.where(bot, x, 0.0)], axis=0)

        a2w = [jnp.concatenate([pa['a2'], pb['a2']], axis=1) for pa, pb in zip(pairs[0::2], pairs[1::2])]
        tw = [tile2(eye) - jnp.where(tile2(inv_levels[0]), a, 0.0) for a in a2w]
        for m in inv_levels[1:]:
            mw = tile2(m)
            half = [_dot(t, block_diag(a, mw)) for t, a in zip(tw, a2w)]
            tw = [t - _dot(hf, block_diag(t)) for t, hf in zip(tw, half)]
        tinv = [x for t in tw for x in (t[:, :c2], t[:, c2:])]
    else:
        tinv = [eye - jnp.where(inv_levels[0], pr['a2'], 0.0) for pr in pairs]
        for m in inv_levels[1:]:
            half = [_dot(t, jnp.where(m, pr['a2'], 0.0)) for t, pr in zip(tinv, pairs)]
            tinv = [t - _dot(hf, t) for t, hf in zip(tinv, half)]
    for pr, tj in zip(pairs, tinv):
        pr['tk'], pr['tv'] = dot2(tj, pr['kap_s'], pr['a1v'])
    for pr in pairs:
        a4tk, a4tv = dot2(pr['a4'], pr['tk'], pr['tv'])
        pr['reff'] = pr['r_s'] - a4tk
        pr['oc'] = pr['a3v'] - a4tv
        pr['m'] = _dot_tn(pr['tk'], pr['bend_s'])
        pr['q'] = pr['kv'] - _dot_tn(pr['tv'], pr['bend_s'])
    sts = [s_scr[j] for j in range(RWKV_PAIRS)]
    os_ = []
    for i in range(nsub):
        sub = list(zip(pairs[i * RWKV_PAIRS:(i + 1) * RWKV_PAIRS], sts))
        o2s = [_dot_nt(pr['reff'], st) + pr['oc'] for pr, st in sub]
        sts = [st * pr['dec'] - _dot(st, pr['m']) + pr['q'] for pr, st in sub]
        os_ += [o2[0:c] + o2[c:c2] for o2 in o2s]
    for j, st in enumerate(sts):
        s_scr[j] = st
    ds = [o - _dot(o, seg) * (1.0 / RWKV_HD) for o in os_]
    vars_ = [_dot(d * d, seg) * (1.0 / RWKV_HD) for d in ds]
    for pr, d, var in zip(pairs, ds, vars_):
        s, rs = pr['s'], pr['rs']
        on = d * lax.rsqrt(var + RWKV_GN_EPS) * lnw_ref[:, s] + lnb_ref[:, s]
        o_ref[rs, s] = ((on + pr['bonus']) * pr['gate']).astype(BF16)

    @pl.when(ci == pl.num_programs(1) - 1)
    def _():
        sout_ref[0] = s_scr[...]


def _rwkv_mixer(p, shift0, s0, wts, b, t, c, nsub):
    rows = c * nsub
    nc = t // rows
    kern = functools.partial(_rwkv_kernel, c=c, nsub=nsub)
    full = lambda a: pl.BlockSpec(a.shape, lambda i, j: (0,) * a.ndim)
    st_spec = pl.BlockSpec((1, RWKV_PAIRS, LANE, LANE), lambda i, j: (i, 0, 0, 0))
    sh_spec = pl.BlockSpec((1, 1, RWKV_COLS), lambda i, j: (i, 0, 0))
    return pl.pallas_call(
        kern,
        grid=(b, nc),
        in_specs=[pl.BlockSpec((rows, RWKV_COLS), lambda i, j: (i * nc + j, 0)), sh_spec, st_spec]
                 + [full(a) for a in wts],
        out_specs=[pl.BlockSpec((rows, RWKV_WIDTH), lambda i, j: (i * nc + j, 0)), sh_spec, st_spec],
        out_shape=[jax.ShapeDtypeStruct((b * t, RWKV_WIDTH), BF16),
                   jax.ShapeDtypeStruct((b, 1, RWKV_COLS), F32),
                   jax.ShapeDtypeStruct((b, RWKV_PAIRS, LANE, LANE), F32)],
        scratch_shapes=[pltpu.VMEM((RWKV_PAIRS, LANE, LANE), F32), pltpu.VMEM((1, RWKV_COLS), F32)],
        compiler_params=_compiler_params(("arbitrary", "arbitrary")),
    )(p, shift0, s0, *wts)


def _rwkv_cols_in(a):
    return jnp.concatenate([a[..., 0:640], a[..., 704:1984], a[..., 640:704], a[..., 1984:2176]], axis=-1)


def _rwkv_cols_out(a):
    return jnp.concatenate([a[..., 0:640], a[..., 1920:1984], a[..., 640:1920], a[..., 1984:2176]], axis=-1)


def _pad_heads(w, heads, d, dp):
    lead = w.shape[:-1]
    w = w.reshape(lead + (heads, d))
    w = jnp.pad(w, [(0, 0)] * len(lead) + [(0, 0), (0, dp - d)])
    return w.reshape(lead + (heads * dp,))


def _prep_weights(w_in, gla_gate_up, gla_gate_bias, gla_norm, rwkv_mu, rwkv_w_up, rwkv_a_up,
                  rwkv_r_k, w_out):
    gq, gk, gv, gdn, gout, rw, hg = jnp.split(
        w_in, [384, 768, 1536, 1552, 2320, 2320 + RWKV_COLS], axis=-1)
    w_gla = jnp.concatenate([
        _pad_heads(gq, GLA_HEADS, GLA_DK, GLA_DKP), _pad_heads(gk, GLA_HEADS, GLA_DK, GLA_DKP),
        _pad_heads(gv, GLA_HEADS, GLA_DV, GLA_DVP), _pad_heads(gout, GLA_HEADS, GLA_DV, GLA_DVP),
        jnp.pad(gdn, ((0, 0), (0, 0), (0, GLA_RANKP - GLA_RANK)))], axis=-1).astype(BF16)
    w_rwkv = _rwkv_cols_in(rw).astype(BF16)
    w_hgrn = hg.astype(BF16)
    gup = jnp.pad(_pad_heads(gla_gate_up, GLA_HEADS, GLA_DK, GLA_DKP),
                  ((0, 0), (0, GLA_RANKP - GLA_RANK), (0, 0))).astype(BF16)
    gbias = _pad_heads(gla_gate_bias, GLA_HEADS, GLA_DK, GLA_DKP)[:, None, :]
    gnorm = jnp.pad(gla_norm, ((0, 0), (0, GLA_DVP - GLA_DV)))[:, None, :]
    mu = _rwkv_cols_in(rwkv_mu)[:, None, :]
    wup = jnp.pad(rwkv_w_up, ((0, 0), (0, 64), (0, 0))).astype(BF16)
    aup = jnp.pad(rwkv_a_up, ((0, 0), (64, 0), (0, 0))).astype(BF16)
    rk = rwkv_r_k.reshape(DEPTH, 1, RWKV_WIDTH)
    wo_a = w_out[:, :768].reshape(DEPTH, GLA_HEADS, GLA_DV, D_MODEL)
    wo_a = jnp.pad(wo_a, ((0, 0), (0, 0), (0, GLA_DVP - GLA_DV), (0, 0)))
    wo_a = wo_a.reshape(DEPTH, GLA_VW, D_MODEL).astype(BF16)
    wo_b = w_out[:, 768:1408].astype(BF16)
    wo_c = w_out[:, 1408:].astype(BF16)
    return w_gla, w_rwkv, w_hgrn, gup, gbias, gnorm, mu, wup, aup, rk, wo_a, wo_b, wo_c


def _gla_state_in(s):
    s = jnp.swapaxes(s, -1, -2)
    return jnp.pad(s, [(0, 0)] * 3 + [(0, GLA_DVP - GLA_DV), (0, GLA_DKP - GLA_DK)])


def _gla_state_out(s):
    return jnp.swapaxes(s[..., :GLA_DV, :GLA_DK], -1, -2)


def _rwkv_state_in(s):
    d, b = s.shape[:2]
    s = s.reshape(d, b, RWKV_PAIRS, 2, RWKV_HD, RWKV_HD)
    out = jnp.einsum('dbjhvk,hg->dbjhvgk', s, jnp.eye(2, dtype=s.dtype))
    return out.reshape(d, b, RWKV_PAIRS, LANE, LANE)


def _rwkv_state_out(s):
    d, b = s.shape[:2]
    s = s.reshape(d, b, RWKV_PAIRS, 2, RWKV_HD, 2, RWKV_HD)
    out = jnp.stack([s[:, :, :, 0, :, 0, :], s[:, :, :, 1, :, 1, :]], axis=3)
    return out.reshape(d, b, RWKV_HEADS, RWKV_HD, RWKV_HD)


def _row_tile(n, cap):
    t = cap
    while n % t:
        t //= 2
    return t


def _trunk(x, s_gla, s_rwkv, s_shift, s_hgrn, wt, b, t):
    n = b * t
    tm = _row_tile(n, 512)
    c_glr = min(128, t)
    c_rwkv = min(64, t)
    n_rwkv = 4 if t % (4 * c_rwkv) == 0 else 1
    new_gla, new_rwkv, new_shift, new_hgrn = [], [], [], []
    h = _norm(x, wt['norm_mix_pre'][0], tm)
    for l in range(DEPTH):
        pr = _matmul(h, wt['w_rwkv'][l], tm)
        oa, g1 = _gla_mixer(h, wt['w_gla'][l], s_gla[l], wt['gup'][l], wt['gbias'][l],
                            wt['gnorm'][l], b, t, c_glr)
        rw = [wt[k][l] for k in ('mu', 'w0', 'wup', 'a0', 'aup', 'g_up', 'k_k', 'k_a', 'rk',
                                 'ln_w', 'ln_b')]
        ob, sh1, r1 = _rwkv_mixer(pr, s_shift[l], s_rwkv[l], rw, b, t, c_rwkv, n_rwkv)
        oc, h1 = _hgrn_mixer(h, wt['w_hgrn'][l], s_hgrn[l], wt['lb_logits'], wt['hnorm'][l],
                             b, t, c_glr, l)
        x, h = _out_proj(oa, ob, oc, wt['wo_a'][l], wt['wo_b'][l], wt['wo_c'][l], x,
                         wt['norm_mix_post'][l], wt['norm_ffn_pre'][l], tm)
        x, h = _ffn(h, x, wt['ffn_w_gate'][l], wt['ffn_w_up'][l], wt['ffn_w_down'][l],
                    wt['norm_ffn_post'][l], wt['norm_mix_pre'][(l + 1) % DEPTH], tm, 512)
        new_gla.append(g1)
        new_rwkv.append(r1)
        new_shift.append(sh1)
        new_hgrn.append(h1)
    return x, jnp.stack(new_gla), jnp.stack(new_rwkv), jnp.stack(new_shift), jnp.stack(new_hgrn)


def _run_path(x, state_gla, state_rwkv, state_shift, state_hgrn, wt):
    b, t, d = x.shape
    y, g, r, sh, h = _trunk(
        x.reshape(b * t, d), _gla_state_in(state_gla), _rwkv_state_in(state_rwkv),
        _rwkv_cols_in(state_shift)[:, :, None, :], jnp.swapaxes(state_hgrn, -1, -2), wt, b, t)
    return (y.reshape(b, t, d), _gla_state_out(g), _rwkv_state_out(r),
            _rwkv_cols_out(sh[:, :, 0, :]), jnp.swapaxes(h, -1, -2))


def kernel(x_prompt, x_sample, state_gla, state_rwkv, state_rwkv_shift, state_hgrn, norm_mix_pre, norm_mix_post, norm_ffn_pre, norm_ffn_post, w_in, gla_gate_up, gla_gate_bias, gla_norm, rwkv_mu, rwkv_w0, rwkv_w_up, rwkv_a0, rwkv_a_up, rwkv_g_up, rwkv_k_k, rwkv_k_a, rwkv_r_k, rwkv_ln_w, rwkv_ln_b, hgrn_lb_logits, hgrn_norm, w_out, ffn_w_gate, ffn_w_up, ffn_w_down):
    (w_gla, w_rwkv, w_hgrn, gup, gbias, gnorm, mu, wup, aup, rk, wo_a, wo_b, wo_c) = _prep_weights(
        w_in, gla_gate_up, gla_gate_bias, gla_norm, rwkv_mu, rwkv_w_up, rwkv_a_up, rwkv_r_k, w_out)
    vec = lambda a: a[:, None, :]
    wt = dict(
        norm_mix_pre=vec(norm_mix_pre), norm_mix_post=vec(norm_mix_post),
        norm_ffn_pre=vec(norm_ffn_pre), norm_ffn_post=vec(norm_ffn_post),
        w_gla=w_gla, w_rwkv=w_rwkv, w_hgrn=w_hgrn, gup=gup, gbias=gbias, gnorm=gnorm,
        mu=mu, w0=vec(rwkv_w0), wup=wup, a0=vec(rwkv_a0), aup=aup, g_up=rwkv_g_up.astype(BF16),
        k_k=vec(rwkv_k_k), k_a=vec(rwkv_k_a), rk=rk, ln_w=vec(rwkv_ln_w), ln_b=vec(rwkv_ln_b),
        lb_logits=hgrn_lb_logits, hnorm=vec(hgrn_norm),
        wo_a=wo_a, wo_b=wo_b, wo_c=wo_c,
        ffn_w_gate=ffn_w_gate.astype(BF16), ffn_w_up=ffn_w_up.astype(BF16),
        ffn_w_down=ffn_w_down.astype(BF16))

    bp = x_prompt.shape[0]
    zeros = lambda s: jnp.zeros((DEPTH, bp) + s.shape[2:], x_prompt.dtype)
    y_p, gla_p, rwkv_p, shift_p, hgrn_p = _run_path(
        x_prompt, zeros(state_gla), zeros(state_rwkv), zeros(state_rwkv_shift), zeros(state_hgrn), wt)
    y_s, gla_s, rwkv_s, shift_s, hgrn_s = _run_path(
        x_sample, state_gla, state_rwkv, state_rwkv_shift, state_hgrn, wt)
    return (y_p, y_s, gla_p, rwkv_p, shift_p, hgrn_p, gla_s, rwkv_s, shift_s, hgrn_s)
```

```python
import functools

import jax
import jax.numpy as jnp
from jax import lax
from jax.experimental import pallas as pl
from jax.experimental.pallas import tpu as pltpu

F32 = jnp.float32
BF16 = jnp.bfloat16

D_MODEL = 2048
DEPTH = 4
EPS = 1e-6

GLA_HEADS, GLA_DK, GLA_DV = 4, 96, 192
GLA_DKP, GLA_DVP = 128, 256
GLA_RANK, GLA_RANKP = 16, 128
GLA_GATE_NORMALIZER = 16.0
GLA_QW = GLA_HEADS * GLA_DKP
GLA_VW = GLA_HEADS * GLA_DVP
GLA_COLS_P = 2 * GLA_QW + 2 * GLA_VW + GLA_RANKP

RWKV_HEADS, RWKV_HD = 10, 64
RWKV_WIDTH = RWKV_HEADS * RWKV_HD
RWKV_PAIRS = RWKV_HEADS // 2
RWKV_COLS = 2176
RWKV_GN_EPS = 64e-5

HGRN_HEADS, HGRN_D = 5, 128
HGRN_WIDTH = HGRN_HEADS * HGRN_D
HGRN_COLS = 4 * HGRN_WIDTH

D_FF = 5632
LANE = 128
LOG2_E = 1.4426950408889634

VMEM_LIMIT = 56 * 1024 * 1024


def _sigmoid(x):
    return 1.0 / (1.0 + jnp.exp(-x))


def _silu(x):
    return x * _sigmoid(x)


def _log_sigmoid(x):
    return jnp.minimum(x, 0.0) - jnp.log(1.0 + jnp.exp(-jnp.abs(x)))


def _dot(a, b):
    return jnp.dot(a.astype(BF16), b.astype(BF16), preferred_element_type=F32)


def _dot_nt(a, b):
    return lax.dot_general(a.astype(BF16), b.astype(BF16), (((1,), (1,)), ((), ())),
                           preferred_element_type=F32)


def _dot_tn(a, b):
    return lax.dot_general(a.astype(BF16), b.astype(BF16), (((0,), (0,)), ((), ())),
                           preferred_element_type=F32)


def _split_bf16(x, parts):
    out = []
    r = x
    for i in range(parts):
        p = r.astype(BF16)
        out.append(p)
        if i + 1 < parts:
            r = r - p.astype(F32)
    return out


def _dot_exact_lhs(m_bf16, x, parts=3):
    acc = None
    for p in _split_bf16(x, parts):
        t = jnp.dot(m_bf16, p, preferred_element_type=F32)
        acc = t if acc is None else acc + t
    return acc


def _iota(shape, dim):
    return lax.broadcasted_iota(jnp.int32, shape, dim)


def _tril_ones(c):
    return jnp.where(_iota((c, c), 0) >= _iota((c, c), 1), 1.0, 0.0).astype(BF16)


def _rms_rows(x, g):
    ms = jnp.mean(x * x, axis=-1, keepdims=True)
    return x * lax.rsqrt(ms + EPS) * g


def _compiler_params(sem, flags=None):
    return pltpu.CompilerParams(dimension_semantics=sem, vmem_limit_bytes=VMEM_LIMIT, flags=flags)


def _norm_kernel(x_ref, g_ref, h_ref):
    h_ref[...] = _rms_rows(x_ref[...], g_ref[...]).astype(BF16)


def _norm(x, g, tm):
    n, d = x.shape
    return pl.pallas_call(
        _norm_kernel,
        grid=(n // tm,),
        in_specs=[pl.BlockSpec((tm, d), lambda i: (i, 0)), pl.BlockSpec((1, d), lambda i: (0, 0))],
        out_specs=pl.BlockSpec((tm, d), lambda i: (i, 0)),
        out_shape=jax.ShapeDtypeStruct((n, d), BF16),
        compiler_params=_compiler_params(("arbitrary",)),
    )(x, g)


def _out_proj_kernel(oa_ref, ob_ref, oc_ref, wa_ref, wb_ref, wc_ref, x_ref, g_ref, gnext_ref,
                     y_ref, hn_ref):
    half = x_ref.shape[0] // 2
    for r in (slice(0, half), slice(half, 2 * half)):
        m = (jnp.dot(oa_ref[r, :], wa_ref[...], preferred_element_type=F32)
             + jnp.dot(ob_ref[r, :], wb_ref[...], preferred_element_type=F32)
             + jnp.dot(oc_ref[r, :], wc_ref[...], preferred_element_type=F32))
        y = x_ref[r, :] + _rms_rows(m, g_ref[...])
        y_ref[r, :] = y
        hn_ref[r, :] = _rms_rows(y, gnext_ref[...]).astype(BF16)


def _out_proj(oa, ob, oc, wa, wb, wc, x, g, gnext, tm):
    n, d = x.shape
    row = lambda w: pl.BlockSpec((tm, w), lambda i: (i, 0))
    full = lambda a: pl.BlockSpec(a.shape, lambda i: (0, 0))
    return pl.pallas_call(
        _out_proj_kernel,
        grid=(n // tm,),
        in_specs=[row(oa.shape[1]), row(ob.shape[1]), row(oc.shape[1]),
                  full(wa), full(wb), full(wc), row(d), full(g), full(gnext)],
        out_specs=[row(d), row(d)],
        out_shape=[jax.ShapeDtypeStruct((n, d), F32), jax.ShapeDtypeStruct((n, d), BF16)],
        compiler_params=_compiler_params(("arbitrary",)),
    )(oa, ob, oc, wa, wb, wc, x, g, gnext)


def _ffn_kernel(h_ref, x_ref, wg_ref, wu_ref, wd_ref, gpost_ref, gnext_ref, y_ref, hn_ref, acc_scr):
    j = pl.program_id(1)

    @pl.when(j == 0)
    def _():
        acc_scr[...] = jnp.zeros_like(acc_scr)

    h = h_ref[...]
    a = jnp.dot(h, wg_ref[...], preferred_element_type=F32)
    u = jnp.dot(h, wu_ref[...], preferred_element_type=F32)
    acc_scr[...] += jnp.dot((_silu(a) * u).astype(BF16), wd_ref[...], preferred_element_type=F32)

    @pl.when(j == pl.num_programs(1) - 1)
    def _():
        y = x_ref[...] + _rms_rows(acc_scr[...], gpost_ref[...])
        y_ref[...] = y
        hn_ref[...] = _rms_rows(y, gnext_ref[...]).astype(BF16)


def _ffn(h, x, wg, wu, wd, gpost, gnext, tm, tf):
    n, d = x.shape
    dff = wg.shape[1]
    return pl.pallas_call(
        _ffn_kernel,
        grid=(n // tm, dff // tf),
        in_specs=[pl.BlockSpec((tm, d), lambda i, j: (i, 0)),
                  pl.BlockSpec((tm, d), lambda i, j: (i, 0)),
                  pl.BlockSpec((d, tf), lambda i, j: (0, j)),
                  pl.BlockSpec((d, tf), lambda i, j: (0, j)),
                  pl.BlockSpec((tf, d), lambda i, j: (j, 0)),
                  pl.BlockSpec((1, d), lambda i, j: (0, 0)),
                  pl.BlockSpec((1, d), lambda i, j: (0, 0))],
        out_specs=[pl.BlockSpec((tm, d), lambda i, j: (i, 0)),
                   pl.BlockSpec((tm, d), lambda i, j: (i, 0))],
        out_shape=[jax.ShapeDtypeStruct((n, d), F32), jax.ShapeDtypeStruct((n, d), BF16)],
        scratch_shapes=[pltpu.VMEM((tm, d), F32)],
        compiler_params=_compiler_params(("arbitrary", "arbitrary")),
    )(h, x, wg, wu, wd, gpost, gnext)


class _GlrMasks:
    def __init__(self, c):
        self.c = c
        row_a = _iota((c, c), 0)
        col_a = _iota((c, c), 1)
        self.tril = _tril_ones(c)
        self.level = {}
        h = 1
        while 2 * h <= c:
            self.level[h] = (((row_a // (2 * h)) == (col_a // (2 * h)))
                             & ((row_a // h) % 2 == 1) & ((col_a // h) % 2 == 0))
            h *= 2
        self.eye = row_a == col_a


def _small_level_factors(g):
    c = g.shape[0]
    r4 = _iota(g.shape, 0) % 4
    g_prev = pltpu.roll(g, 1, 0)
    g_next = pltpu.roll(g, c - 1, 0)
    e1 = jnp.exp2(jnp.where(r4 % 2 == 1, g, 0.0))
    e2 = jnp.exp2(jnp.where(r4 == 3, g + g_prev, jnp.where(r4 == 2, g, jnp.where(r4 == 0, g_next, 0.0))))
    return e1, e2


def _glr_chunk(q, k, v, b, e1, e2, st, mk):
    c, dk = q.shape
    att = jnp.where(mk.eye, _dot_nt(q, k), 0.0)
    for h, pair_mask in mk.level.items():
        if h == 1:
            e = e1
        elif h == 2:
            e = e2
        else:
            n = c // (2 * h)
            ref = b.reshape(n, 2 * h, dk)[:, h - 1:h, :]
            refb = jnp.broadcast_to(ref, (n, 2 * h, dk)).reshape(c, dk)
            e = jnp.exp2(-jnp.abs(b - refb))
        att = jnp.where(pair_mask, _dot_nt(q * e, k * e), att)
    o = _dot(att, v) + _dot_nt(q * jnp.exp2(b), st)
    b_end = b[c - 1:c, :]
    st_new = st * jnp.exp2(b_end) + _dot_tn(v, k * jnp.exp2(b_end - b))
    return o, st_new


def _ahead_groups(cols, slots):
    tiles = -(-cols // (2 * LANE))
    bounds = [min(cols, 2 * LANE * ((k * tiles) // slots)) for k in range(slots)] + [cols]
    return [slice(bounds[k], bounds[k + 1]) for k in range(slots)]


def _gla_kernel(hc_ref, hn_ref, w_ref, s0_ref, gup_ref, gbias_ref, gnorm_ref, o_ref, sout_ref,
                s_scr, pa_scr, pb_scr, *, c, nsub):
    ci = pl.program_id(1)
    half = max(nsub // 2, 1)
    hrows = c * half
    groups = _ahead_groups(GLA_COLS_P, GLA_HEADS * half)

    @pl.when(ci == 0)
    def _():
        s_scr[...] = s0_ref[0]

    @pl.when(jnp.logical_or(ci == 0, nsub == 1))
    def _():
        pa_scr[...] = jnp.dot(hc_ref[0:hrows, :], w_ref[...], preferred_element_type=F32)

    mk = _GlrMasks(c)
    gnorm = gnorm_ref[...]
    for i in range(nsub):
        rs = slice(i * c, (i + 1) * c)
        src = pa_scr if i < half else pb_scr
        p = src[(i % half) * c:(i % half + 1) * c, :]
        q_all = p[:, 0:GLA_QW] * (GLA_DK ** -0.5)
        k_all = p[:, GLA_QW:2 * GLA_QW]
        v_all = p[:, 2 * GLA_QW:2 * GLA_QW + GLA_VW]
        gout = p[:, 2 * GLA_QW + GLA_VW:2 * GLA_QW + 2 * GLA_VW]
        gdown = p[:, 2 * GLA_QW + 2 * GLA_VW:]
        g_all = (_log_sigmoid(_dot(gdown, gup_ref[...]) + gbias_ref[...])
                 * (LOG2_E / GLA_GATE_NORMALIZER))
        b_all = _dot_exact_lhs(mk.tril, g_all)
        e1_all, e2_all = _small_level_factors(g_all)
        for h in range(GLA_HEADS):
            ks = slice(h * GLA_DKP, (h + 1) * GLA_DKP)
            vs = slice(h * GLA_DVP, (h + 1) * GLA_DVP)
            o, st = _glr_chunk(q_all[:, ks], k_all[:, ks], v_all[:, vs], b_all[:, ks],
                               e1_all[:, ks], e2_all[:, ks], s_scr[h], mk)
            s_scr[h] = st
            ms = jnp.sum(o * o, axis=-1, keepdims=True) * (1.0 / GLA_DV)
            o = o * lax.rsqrt(ms + EPS) * gnorm * _silu(gout[:, vs])
            o_ref[rs, vs] = o.astype(BF16)
            if nsub > 1:
                g = groups[(i % half) * GLA_HEADS + h]
                ahead, dst = (hc_ref[hrows:, :], pb_scr) if i < half else (hn_ref[0:hrows, :], pa_scr)
                dst[:, g] = jnp.dot(ahead, w_ref[:, g], preferred_element_type=F32)

    @pl.when(ci == pl.num_programs(1) - 1)
    def _():
        sout_ref[0] = s_scr[...]


def _gla_mixer(hx, w, s0t, gup, gbias, gnorm, b, t, c):
    nsub = 4 if t % (4 * c) == 0 else 1
    rows = c * nsub
    hrows = c * max(nsub // 2, 1)
    nc = t // rows
    nblk = b * nc
    kern = functools.partial(_gla_kernel, c=c, nsub=nsub)
    full = lambda a: pl.BlockSpec(a.shape, lambda i, j: (0,) * a.ndim)
    st_spec = pl.BlockSpec((1, GLA_HEADS, GLA_DVP, GLA_DKP), lambda i, j: (i, 0, 0, 0))
    d = hx.shape[1]
    return pl.pallas_call(
        kern,
        grid=(b, nc),
        in_specs=[pl.BlockSpec((rows, d), lambda i, j: (i * nc + j, 0)),
                  pl.BlockSpec((rows, d), lambda i, j: (jnp.minimum(i * nc + j + 1, nblk - 1), 0)),
                  pl.BlockSpec(w.shape, lambda i, j: (0, 0), pipeline_mode=pl.Buffered(1)),
                  st_spec, full(gup), full(gbias), full(gnorm)],
        out_specs=[pl.BlockSpec((rows, GLA_VW), lambda i, j: (i * nc + j, 0)), st_spec],
        out_shape=[jax.ShapeDtypeStruct((b * t, GLA_VW), BF16),
                   jax.ShapeDtypeStruct((b, GLA_HEADS, GLA_DVP, GLA_DKP), F32)],
        scratch_shapes=[pltpu.VMEM((GLA_HEADS, GLA_DVP, GLA_DKP), F32),
                        pltpu.VMEM((hrows, GLA_COLS_P), F32), pltpu.VMEM((hrows, GLA_COLS_P), F32)],
        compiler_params=_compiler_params(("arbitrary", "arbitrary")),
    )(hx, hx, w, s0t, gup, gbias, gnorm)


def _hgrn_kernel(hc_ref, hn_ref, w_ref, s0_ref, lbl_ref, hnorm_ref, o_ref, sout_ref,
                 s_scr, pa_scr, pb_scr, *, c, nsub, layer):
    ci = pl.program_id(1)
    half = max(nsub // 2, 1)
    hrows = c * half
    gw = HGRN_COLS // (HGRN_HEADS * half)

    @pl.when(ci == 0)
    def _():
        s_scr[...] = s0_ref[0]

    @pl.when(jnp.logical_or(ci == 0, nsub == 1))
    def _():
        pa_scr[...] = jnp.dot(hc_ref[0:hrows, :], w_ref[...], preferred_element_type=F32)

    lg = lbl_ref[...]
    e = jnp.exp(lg - jnp.max(lg, axis=0, keepdims=True))
    prob = e / jnp.sum(e, axis=0, keepdims=True)
    lb = jnp.zeros((1, HGRN_WIDTH), F32)
    for i in range(1, layer + 1):
        lb = lb + prob[i:i + 1, :]

    mk = _GlrMasks(c)
    hnorm = hnorm_ref[...]

    for i in range(nsub):
        rs = slice(i * c, (i + 1) * c)
        src = pa_scr if i < half else pb_scr
        p = src[(i % half) * c:(i % half + 1) * c, :]
        hq = p[:, 0:HGRN_WIDTH]
        hf = p[:, HGRN_WIDTH:2 * HGRN_WIDTH]
        hi = p[:, 2 * HGRN_WIDTH:3 * HGRN_WIDTH]
        hg = p[:, 3 * HGRN_WIDTH:]
        q_all = _silu(hq)
        f_all = lb + (1.0 - lb) * _sigmoid(hf)
        k_all = 1.0 - f_all
        g_all = jnp.log2(f_all)
        b_all = _dot_exact_lhs(mk.tril, g_all)
        e1_all, e2_all = _small_level_factors(g_all)
        for h in range(HGRN_HEADS):
            s = slice(h * HGRN_D, (h + 1) * HGRN_D)
            o, st = _glr_chunk(q_all[:, s], k_all[:, s], hi[:, s], b_all[:, s],
                               e1_all[:, s], e2_all[:, s], s_scr[h], mk)
            s_scr[h] = st
            o = _rms_rows(o, hnorm) * _silu(hg[:, s])
            o_ref[rs, s] = o.astype(BF16)
            if nsub > 1:
                k = (i % half) * HGRN_HEADS + h
                g = slice(k * gw, (k + 1) * gw)
                ahead, dst = (hc_ref[hrows:, :], pb_scr) if i < half else (hn_ref[0:hrows, :], pa_scr)
                dst[:, g] = jnp.dot(ahead, w_ref[:, g], preferred_element_type=F32)

    @pl.when(ci == pl.num_programs(1) - 1)
    def _():
        sout_ref[0] = s_scr[...]


def _hgrn_mixer(hx, w, s0t, lb_logits, hnorm, b, t, c, layer):
    nsub = 4 if t % (4 * c) == 0 else 1
    rows = c * nsub
    hrows = c * max(nsub // 2, 1)
    nc = t // rows
    nblk = b * nc
    kern = functools.partial(_hgrn_kernel, c=c, nsub=nsub, layer=layer)
    full = lambda a: pl.BlockSpec(a.shape, lambda i, j: (0,) * a.ndim)
    st_spec = pl.BlockSpec((1, HGRN_HEADS, HGRN_D, HGRN_D), lambda i, j: (i, 0, 0, 0))
    d = hx.shape[1]
    return pl.pallas_call(
        kern,
        grid=(b, nc),
        in_specs=[pl.BlockSpec((rows, d), lambda i, j: (i * nc + j, 0)),
                  pl.BlockSpec((rows, d), lambda i, j: (jnp.minimum(i * nc + j + 1, nblk - 1), 0)),
                  pl.BlockSpec(w.shape, lambda i, j: (0, 0), pipeline_mode=pl.Buffered(1)),
                  st_spec, full(lb_logits), full(hnorm)],
        out_specs=[pl.BlockSpec((rows, HGRN_WIDTH), lambda i, j: (i * nc + j, 0)), st_spec],
        out_shape=[jax.ShapeDtypeStruct((b * t, HGRN_WIDTH), BF16),
                   jax.ShapeDtypeStruct((b, HGRN_HEADS, HGRN_D, HGRN_D), F32)],
        scratch_shapes=[pltpu.VMEM((HGRN_HEADS, HGRN_D, HGRN_D), F32),
                        pltpu.VMEM((hrows, HGRN_COLS), F32), pltpu.VMEM((hrows, HGRN_COLS), F32)],
        compiler_params=_compiler_params(("arbitrary", "arbitrary")),
    )(hx, hx, w, s0t, lb_logits, hnorm)


def _rwkv_kernel(hc_ref, hn_ref, w_ref, shift0_ref, s0_ref, mu_ref, w0_ref, wup_ref, a0_ref,
                 aup_ref, gup_ref, kk_ref, ka_ref, rk_ref, lnw_ref, lnb_ref,
                 o_ref, shift_ref, sout_ref, s_scr, carry_scr, pa_scr, pb_scr, *, c, nsub):
    ci = pl.program_id(1)
    w = RWKV_WIDTH
    rows = c * nsub
    grp = min(nsub, 2)
    grows = c * grp

    @pl.when(ci == 0)
    def _():
        s_scr[...] = s0_ref[0]
        carry_scr[...] = shift0_ref[0]

    @pl.when(jnp.logical_or(ci == 0, nsub == 1))
    def _():
        pa_scr[...] = jnp.dot(hc_ref[0:grows, :], w_ref[...], preferred_element_type=F32)

    first_prev = carry_scr[...]
    grp_count = rows // grows
    col_tiles = _ahead_groups(RWKV_COLS, -(-RWKV_COLS // (2 * LANE)))

    c2 = 2 * c
    lane = _iota((c, LANE), 1)
    lo_half = lane < RWKV_HD
    ri = _iota((c2, c2), 0)
    cj = _iota((c2, c2), 1)
    strict = (ri % c) > (cj % c)
    incl = (ri % c) >= (cj % c)
    eye = jnp.where(ri == cj, 1.0, 0.0)
    inv_levels = []
    blk = 1
    while blk < c:
        inv_levels.append(((ri // (2 * blk)) == (cj // (2 * blk)))
                          & ((ri // blk) % 2 == 1) & ((cj // blk) % 2 == 0))
        blk *= 2
    tr, tc = _iota((grows, grows), 0), _iota((grows, grows), 1)
    tril = jnp.where((tr >= tc) & (tr // c == tc // c), 1.0, 0.0).astype(BF16)
    row = _iota((grows, RWKV_COLS), 0)
    seg = jnp.where((_iota((LANE, LANE), 0) // RWKV_HD) == (_iota((LANE, LANE), 1) // RWKV_HD),
                    1.0, 0.0).astype(BF16)

    def stack(x):
        return jnp.concatenate([jnp.where(lo_half, x, 0.0), jnp.where(lo_half, 0.0, x)], axis=0)

    def head_sum(x):
        return jnp.concatenate([_dot(x[:, j * LANE:(j + 1) * LANE], seg)
                                for j in range(RWKV_PAIRS)], axis=1)

    def prepare(gi):
        g0 = gi * grows
        src = pb_scr if gi else pa_scr
        p = src[...]
        before = pa_scr[grows - 1:grows, :] if gi else first_prev
        if nsub == 1:
            todo = iter(())
        elif gi == 0:
            todo = iter([(hc_ref, slice(grows, 2 * grows), pb_scr, g) for g in col_tiles])
        else:
            todo = iter([(hn_ref, slice(0, grows), pa_scr, g) for g in col_tiles])

        def project_one():
            t = next(todo, None)
            if t is not None:
                h_ref, h_rows, dst, g = t
                dst[:, g] = jnp.dot(h_ref[h_rows, :], w_ref[:, g], preferred_element_type=F32)

        prev = jnp.where(row == 0, before, pltpu.roll(p, 1, 0))
        pm = p + (prev - p) * mu_ref[...]
        project_one()
        r_all = pm[:, 0:w]
        k_all = pm[:, w:2 * w]
        v_all = pm[:, 2 * w:3 * w]
        wa = pm[:, 3 * w:3 * w + LANE]
        gd = pm[:, 3 * w + LANE:]
        lw_all = -jnp.exp(_log_sigmoid(w0_ref[...] + _dot(jnp.tanh(wa), wup_ref[...])) - 0.5)
        a_all = _sigmoid(a0_ref[...] + _dot(wa, aup_ref[...]))
        project_one()
        gate_all = _dot(_sigmoid(gd), gup_ref[...])
        kk_all = k_all * kk_ref[...]
        kk_all = kk_all * lax.rsqrt(jnp.maximum(head_sum(kk_all * kk_all), 1e-24))
        project_one()
        k_all = k_all * (1.0 + (a_all - 1.0) * ka_ref[...])
        beta_all = kk_all * a_all
        bonus_all = head_sum(r_all * k_all * rk_ref[...]) * v_all
        project_one()
        cin_all = _dot_exact_lhs(tril, lw_all, parts=2)
        ginv = jnp.exp(-cin_all)
        kap_all = kk_all * jnp.exp(cin_all - lw_all)
        project_one()
        rt_all = r_all * jnp.exp(cin_all)
        kt_all = k_all * ginv
        bt_all = beta_all * ginv
        out = []
        for i in range(grp):
            rs = slice(i * c, (i + 1) * c)
            cend = cin_all[(i + 1) * c - 1:(i + 1) * c, :]
            gend = jnp.exp(cend - cin_all[rs])
            kend, bend, dec = k_all[rs] * gend, beta_all[rs] * gend, jnp.exp(cend)
            for j in range(RWKV_PAIRS):
                s = slice(j * LANE, (j + 1) * LANE)
                out.append(dict(
                    s=s, rs=slice(g0 + i * c, g0 + (i + 1) * c),
                    kap_s=stack(kap_all[rs, s]), r_s=stack(rt_all[rs, s]), k_s=stack(kt_all[rs, s]),
                    b_s=stack(bt_all[rs, s]), v_s=stack(v_all[rs, s]), kend_s=stack(kend[:, s]),
                    bend_s=stack(bend[:, s]), dec=dec[:, s], bonus=bonus_all[rs, s],
                    gate=gate_all[rs, s]))
                if j % 2 == 0:
                    project_one()
        for _ in col_tiles:
            project_one()
        if gi == grp_count - 1:
            last = src[grows - 1:grows, :]
            carry_scr[...] = last
            shift_ref[0] = last
        return out

    def dot_nt2(lhs, r1, r2):
        if r1.shape[0] % LANE:
            return _dot_nt(lhs, r1), _dot_nt(lhs, r2)
        both = _dot_nt(lhs, jnp.concatenate([r1, r2], axis=0))
        return both[:, :r1.shape[0]], both[:, r1.shape[0]:]

    def dot2(lhs, r1, r2):
        both = _dot(lhs, jnp.concatenate([r1, r2], axis=1))
        return both[:, :r1.shape[1]], both[:, r1.shape[1]:]

    pairs = []
    for gi in range(grp_count):
        group = prepare(gi)
        for pr in group:
            a1, a2 = dot_nt2(pr['kap_s'], pr['k_s'], pr['b_s'])
            pr['a1'], pr['a2'] = jnp.where(strict, a1, 0.0), jnp.where(strict, a2, 0.0)
        for pr in group:
            a3, a4 = dot_nt2(pr['r_s'], pr['k_s'], pr['b_s'])
            pr['a3'], pr['a4'] = jnp.where(incl, a3, 0.0), jnp.where(incl, a4, 0.0)
        for pr in group:
            pr['a1v'] = _dot(pr['a1'], pr['v_s'])
            pr['a3v'] = _dot(pr['a3'], pr['v_s'])
            pr['kv'] = _dot_tn(pr['v_s'], pr['kend_s'])
        pairs += group
    if c2 % LANE == 0 and len(pairs) % 2 == 0:
        left = _iota((c2, 2 * c2), 1) < c2
        tile2 = lambda x: jnp.concatenate([x, x], axis=1)

        def block_diag(x, keep=None):
            top = left if keep is None else keep & left
            bot = ~left if keep is None else keep & ~left
            x = x.astype(BF16)
            return jnp.concatenate([jnp.where(top, x, 0.0), jnp.where(bot, x, 0.0)], axis=0)

        a2w = [jnp.concatenate([pa['a2'], pb['a2']], axis=1) for pa, pb in zip(pairs[0::2], pairs[1::2])]
        tw = [tile2(eye) - jnp.where(tile2(inv_levels[0]), a, 0.0) for a in a2w]
        for m in inv_levels[1:]:
            mw = tile2(m)
            half = [_dot(t, block_diag(a, mw)) for t, a in zip(tw, a2w)]
            tw = [t - _dot(hf, block_diag(t)) for t, hf in zip(tw, half)]
        tinv = [x for t in tw for x in (t[:, :c2], t[:, c2:])]
    else:
        tinv = [eye - jnp.where(inv_levels[0], pr['a2'], 0.0) for pr in pairs]
        for m in inv_levels[1:]:
            half = [_dot(t, jnp.where(m, pr['a2'], 0.0)) for t, pr in zip(tinv, pairs)]
            tinv = [t - _dot(hf, t) for t, hf in zip(tinv, half)]
    for pr, tj in zip(pairs, tinv):
        pr['tk'], pr['tv'] = dot2(tj, pr['kap_s'], pr['a1v'])
    for pr in pairs:
        a4tk, a4tv = dot2(pr['a4'], pr['tk'], pr['tv'])
        pr['reff'] = pr['r_s'] - a4tk
        pr['oc'] = pr['a3v'] - a4tv
        pr['m'] = _dot_tn(pr['tk'], pr['bend_s'])
        pr['q'] = pr['kv'] - _dot_tn(pr['tv'], pr['bend_s'])
    sts = [s_scr[j] for j in range(RWKV_PAIRS)]
    os_ = []
    for i in range(nsub):
        sub = list(zip(pairs[i * RWKV_PAIRS:(i + 1) * RWKV_PAIRS], sts))
        o2s = [_dot_nt(pr['reff'], st) + pr['oc'] for pr, st in sub]
        sts = [st * pr['dec'] - _dot(st, pr['m']) + pr['q'] for pr, st in sub]
        os_ += [o2[0:c] + o2[c:c2] for o2 in o2s]
    for j, st in enumerate(sts):
        s_scr[j] = st
    ds = [o - _dot(o, seg) * (1.0 / RWKV_HD) for o in os_]
    vars_ = [_dot(d * d, seg) * (1.0 / RWKV_HD) for d in ds]
    for pr, d, var in zip(pairs, ds, vars_):
        s, rs = pr['s'], pr['rs']
        on = d * lax.rsqrt(var + RWKV_GN_EPS) * lnw_ref[:, s] + lnb_ref[:, s]
        o_ref[rs, s] = ((on + pr['bonus']) * pr['gate']).astype(BF16)

    @pl.when(ci == pl.num_programs(1) - 1)
    def _():
        sout_ref[0] = s_scr[...]


def _rwkv_mixer(hx, w, shift0, s0, wts, b, t, c, nsub):
    rows = c * nsub
    grows = c * min(nsub, 2)
    nc = t // rows
    nblk = b * nc
    d = hx.shape[1]
    kern = functools.partial(_rwkv_kernel, c=c, nsub=nsub)
    full = lambda a: pl.BlockSpec(a.shape, lambda i, j: (0,) * a.ndim)
    st_spec = pl.BlockSpec((1, RWKV_PAIRS, LANE, LANE), lambda i, j: (i, 0, 0, 0))
    sh_spec = pl.BlockSpec((1, 1, RWKV_COLS), lambda i, j: (i, 0, 0))
    return pl.pallas_call(
        kern,
        grid=(b, nc),
        in_specs=[pl.BlockSpec((rows, d), lambda i, j: (i * nc + j, 0)),
                  pl.BlockSpec((rows, d), lambda i, j: (jnp.minimum(i * nc + j + 1, nblk - 1), 0)),
                  pl.BlockSpec(w.shape, lambda i, j: (0, 0), pipeline_mode=pl.Buffered(1)),
                  sh_spec, st_spec] + [full(a) for a in wts],
        out_specs=[pl.BlockSpec((rows, RWKV_WIDTH), lambda i, j: (i * nc + j, 0)), sh_spec, st_spec],
        out_shape=[jax.ShapeDtypeStruct((b * t, RWKV_WIDTH), BF16),
                   jax.ShapeDtypeStruct((b, 1, RWKV_COLS), F32),
                   jax.ShapeDtypeStruct((b, RWKV_PAIRS, LANE, LANE), F32)],
        scratch_shapes=[pltpu.VMEM((RWKV_PAIRS, LANE, LANE), F32), pltpu.VMEM((1, RWKV_COLS), F32),
                        pltpu.VMEM((grows, RWKV_COLS), F32), pltpu.VMEM((grows, RWKV_COLS), F32)],
        compiler_params=_compiler_params(("arbitrary", "arbitrary")),
    )(hx, hx, w, shift0, s0, *wts)


def _rwkv_cols_in(a):
    return jnp.concatenate([a[..., 0:640], a[..., 704:1984], a[..., 640:704], a[..., 1984:2176]], axis=-1)


def _rwkv_cols_out(a):
    return jnp.concatenate([a[..., 0:640], a[..., 1920:1984], a[..., 640:1920], a[..., 1984:2176]], axis=-1)


def _pad_heads(w, heads, d, dp):
    lead = w.shape[:-1]
    w = w.reshape(lead + (heads, d))
    w = jnp.pad(w, [(0, 0)] * len(lead) + [(0, 0), (0, dp - d)])
    return w.reshape(lead + (heads * dp,))


def _prep_weights(w_in, gla_gate_up, gla_gate_bias, gla_norm, rwkv_mu, rwkv_w_up, rwkv_a_up,
                  rwkv_r_k, w_out):
    wb = w_in.astype(BF16)
    zeros = lambda n: jnp.zeros(wb.shape[:-1] + (n,), BF16)

    def padded_heads(start, d, dp):
        return [piece for h in range(GLA_HEADS)
                for piece in (wb[..., start + h * d:start + (h + 1) * d], zeros(dp - d))]

    w_gla = jnp.concatenate(
        padded_heads(0, GLA_DK, GLA_DKP) + padded_heads(384, GLA_DK, GLA_DKP)
        + padded_heads(768, GLA_DV, GLA_DVP) + padded_heads(1552, GLA_DV, GLA_DVP)
        + [wb[..., 1536:1552], zeros(GLA_RANKP - GLA_RANK)], axis=-1)
    w_rwkv = _rwkv_cols_in(wb[..., 2320:2320 + RWKV_COLS])
    w_hgrn = wb[..., 2320 + RWKV_COLS:]
    gup = jnp.pad(_pad_heads(gla_gate_up, GLA_HEADS, GLA_DK, GLA_DKP),
                  ((0, 0), (0, GLA_RANKP - GLA_RANK), (0, 0))).astype(BF16)
    gbias = _pad_heads(gla_gate_bias, GLA_HEADS, GLA_DK, GLA_DKP)[:, None, :]
    gnorm = jnp.pad(gla_norm, ((0, 0), (0, GLA_DVP - GLA_DV)))[:, None, :]
    mu = _rwkv_cols_in(rwkv_mu)[:, None, :]
    wup = jnp.pad(rwkv_w_up, ((0, 0), (0, 64), (0, 0))).astype(BF16)
    aup = jnp.pad(rwkv_a_up, ((0, 0), (64, 0), (0, 0))).astype(BF16)
    rk = rwkv_r_k.reshape(DEPTH, 1, RWKV_WIDTH)
    wo_a = w_out[:, :768].reshape(DEPTH, GLA_HEADS, GLA_DV, D_MODEL)
    wo_a = jnp.pad(wo_a, ((0, 0), (0, 0), (0, GLA_DVP - GLA_DV), (0, 0)))
    wo_a = wo_a.reshape(DEPTH, GLA_VW, D_MODEL).astype(BF16)
    wo_b = w_out[:, 768:1408].astype(BF16)
    wo_c = w_out[:, 1408:].astype(BF16)
    return w_gla, w_rwkv, w_hgrn, gup, gbias, gnorm, mu, wup, aup, rk, wo_a, wo_b, wo_c


def _gla_state_in(s):
    s = jnp.swapaxes(s, -1, -2)
    return jnp.pad(s, [(0, 0)] * 3 + [(0, GLA_DVP - GLA_DV), (0, GLA_DKP - GLA_DK)])


def _gla_state_out(s):
    return jnp.swapaxes(s[..., :GLA_DV, :GLA_DK], -1, -2)


def _rwkv_state_in(s):
    d, b = s.shape[:2]
    s = s.reshape(d, b, RWKV_PAIRS, 2, RWKV_HD, RWKV_HD)
    out = jnp.einsum('dbjhvk,hg->dbjhvgk', s, jnp.eye(2, dtype=s.dtype))
    return out.reshape(d, b, RWKV_PAIRS, LANE, LANE)


def _rwkv_state_out(s):
    d, b = s.shape[:2]
    s = s.reshape(d, b, RWKV_PAIRS, 2, RWKV_HD, 2, RWKV_HD)
    out = jnp.stack([s[:, :, :, 0, :, 0, :], s[:, :, :, 1, :, 1, :]], axis=3)
    return out.reshape(d, b, RWKV_HEADS, RWKV_HD, RWKV_HD)


def _row_tile(n, cap):
    t = cap
    while n % t:
        t //= 2
    return t


def _trunk(x, s_gla, s_rwkv, s_shift, s_hgrn, wt, b, t):
    n = b * t
    tm = _row_tile(n, 512)
    c_glr = min(128, t)
    c_rwkv = min(64, t)
    n_rwkv = 4 if t % (4 * c_rwkv) == 0 else 1
    new_gla, new_rwkv, new_shift, new_hgrn = [], [], [], []
    h = _norm(x, wt['norm_mix_pre'][0], tm)
    for l in range(DEPTH):
        oa, g1 = _gla_mixer(h, wt['w_gla'][l], s_gla[l], wt['gup'][l], wt['gbias'][l],
                            wt['gnorm'][l], b, t, c_glr)
        rw = [wt[k][l] for k in ('mu', 'w0', 'wup', 'a0', 'aup', 'g_up', 'k_k', 'k_a', 'rk',
                                 'ln_w', 'ln_b')]
        ob, sh1, r1 = _rwkv_mixer(h, wt['w_rwkv'][l], s_shift[l], s_rwkv[l], rw, b, t, c_rwkv, n_rwkv)
        oc, h1 = _hgrn_mixer(h, wt['w_hgrn'][l], s_hgrn[l], wt['lb_logits'], wt['hnorm'][l],
                             b, t, c_glr, l)
        x, h = _out_proj(oa, ob, oc, wt['wo_a'][l], wt['wo_b'][l], wt['wo_c'][l], x,
                         wt['norm_mix_post'][l], wt['norm_ffn_pre'][l], tm)
        x, h = _ffn(h, x, wt['ffn_w_gate'][l], wt['ffn_w_up'][l], wt['ffn_w_down'][l],
                    wt['norm_ffn_post'][l], wt['norm_mix_pre'][(l + 1) % DEPTH], tm, 512)
        new_gla.append(g1)
        new_rwkv.append(r1)
        new_shift.append(sh1)
        new_hgrn.append(h1)
    return x, jnp.stack(new_gla), jnp.stack(new_rwkv), jnp.stack(new_shift), jnp.stack(new_hgrn)


def _run_path(x, state_gla, state_rwkv, state_shift, state_hgrn, wt):
    b, t, d = x.shape
    y, g, r, sh, h = _trunk(
        x.reshape(b * t, d), _gla_state_in(state_gla), _rwkv_state_in(state_rwkv),
        _rwkv_cols_in(state_shift)[:, :, None, :], jnp.swapaxes(state_hgrn, -1, -2), wt, b, t)
    return (y.reshape(b, t, d), _gla_state_out(g), _rwkv_state_out(r),
            _rwkv_cols_out(sh[:, :, 0, :]), jnp.swapaxes(h, -1, -2))


def kernel(x_prompt, x_sample, state_gla, state_rwkv, state_rwkv_shift, state_hgrn, norm_mix_pre, norm_mix_post, norm_ffn_pre, norm_ffn_post, w_in, gla_gate_up, gla_gate_bias, gla_norm, rwkv_mu, rwkv_w0, rwkv_w_up, rwkv_a0, rwkv_a_up, rwkv_g_up, rwkv_k_k, rwkv_k_a, rwkv_r_k, rwkv_ln_w, rwkv_ln_b, hgrn_lb_logits, hgrn_norm, w_out, ffn_w_gate, ffn_w_up, ffn_w_down):
    (w_gla, w_rwkv, w_hgrn, gup, gbias, gnorm, mu, wup, aup, rk, wo_a, wo_b, wo_c) = _prep_weights(
        w_in, gla_gate_up, gla_gate_bias, gla_norm, rwkv_mu, rwkv_w_up, rwkv_a_up, rwkv_r_k, w_out)
    vec = lambda a: a[:, None, :]
    wt = dict(
        norm_mix_pre=vec(norm_mix_pre), norm_mix_post=vec(norm_mix_post),
        norm_ffn_pre=vec(norm_ffn_pre), norm_ffn_post=vec(norm_ffn_post),
        w_gla=w_gla, w_rwkv=w_rwkv, w_hgrn=w_hgrn, gup=gup, gbias=gbias, gnorm=gnorm,
        mu=mu, w0=vec(rwkv_w0), wup=wup, a0=vec(rwkv_a0), aup=aup, g_up=rwkv_g_up.astype(BF16),
        k_k=vec(rwkv_k_k), k_a=vec(rwkv_k_a), rk=rk, ln_w=vec(rwkv_ln_w), ln_b=vec(rwkv_ln_b),
        lb_logits=hgrn_lb_logits, hnorm=vec(hgrn_norm),
        wo_a=wo_a, wo_b=wo_b, wo_c=wo_c,
        ffn_w_gate=ffn_w_gate.astype(BF16), ffn_w_up=ffn_w_up.astype(BF16),
        ffn_w_down=ffn_w_down.astype(BF16))

    bp = x_prompt.shape[0]
    zeros = lambda s: jnp.zeros((DEPTH, bp) + s.shape[2:], x_prompt.dtype)
    y_p, gla_p, rwkv_p, shift_p, hgrn_p = _run_path(
        x_prompt, zeros(state_gla), zeros(state_rwkv), zeros(state_rwkv_shift), zeros(state_hgrn), wt)
    y_s, gla_s, rwkv_s, shift_s, hgrn_s = _run_path(
        x_sample, state_gla, state_rwkv, state_rwkv_shift, state_hgrn, wt)
    return (y_p, y_s, gla_p, rwkv_p, shift_p, hgrn_p, gla_s, rwkv_s, shift_s, hgrn_s)
```

```python
import functools

import jax
import jax.numpy as jnp
from jax import lax
from jax.experimental import pallas as pl
from jax.experimental.pallas import tpu as pltpu

F32 = jnp.float32
BF16 = jnp.bfloat16

D_MODEL = 2048
DEPTH = 4
EPS = 1e-6

GLA_HEADS, GLA_DK, GLA_DV = 4, 96, 192
GLA_DKP, GLA_DVP = 128, 256
GLA_RANK, GLA_RANKP = 16, 128
GLA_GATE_NORMALIZER = 16.0
GLA_QW = GLA_HEADS * GLA_DKP
GLA_VW = GLA_HEADS * GLA_DVP
GLA_COLS_P = 2 * GLA_QW + 2 * GLA_VW + GLA_RANKP

RWKV_HEADS, RWKV_HD = 10, 64
RWKV_WIDTH = RWKV_HEADS * RWKV_HD
RWKV_PAIRS = RWKV_HEADS // 2
RWKV_COLS = 2176
RWKV_GN_EPS = 64e-5

HGRN_HEADS, HGRN_D = 5, 128
HGRN_WIDTH = HGRN_HEADS * HGRN_D
HGRN_COLS = 4 * HGRN_WIDTH

D_FF = 5632
FFN_TF = 512
LANE = 128
LOG2_E = 1.4426950408889634

VMEM_LIMIT = 56 * 1024 * 1024


def _sigmoid(x):
    return 1.0 / (1.0 + jnp.exp(-x))


def _silu(x):
    return x * _sigmoid(x)


def _log_sigmoid(x):
    return jnp.minimum(x, 0.0) - jnp.log(1.0 + jnp.exp(-jnp.abs(x)))


def _dot(a, b):
    return jnp.dot(a.astype(BF16), b.astype(BF16), preferred_element_type=F32)


def _dot_nt(a, b):
    return lax.dot_general(a.astype(BF16), b.astype(BF16), (((1,), (1,)), ((), ())),
                           preferred_element_type=F32)


def _dot_tn(a, b):
    return lax.dot_general(a.astype(BF16), b.astype(BF16), (((0,), (0,)), ((), ())),
                           preferred_element_type=F32)


def _split_bf16(x, parts):
    out = []
    r = x
    for i in range(parts):
        p = r.astype(BF16)
        out.append(p)
        if i + 1 < parts:
            r = r - p.astype(F32)
    return out


def _dot_exact_lhs(m_bf16, x, parts=3):
    acc = None
    for p in _split_bf16(x, parts):
        t = jnp.dot(m_bf16, p, preferred_element_type=F32)
        acc = t if acc is None else acc + t
    return acc


def _iota(shape, dim):
    return lax.broadcasted_iota(jnp.int32, shape, dim)


def _tril_ones(c):
    return jnp.where(_iota((c, c), 0) >= _iota((c, c), 1), 1.0, 0.0).astype(BF16)


def _rms_rows(x, g):
    ms = jnp.mean(x * x, axis=-1, keepdims=True)
    return x * lax.rsqrt(ms + EPS) * g


def _compiler_params(sem, flags=None):
    return pltpu.CompilerParams(dimension_semantics=sem, vmem_limit_bytes=VMEM_LIMIT, flags=flags)


def _norm_kernel(x_ref, g_ref, h_ref):
    h_ref[...] = _rms_rows(x_ref[...], g_ref[...]).astype(BF16)


def _norm(x, g, tm):
    n, d = x.shape
    return pl.pallas_call(
        _norm_kernel,
        grid=(n // tm,),
        in_specs=[pl.BlockSpec((tm, d), lambda i: (i, 0)), pl.BlockSpec((1, d), lambda i: (0, 0))],
        out_specs=pl.BlockSpec((tm, d), lambda i: (i, 0)),
        out_shape=jax.ShapeDtypeStruct((n, d), BF16),
        compiler_params=_compiler_params(("arbitrary",)),
    )(x, g)


def _out_proj_kernel(oa_ref, ob_ref, oc_ref, wa_ref, wb_ref, wc_ref, x_ref, g_ref, gnext_ref,
                     y_ref, hn_ref):
    half = x_ref.shape[0] // 2
    for r in (slice(0, half), slice(half, 2 * half)):
        m = (jnp.dot(oa_ref[r, :], wa_ref[...], preferred_element_type=F32)
             + jnp.dot(ob_ref[r, :], wb_ref[...], preferred_element_type=F32)
             + jnp.dot(oc_ref[r, :], wc_ref[...], preferred_element_type=F32))
        y = x_ref[r, :] + _rms_rows(m, g_ref[...])
        y_ref[r, :] = y
        hn_ref[r, :] = _rms_rows(y, gnext_ref[...]).astype(BF16)


def _out_proj(oa, ob, oc, wa, wb, wc, x, g, gnext, tm):
    n, d = x.shape
    row = lambda w: pl.BlockSpec((tm, w), lambda i: (i, 0))
    full = lambda a: pl.BlockSpec(a.shape, lambda i: (0, 0))
    return pl.pallas_call(
        _out_proj_kernel,
        grid=(n // tm,),
        in_specs=[row(oa.shape[1]), row(ob.shape[1]), row(oc.shape[1]),
                  full(wa), full(wb), full(wc), row(d), full(g), full(gnext)],
        out_specs=[row(d), row(d)],
        out_shape=[jax.ShapeDtypeStruct((n, d), F32), jax.ShapeDtypeStruct((n, d), BF16)],
        compiler_params=_compiler_params(("arbitrary",)),
    )(oa, ob, oc, wa, wb, wc, x, g, gnext)


def _ffn_kernel(h_ref, x_ref, wg_ref, wu_ref, wd_ref, gpost_ref, gnext_ref, y_ref, hn_ref, acc_scr):
    j = pl.program_id(1)

    @pl.when(j == 0)
    def _():
        acc_scr[...] = jnp.zeros_like(acc_scr)

    def partial_sum(rows):
        h = h_ref[rows, :]
        a = jnp.dot(h, wg_ref[...], preferred_element_type=F32)
        u = jnp.dot(h, wu_ref[...], preferred_element_type=F32)
        return jnp.dot((_silu(a) * u).astype(BF16), wd_ref[...], preferred_element_type=F32)

    last = pl.num_programs(1) - 1

    @pl.when(j < last)
    def _():
        acc_scr[...] += partial_sum(slice(None))

    @pl.when(j == last)
    def _():
        half = x_ref.shape[0] // 2
        for r in (slice(0, half), slice(half, 2 * half)):
            y = x_ref[r, :] + _rms_rows(acc_scr[r, :] + partial_sum(r), gpost_ref[...])
            y_ref[r, :] = y
            hn_ref[r, :] = _rms_rows(y, gnext_ref[...]).astype(BF16)


def _ffn_col_tiles(w):
    lead, (d, dff) = w.shape[:-2], w.shape[-2:]
    w = w.astype(BF16).reshape(lead + (d, dff // FFN_TF, FFN_TF))
    return jnp.swapaxes(w, -2, -3)


def _ffn(h, x, wg, wu, wd, gpost, gnext, tm):
    n, d = x.shape
    nj, _, tf = wg.shape
    return pl.pallas_call(
        _ffn_kernel,
        grid=(n // tm, nj),
        in_specs=[pl.BlockSpec((tm, d), lambda i, j: (i, 0)),
                  pl.BlockSpec((tm, d), lambda i, j: (i, 0)),
                  pl.BlockSpec((None, d, tf), lambda i, j: (j, 0, 0)),
                  pl.BlockSpec((None, d, tf), lambda i, j: (j, 0, 0)),
                  pl.BlockSpec((tf, d), lambda i, j: (j, 0)),
                  pl.BlockSpec((1, d), lambda i, j: (0, 0)),
                  pl.BlockSpec((1, d), lambda i, j: (0, 0))],
        out_specs=[pl.BlockSpec((tm, d), lambda i, j: (i, 0)),
                   pl.BlockSpec((tm, d), lambda i, j: (i, 0))],
        out_shape=[jax.ShapeDtypeStruct((n, d), F32), jax.ShapeDtypeStruct((n, d), BF16)],
        scratch_shapes=[pltpu.VMEM((tm, d), F32)],
        compiler_params=_compiler_params(("arbitrary", "arbitrary")),
    )(h, x, wg, wu, wd, gpost, gnext)


class _GlrMasks:
    def __init__(self, c):
        self.c = c
        row_a = _iota((c, c), 0)
        col_a = _iota((c, c), 1)
        self.tril = _tril_ones(c)
        self.level = {}
        h = 1
        while 2 * h <= c:
            self.level[h] = (((row_a // (2 * h)) == (col_a // (2 * h)))
                             & ((row_a // h) % 2 == 1) & ((col_a // h) % 2 == 0))
            h *= 2
        self.eye = row_a == col_a


def _small_level_factors(g):
    c = g.shape[0]
    r4 = _iota(g.shape, 0) % 4
    g_prev = pltpu.roll(g, 1, 0)
    g_next = pltpu.roll(g, c - 1, 0)
    e1 = jnp.exp2(jnp.where(r4 % 2 == 1, g, 0.0))
    e2 = jnp.exp2(jnp.where(r4 == 3, g + g_prev, jnp.where(r4 == 2, g, jnp.where(r4 == 0, g_next, 0.0))))
    return e1, e2


def _glr_chunk(q, k, v, b, e1, e2, st, mk):
    c, dk = q.shape
    att = jnp.where(mk.eye, _dot_nt(q, k), 0.0)
    for h, pair_mask in mk.level.items():
        if h == 1:
            e = e1
        elif h == 2:
            e = e2
        else:
            n = c // (2 * h)
            ref = b.reshape(n, 2 * h, dk)[:, h - 1:h, :]
            refb = jnp.broadcast_to(ref, (n, 2 * h, dk)).reshape(c, dk)
            e = jnp.exp2(-jnp.abs(b - refb))
        att = jnp.where(pair_mask, _dot_nt(q * e, k * e), att)
    o = _dot(att, v) + _dot_nt(q * jnp.exp2(b), st)
    b_end = b[c - 1:c, :]
    st_new = st * jnp.exp2(b_end) + _dot_tn(v, k * jnp.exp2(b_end - b))
    return o, st_new


def _ahead_groups(cols, slots):
    tiles = -(-cols // (2 * LANE))
    bounds = [min(cols, 2 * LANE * ((k * tiles) // slots)) for k in range(slots)] + [cols]
    return [slice(bounds[k], bounds[k + 1]) for k in range(slots)]


def _gla_kernel(hc_ref, hn_ref, w_ref, s0_ref, gup_ref, gbias_ref, gnorm_ref, o_ref, sout_ref,
                s_scr, pa_scr, pb_scr, *, c, nsub):
    ci = pl.program_id(1)
    half = max(nsub // 2, 1)
    hrows = c * half
    groups = _ahead_groups(GLA_COLS_P, GLA_HEADS * half)

    @pl.when(ci == 0)
    def _():
        s_scr[...] = s0_ref[0]

    @pl.when(jnp.logical_or(ci == 0, nsub == 1))
    def _():
        pa_scr[...] = jnp.dot(hc_ref[0:hrows, :], w_ref[...], preferred_element_type=F32)

    mk = _GlrMasks(c)
    gnorm = gnorm_ref[...]
    for i in range(nsub):
        rs = slice(i * c, (i + 1) * c)
        src = pa_scr if i < half else pb_scr
        p = src[(i % half) * c:(i % half + 1) * c, :]
        q_all = p[:, 0:GLA_QW] * (GLA_DK ** -0.5)
        k_all = p[:, GLA_QW:2 * GLA_QW]
        v_all = p[:, 2 * GLA_QW:2 * GLA_QW + GLA_VW]
        gout = p[:, 2 * GLA_QW + GLA_VW:2 * GLA_QW + 2 * GLA_VW]
        gdown = p[:, 2 * GLA_QW + 2 * GLA_VW:]
        g_all = (_log_sigmoid(_dot(gdown, gup_ref[...]) + gbias_ref[...])
                 * (LOG2_E / GLA_GATE_NORMALIZER))
        b_all = _dot_exact_lhs(mk.tril, g_all)
        e1_all, e2_all = _small_level_factors(g_all)
        for h in range(GLA_HEADS):
            ks = slice(h * GLA_DKP, (h + 1) * GLA_DKP)
            vs = slice(h * GLA_DVP, (h + 1) * GLA_DVP)
            o, st = _glr_chunk(q_all[:, ks], k_all[:, ks], v_all[:, vs], b_all[:, ks],
                               e1_all[:, ks], e2_all[:, ks], s_scr[h], mk)
            s_scr[h] = st
            ms = jnp.sum(o * o, axis=-1, keepdims=True) * (1.0 / GLA_DV)
            o = o * lax.rsqrt(ms + EPS) * gnorm * _silu(gout[:, vs])
            o_ref[rs, vs] = o.astype(BF16)
            if nsub > 1:
                g = groups[(i % half) * GLA_HEADS + h]
                ahead, dst = (hc_ref[hrows:, :], pb_scr) if i < half else (hn_ref[0:hrows, :], pa_scr)
                dst[:, g] = jnp.dot(ahead, w_ref[:, g], preferred_element_type=F32)

    @pl.when(ci == pl.num_programs(1) - 1)
    def _():
        sout_ref[0] = s_scr[...]


def _gla_mixer(hx, w, s0t, gup, gbias, gnorm, b, t, c):
    nsub = 4 if t % (4 * c) == 0 else 1
    rows = c * nsub
    hrows = c * max(nsub // 2, 1)
    nc = t // rows
    nblk = b * nc
    kern = functools.partial(_gla_kernel, c=c, nsub=nsub)
    full = lambda a: pl.BlockSpec(a.shape, lambda i, j: (0,) * a.ndim)
    st_spec = pl.BlockSpec((1, GLA_HEADS, GLA_DVP, GLA_DKP), lambda i, j: (i, 0, 0, 0))
    d = hx.shape[1]
    return pl.pallas_call(
        kern,
        grid=(b, nc),
        in_specs=[pl.BlockSpec((rows, d), lambda i, j: (i * nc + j, 0)),
                  pl.BlockSpec((rows, d), lambda i, j: (jnp.minimum(i * nc + j + 1, nblk - 1), 0)),
                  pl.BlockSpec(w.shape, lambda i, j: (0, 0), pipeline_mode=pl.Buffered(1)),
                  st_spec, full(gup), full(gbias), full(gnorm)],
        out_specs=[pl.BlockSpec((rows, GLA_VW), lambda i, j: (i * nc + j, 0)), st_spec],
        out_shape=[jax.ShapeDtypeStruct((b * t, GLA_VW), BF16),
                   jax.ShapeDtypeStruct((b, GLA_HEADS, GLA_DVP, GLA_DKP), F32)],
        scratch_shapes=[pltpu.VMEM((GLA_HEADS, GLA_DVP, GLA_DKP), F32),
                        pltpu.VMEM((hrows, GLA_COLS_P), F32), pltpu.VMEM((hrows, GLA_COLS_P), F32)],
        compiler_params=_compiler_params(("arbitrary", "arbitrary")),
    )(hx, hx, w, s0t, gup, gbias, gnorm)


def _hgrn_kernel(hc_ref, hn_ref, w_ref, s0_ref, lbl_ref, hnorm_ref, o_ref, sout_ref,
                 s_scr, pa_scr, pb_scr, *, c, nsub, layer):
    ci = pl.program_id(1)
    half = max(nsub // 2, 1)
    hrows = c * half
    gw = HGRN_COLS // (HGRN_HEADS * half)

    @pl.when(ci == 0)
    def _():
        s_scr[...] = s0_ref[0]

    @pl.when(jnp.logical_or(ci == 0, nsub == 1))
    def _():
        pa_scr[...] = jnp.dot(hc_ref[0:hrows, :], w_ref[...], preferred_element_type=F32)

    lg = lbl_ref[...]
    e = jnp.exp(lg - jnp.max(lg, axis=0, keepdims=True))
    prob = e / jnp.sum(e, axis=0, keepdims=True)
    lb = jnp.zeros((1, HGRN_WIDTH), F32)
    for i in range(1, layer + 1):
        lb = lb + prob[i:i + 1, :]

    mk = _GlrMasks(c)
    hnorm = hnorm_ref[...]

    for i in range(nsub):
        rs = slice(i * c, (i + 1) * c)
        src = pa_scr if i < half else pb_scr
        p = src[(i % half) * c:(i % half + 1) * c, :]
        hq = p[:, 0:HGRN_WIDTH]
        hf = p[:, HGRN_WIDTH:2 * HGRN_WIDTH]
        hi = p[:, 2 * HGRN_WIDTH:3 * HGRN_WIDTH]
        hg = p[:, 3 * HGRN_WIDTH:]
        q_all = _silu(hq)
        f_all = lb + (1.0 - lb) * _sigmoid(hf)
        k_all = 1.0 - f_all
        g_all = jnp.log2(f_all)
        b_all = _dot_exact_lhs(mk.tril, g_all)
        e1_all, e2_all = _small_level_factors(g_all)
        for h in range(HGRN_HEADS):
            s = slice(h * HGRN_D, (h + 1) * HGRN_D)
            o, st = _glr_chunk(q_all[:, s], k_all[:, s], hi[:, s], b_all[:, s],
                               e1_all[:, s], e2_all[:, s], s_scr[h], mk)
            s_scr[h] = st
            o = _rms_rows(o, hnorm) * _silu(hg[:, s])
            o_ref[rs, s] = o.astype(BF16)
            if nsub > 1:
                k = (i % half) * HGRN_HEADS + h
                g = slice(k * gw, (k + 1) * gw)
                ahead, dst = (hc_ref[hrows:, :], pb_scr) if i < half else (hn_ref[0:hrows, :], pa_scr)
                dst[:, g] = jnp.dot(ahead, w_ref[:, g], preferred_element_type=F32)

    @pl.when(ci == pl.num_programs(1) - 1)
    def _():
        sout_ref[0] = s_scr[...]


def _hgrn_mixer(hx, w, s0t, lb_logits, hnorm, b, t, c, layer):
    nsub = 4 if t % (4 * c) == 0 else 1
    rows = c * nsub
    hrows = c * max(nsub // 2, 1)
    nc = t // rows
    nblk = b * nc
    kern = functools.partial(_hgrn_kernel, c=c, nsub=nsub, layer=layer)
    full = lambda a: pl.BlockSpec(a.shape, lambda i, j: (0,) * a.ndim)
    st_spec = pl.BlockSpec((1, HGRN_HEADS, HGRN_D, HGRN_D), lambda i, j: (i, 0, 0, 0))
    d = hx.shape[1]
    return pl.pallas_call(
        kern,
        grid=(b, nc),
        in_specs=[pl.BlockSpec((rows, d), lambda i, j: (i * nc + j, 0)),
                  pl.BlockSpec((rows, d), lambda i, j: (jnp.minimum(i * nc + j + 1, nblk - 1), 0)),
                  pl.BlockSpec(w.shape, lambda i, j: (0, 0), pipeline_mode=pl.Buffered(1)),
                  st_spec, full(lb_logits), full(hnorm)],
        out_specs=[pl.BlockSpec((rows, HGRN_WIDTH), lambda i, j: (i * nc + j, 0)), st_spec],
        out_shape=[jax.ShapeDtypeStruct((b * t, HGRN_WIDTH), BF16),
                   jax.ShapeDtypeStruct((b, HGRN_HEADS, HGRN_D, HGRN_D), F32)],
        scratch_shapes=[pltpu.VMEM((HGRN_HEADS, HGRN_D, HGRN_D), F32),
                        pltpu.VMEM((hrows, HGRN_COLS), F32), pltpu.VMEM((hrows, HGRN_COLS), F32)],
        compiler_params=_compiler_params(("arbitrary", "arbitrary")),
    )(hx, hx, w, s0t, lb_logits, hnorm)


def _rwkv_kernel(hc_ref, hn_ref, w_ref, shift0_ref, s0_ref, mu_ref, w0_ref, wup_ref, a0_ref,
                 aup_ref, gup_ref, kk_ref, ka_ref, rk_ref, lnw_ref, lnb_ref,
                 o_ref, shift_ref, sout_ref, s_scr, carry_scr, pa_scr, pb_scr, *, c, nsub):
    ci = pl.program_id(1)
    w = RWKV_WIDTH
    rows = c * nsub
    grp = min(nsub, 2)
    grows = c * grp

    @pl.when(ci == 0)
    def _():
        s_scr[...] = s0_ref[0]
        carry_scr[...] = shift0_ref[0]

    @pl.when(jnp.logical_or(ci == 0, nsub == 1))
    def _():
        pa_scr[...] = jnp.dot(hc_ref[0:grows, :], w_ref[...], preferred_element_type=F32)

    first_prev = carry_scr[...]
    grp_count = rows // grows
    col_tiles = _ahead_groups(RWKV_COLS, -(-RWKV_COLS // (2 * LANE)))

    c2 = 2 * c
    lane = _iota((c, LANE), 1)
    lo_half = lane < RWKV_HD
    ri = _iota((c2, c2), 0)
    cj = _iota((c2, c2), 1)
    strict = (ri % c) > (cj % c)
    incl = (ri % c) >= (cj % c)
    eye = jnp.where(ri == cj, 1.0, 0.0)
    inv_levels = []
    blk = 1
    while blk < c:
        inv_levels.append(((ri // (2 * blk)) == (cj // (2 * blk)))
                          & ((ri // blk) % 2 == 1) & ((cj // blk) % 2 == 0))
        blk *= 2
    tr, tc = _iota((grows, grows), 0), _iota((grows, grows), 1)
    tril = jnp.where((tr >= tc) & (tr // c == tc // c), 1.0, 0.0).astype(BF16)
    row = _iota((grows, RWKV_COLS), 0)
    seg = jnp.where((_iota((LANE, LANE), 0) // RWKV_HD) == (_iota((LANE, LANE), 1) // RWKV_HD),
                    1.0, 0.0).astype(BF16)

    def stack(x):
        return jnp.concatenate([jnp.where(lo_half, x, 0.0), jnp.where(lo_half, 0.0, x)], axis=0)

    def head_sum(x):
        return jnp.concatenate([_dot(x[:, j * LANE:(j + 1) * LANE], seg)
                                for j in range(RWKV_PAIRS)], axis=1)

    def prepare(gi):
        g0 = gi * grows
        src = pb_scr if gi else pa_scr
        p = src[...]
        before = pa_scr[grows - 1:grows, :] if gi else first_prev
        if nsub == 1:
            todo = iter(())
        elif gi == 0:
            todo = iter([(hc_ref, slice(grows, 2 * grows), pb_scr, g) for g in col_tiles])
        else:
            todo = iter([(hn_ref, slice(0, grows), pa_scr, g) for g in col_tiles])

        def project_one():
            t = next(todo, None)
            if t is not None:
                h_ref, h_rows, dst, g = t
                dst[:, g] = jnp.dot(h_ref[h_rows, :], w_ref[:, g], preferred_element_type=F32)

        prev = jnp.where(row == 0, before, pltpu.roll(p, 1, 0))
        pm = p + (prev - p) * mu_ref[...]
        project_one()
        r_all = pm[:, 0:w]
        k_all = pm[:, w:2 * w]
        v_all = pm[:, 2 * w:3 * w]
        wa = pm[:, 3 * w:3 * w + LANE]
        gd = pm[:, 3 * w + LANE:]
        lw_all = -jnp.exp(_log_sigmoid(w0_ref[...] + _dot(jnp.tanh(wa), wup_ref[...])) - 0.5)
        a_all = _sigmoid(a0_ref[...] + _dot(wa, aup_ref[...]))
        project_one()
        gate_all = _dot(_sigmoid(gd), gup_ref[...])
        kk_all = k_all * kk_ref[...]
        kk_all = kk_all * lax.rsqrt(jnp.maximum(head_sum(kk_all * kk_all), 1e-24))
        project_one()
        k_all = k_all * (1.0 + (a_all - 1.0) * ka_ref[...])
        beta_all = kk_all * a_all
        bonus_all = head_sum(r_all * k_all * rk_ref[...]) * v_all
        project_one()
        cin_all = _dot_exact_lhs(tril, lw_all, parts=2)
        ginv = jnp.exp(-cin_all)
        kap_all = kk_all * jnp.exp(cin_all - lw_all)
        project_one()
        rt_all = r_all * jnp.exp(cin_all)
        kt_all = k_all * ginv
        bt_all = beta_all * ginv
        out = []
        for i in range(grp):
            rs = slice(i * c, (i + 1) * c)
            cend = cin_all[(i + 1) * c - 1:(i + 1) * c, :]
            gend = jnp.exp(cend - cin_all[rs])
            kend, bend, dec = k_all[rs] * gend, beta_all[rs] * gend, jnp.exp(cend)
            for j in range(RWKV_PAIRS):
                s = slice(j * LANE, (j + 1) * LANE)
                out.append(dict(
                    s=s, rs=slice(g0 + i * c, g0 + (i + 1) * c),
                    kap_s=stack(kap_all[rs, s]), r_s=stack(rt_all[rs, s]), k_s=stack(kt_all[rs, s]),
                    b_s=stack(bt_all[rs, s]), v_s=stack(v_all[rs, s]), kend_s=stack(kend[:, s]),
                    bend_s=stack(bend[:, s]), dec=dec[:, s], bonus=bonus_all[rs, s],
                    gate=gate_all[rs, s]))
                if j % 2 == 0:
                    project_one()
        for _ in col_tiles:
            project_one()
        if gi == grp_count - 1:
            last = src[grows - 1:grows, :]
            carry_scr[...] = last
            shift_ref[0] = last
        return out

    def dot_nt2(lhs, r1, r2):
        if r1.shape[0] % LANE:
            return _dot_nt(lhs, r1), _dot_nt(lhs, r2)
        both = _dot_nt(lhs, jnp.concatenate([r1, r2], axis=0))
        return both[:, :r1.shape[0]], both[:, r1.shape[0]:]

    def dot2(lhs, r1, r2):
        both = _dot(lhs, jnp.concatenate([r1, r2], axis=1))
        return both[:, :r1.shape[1]], both[:, r1.shape[1]:]

    pairs = []
    for gi in range(grp_count):
        group = prepare(gi)
        for pr in group:
            a1, a2 = dot_nt2(pr['kap_s'], pr['k_s'], pr['b_s'])
            pr['a1'], pr['a2'] = jnp.where(strict, a1, 0.0), jnp.where(strict, a2, 0.0)
        for pr in group:
            a3, a4 = dot_nt2(pr['r_s'], pr['k_s'], pr['b_s'])
            pr['a3'], pr['a4'] = jnp.where(incl, a3, 0.0), jnp.where(incl, a4, 0.0)
        for pr in group:
            pr['a1v'] = _dot(pr['a1'], pr['v_s'])
            pr['a3v'] = _dot(pr['a3'], pr['v_s'])
            pr['kv'] = _dot_tn(pr['v_s'], pr['kend_s'])
        pairs += group
    if c2 % LANE == 0 and len(pairs) % 2 == 0:
        left = _iota((c2, 2 * c2), 1) < c2
        tile2 = lambda x: jnp.concatenate([x, x], axis=1)

        def block_diag(x, keep=None):
            top = left if keep is None else keep & left
            bot = ~left if keep is None else keep & ~left
            x = x.astype(BF16)
            return jnp.concatenate([jnp.where(top, x, 0.0), jnp.where(bot, x, 0.0)], axis=0)

        a2w = [jnp.concatenate([pa['a2'], pb['a2']], axis=1) for pa, pb in zip(pairs[0::2], pairs[1::2])]
        tw = [tile2(eye) - jnp.where(tile2(inv_levels[0]), a, 0.0) for a in a2w]
        for m in inv_levels[1:]:
            mw = tile2(m)
            half = [_dot(t, block_diag(a, mw)) for t, a in zip(tw, a2w)]
            tw = [t - _dot(hf, block_diag(t)) for t, hf in zip(tw, half)]
        tinv = [x for t in tw for x in (t[:, :c2], t[:, c2:])]
    else:
        tinv = [eye - jnp.where(inv_levels[0], pr['a2'], 0.0) for pr in pairs]
        for m in inv_levels[1:]:
            half = [_dot(t, jnp.where(m, pr['a2'], 0.0)) for t, pr in zip(tinv, pairs)]
            tinv = [t - _dot(hf, t) for t, hf in zip(tinv, half)]
    for pr, tj in zip(pairs, tinv):
        pr['tk'], pr['tv'] = dot2(tj, pr['kap_s'], pr['a1v'])
    for pr in pairs:
        a4tk, a4tv = dot2(pr['a4'], pr['tk'], pr['tv'])
        pr['reff'] = pr['r_s'] - a4tk
        pr['oc'] = pr['a3v'] - a4tv
        pr['m'] = _dot_tn(pr['tk'], pr['bend_s'])
        pr['q'] = pr['kv'] - _dot_tn(pr['tv'], pr['bend_s'])
    sts = [s_scr[j] for j in range(RWKV_PAIRS)]
    os_ = []
    for i in range(nsub):
        sub = list(zip(pairs[i * RWKV_PAIRS:(i + 1) * RWKV_PAIRS], sts))
        o2s = [_dot_nt(pr['reff'], st) + pr['oc'] for pr, st in sub]
        sts = [st * pr['dec'] - _dot(st, pr['m']) + pr['q'] for pr, st in sub]
        os_ += [o2[0:c] + o2[c:c2] for o2 in o2s]
    for j, st in enumerate(sts):
        s_scr[j] = st
    ds = [o - _dot(o, seg) * (1.0 / RWKV_HD) for o in os_]
    vars_ = [_dot(d * d, seg) * (1.0 / RWKV_HD) for d in ds]
    for pr, d, var in zip(pairs, ds, vars_):
        s, rs = pr['s'], pr['rs']
        on = d * lax.rsqrt(var + RWKV_GN_EPS) * lnw_ref[:, s] + lnb_ref[:, s]
        o_ref[rs, s] = ((on + pr['bonus']) * pr['gate']).astype(BF16)

    @pl.when(ci == pl.num_programs(1) - 1)
    def _():
        sout_ref[0] = s_scr[...]


def _rwkv_mixer(hx, w, shift0, s0, wts, b, t, c, nsub):
    rows = c * nsub
    grows = c * min(nsub, 2)
    nc = t // rows
    nblk = b * nc
    d = hx.shape[1]
    kern = functools.partial(_rwkv_kernel, c=c, nsub=nsub)
    full = lambda a: pl.BlockSpec(a.shape, lambda i, j: (0,) * a.ndim)
    st_spec = pl.BlockSpec((1, RWKV_PAIRS, LANE, LANE), lambda i, j: (i, 0, 0, 0))
    sh_spec = pl.BlockSpec((1, 1, RWKV_COLS), lambda i, j: (i, 0, 0))
    return pl.pallas_call(
        kern,
        grid=(b, nc),
        in_specs=[pl.BlockSpec((rows, d), lambda i, j: (i * nc + j, 0)),
                  pl.BlockSpec((rows, d), lambda i, j: (jnp.minimum(i * nc + j + 1, nblk - 1), 0)),
                  pl.BlockSpec(w.shape, lambda i, j: (0, 0), pipeline_mode=pl.Buffered(1)),
                  sh_spec, st_spec] + [full(a) for a in wts],
        out_specs=[pl.BlockSpec((rows, RWKV_WIDTH), lambda i, j: (i * nc + j, 0)), sh_spec, st_spec],
        out_shape=[jax.ShapeDtypeStruct((b * t, RWKV_WIDTH), BF16),
                   jax.ShapeDtypeStruct((b, 1, RWKV_COLS), F32),
                   jax.ShapeDtypeStruct((b, RWKV_PAIRS, LANE, LANE), F32)],
        scratch_shapes=[pltpu.VMEM((RWKV_PAIRS, LANE, LANE), F32), pltpu.VMEM((1, RWKV_COLS), F32),
                        pltpu.VMEM((grows, RWKV_COLS), F32), pltpu.VMEM((grows, RWKV_COLS), F32)],
        compiler_params=_compiler_params(("arbitrary", "arbitrary")),
    )(hx, hx, w, shift0, s0, *wts)


def _rwkv_cols_in(a):
    return jnp.concatenate([a[..., 0:640], a[..., 704:1984], a[..., 640:704], a[..., 1984:2176]], axis=-1)


def _rwkv_cols_out(a):
    return jnp.concatenate([a[..., 0:640], a[..., 1920:1984], a[..., 640:1920], a[..., 1984:2176]], axis=-1)


def _pad_heads(w, heads, d, dp):
    lead = w.shape[:-1]
    w = w.reshape(lead + (heads, d))
    w = jnp.pad(w, [(0, 0)] * len(lead) + [(0, 0), (0, dp - d)])
    return w.reshape(lead + (heads * dp,))


def _prep_weights(w_in, gla_gate_up, gla_gate_bias, gla_norm, rwkv_mu, rwkv_w_up, rwkv_a_up,
                  rwkv_r_k, w_out):
    wb = w_in.astype(BF16)
    zeros = lambda n: jnp.zeros(wb.shape[:-1] + (n,), BF16)

    def padded_heads(start, d, dp):
        return [piece for h in range(GLA_HEADS)
                for piece in (wb[..., start + h * d:start + (h + 1) * d], zeros(dp - d))]

    w_gla = jnp.concatenate(
        padded_heads(0, GLA_DK, GLA_DKP) + padded_heads(384, GLA_DK, GLA_DKP)
        + padded_heads(768, GLA_DV, GLA_DVP) + padded_heads(1552, GLA_DV, GLA_DVP)
        + [wb[..., 1536:1552], zeros(GLA_RANKP - GLA_RANK)], axis=-1)
    w_rwkv = _rwkv_cols_in(wb[..., 2320:2320 + RWKV_COLS])
    w_hgrn = wb[..., 2320 + RWKV_COLS:]
    gup = jnp.pad(_pad_heads(gla_gate_up, GLA_HEADS, GLA_DK, GLA_DKP),
                  ((0, 0), (0, GLA_RANKP - GLA_RANK), (0, 0))).astype(BF16)
    gbias = _pad_heads(gla_gate_bias, GLA_HEADS, GLA_DK, GLA_DKP)[:, None, :]
    gnorm = jnp.pad(gla_norm, ((0, 0), (0, GLA_DVP - GLA_DV)))[:, None, :]
    mu = _rwkv_cols_in(rwkv_mu)[:, None, :]
    wup = jnp.pad(rwkv_w_up, ((0, 0), (0, 64), (0, 0))).astype(BF16)
    aup = jnp.pad(rwkv_a_up, ((0, 0), (64, 0), (0, 0))).astype(BF16)
    rk = rwkv_r_k.reshape(DEPTH, 1, RWKV_WIDTH)
    wo_a = w_out[:, :768].reshape(DEPTH, GLA_HEADS, GLA_DV, D_MODEL)
    wo_a = jnp.pad(wo_a, ((0, 0), (0, 0), (0, GLA_DVP - GLA_DV), (0, 0)))
    wo_a = wo_a.reshape(DEPTH, GLA_VW, D_MODEL).astype(BF16)
    wo_b = w_out[:, 768:1408].astype(BF16)
    wo_c = w_out[:, 1408:].astype(BF16)
    return w_gla, w_rwkv, w_hgrn, gup, gbias, gnorm, mu, wup, aup, rk, wo_a, wo_b, wo_c


def _gla_state_in(s):
    s = jnp.swapaxes(s, -1, -2)
    return jnp.pad(s, [(0, 0)] * 3 + [(0, GLA_DVP - GLA_DV), (0, GLA_DKP - GLA_DK)])


def _gla_state_out(s):
    return jnp.swapaxes(s[..., :GLA_DV, :GLA_DK], -1, -2)


def _rwkv_state_in(s):
    d, b = s.shape[:2]
    s = s.reshape(d, b, RWKV_PAIRS, 2, RWKV_HD, RWKV_HD)
    out = jnp.einsum('dbjhvk,hg->dbjhvgk', s, jnp.eye(2, dtype=s.dtype))
    return out.reshape(d, b, RWKV_PAIRS, LANE, LANE)


def _rwkv_state_out(s):
    d, b = s.shape[:2]
    s = s.reshape(d, b, RWKV_PAIRS, 2, RWKV_HD, 2, RWKV_HD)
    out = jnp.stack([s[:, :, :, 0, :, 0, :], s[:, :, :, 1, :, 1, :]], axis=3)
    return out.reshape(d, b, RWKV_HEADS, RWKV_HD, RWKV_HD)


def _row_tile(n, cap):
    t = cap
    while n % t:
        t //= 2
    return t


def _trunk(x, s_gla, s_rwkv, s_shift, s_hgrn, wt, b, t):
    n = b * t
    tm = _row_tile(n, 512)
    c_glr = min(128, t)
    c_rwkv = min(64, t)
    n_rwkv = 4 if t % (4 * c_rwkv) == 0 else 1
    new_gla, new_rwkv, new_shift, new_hgrn = [], [], [], []
    h = _norm(x, wt['norm_mix_pre'][0], tm)
    for l in range(DEPTH):
        oa, g1 = _gla_mixer(h, wt['w_gla'][l], s_gla[l], wt['gup'][l], wt['gbias'][l],
                            wt['gnorm'][l], b, t, c_glr)
        rw = [wt[k][l] for k in ('mu', 'w0', 'wup', 'a0', 'aup', 'g_up', 'k_k', 'k_a', 'rk',
                                 'ln_w', 'ln_b')]
        ob, sh1, r1 = _rwkv_mixer(h, wt['w_rwkv'][l], s_shift[l], s_rwkv[l], rw, b, t, c_rwkv, n_rwkv)
        oc, h1 = _hgrn_mixer(h, wt['w_hgrn'][l], s_hgrn[l], wt['lb_logits'], wt['hnorm'][l],
                             b, t, c_glr, l)
        x, h = _out_proj(oa, ob, oc, wt['wo_a'][l], wt['wo_b'][l], wt['wo_c'][l], x,
                         wt['norm_mix_post'][l], wt['norm_ffn_pre'][l], tm)
        x, h = _ffn(h, x, wt['ffn_w_gate'][l], wt['ffn_w_up'][l], wt['ffn_w_down'][l],
                    wt['norm_ffn_post'][l], wt['norm_mix_pre'][(l + 1) % DEPTH], tm)
        new_gla.append(g1)
        new_rwkv.append(r1)
        new_shift.append(sh1)
        new_hgrn.append(h1)
    return x, jnp.stack(new_gla), jnp.stack(new_rwkv), jnp.stack(new_shift), jnp.stack(new_hgrn)


def _run_path(x, state_gla, state_rwkv, state_shift, state_hgrn, wt):
    b, t, d = x.shape
    y, g, r, sh, h = _trunk(
        x.reshape(b * t, d), _gla_state_in(state_gla), _rwkv_state_in(state_rwkv),
        _rwkv_cols_in(state_shift)[:, :, None, :], jnp.swapaxes(state_hgrn, -1, -2), wt, b, t)
    return (y.reshape(b, t, d), _gla_state_out(g), _rwkv_state_out(r),
            _rwkv_cols_out(sh[:, :, 0, :]), jnp.swapaxes(h, -1, -2))


def kernel(x_prompt, x_sample, state_gla, state_rwkv, state_rwkv_shift, state_hgrn, norm_mix_pre, norm_mix_post, norm_ffn_pre, norm_ffn_post, w_in, gla_gate_up, gla_gate_bias, gla_norm, rwkv_mu, rwkv_w0, rwkv_w_up, rwkv_a0, rwkv_a_up, rwkv_g_up, rwkv_k_k, rwkv_k_a, rwkv_r_k, rwkv_ln_w, rwkv_ln_b, hgrn_lb_logits, hgrn_norm, w_out, ffn_w_gate, ffn_w_up, ffn_w_down):
    (w_gla, w_rwkv, w_hgrn, gup, gbias, gnorm, mu, wup, aup, rk, wo_a, wo_b, wo_c) = _prep_weights(
        w_in, gla_gate_up, gla_gate_bias, gla_norm, rwkv_mu, rwkv_w_up, rwkv_a_up, rwkv_r_k, w_out)
    vec = lambda a: a[:, None, :]
    wt = dict(
        norm_mix_pre=vec(norm_mix_pre), norm_mix_post=vec(norm_mix_post),
        norm_ffn_pre=vec(norm_ffn_pre), norm_ffn_post=vec(norm_ffn_post),
        w_gla=w_gla, w_rwkv=w_rwkv, w_hgrn=w_hgrn, gup=gup, gbias=gbias, gnorm=gnorm,
        mu=mu, w0=vec(rwkv_w0), wup=wup, a0=vec(rwkv_a0), aup=aup, g_up=rwkv_g_up.astype(BF16),
        k_k=vec(rwkv_k_k), k_a=vec(rwkv_k_a), rk=rk, ln_w=vec(rwkv_ln_w), ln_b=vec(rwkv_ln_b),
        lb_logits=hgrn_lb_logits, hnorm=vec(hgrn_norm),
        wo_a=wo_a, wo_b=wo_b, wo_c=wo_c,
        ffn_w_gate=_ffn_col_tiles(ffn_w_gate), ffn_w_up=_ffn_col_tiles(ffn_w_up),
        ffn_w_down=ffn_w_down.astype(BF16))

    bp = x_prompt.shape[0]
    zeros = lambda s: jnp.zeros((DEPTH, bp) + s.shape[2:], x_prompt.dtype)
    y_p, gla_p, rwkv_p, shift_p, hgrn_p = _run_path(
        x_prompt, zeros(state_gla), zeros(state_rwkv), zeros(state_rwkv_shift), zeros(state_hgrn), wt)
    y_s, gla_s, rwkv_s, shift_s, hgrn_s = _run_path(
        x_sample, state_gla, state_rwkv, state_rwkv_shift, state_hgrn, wt)
    return (y_p, y_s, gla_p, rwkv_p, shift_p, hgrn_p, gla_s, rwkv_s, shift_s, hgrn_s)
```

```python
import functools

import jax
import jax.numpy as jnp
from jax import lax
from jax.experimental import pallas as pl
from jax.experimental.pallas import tpu as pltpu

F32 = jnp.float32
BF16 = jnp.bfloat16

D_MODEL = 2048
DEPTH = 4
EPS = 1e-6

GLA_HEADS, GLA_DK, GLA_DV = 4, 96, 192
GLA_DKP, GLA_DVP = 128, 256
GLA_RANK, GLA_RANKP = 16, 128
GLA_GATE_NORMALIZER = 16.0
GLA_QW = GLA_HEADS * GLA_DKP
GLA_VW = GLA_HEADS * GLA_DVP
GLA_COLS_P = 2 * GLA_QW + 2 * GLA_VW + GLA_RANKP

RWKV_HEADS, RWKV_HD = 10, 64
RWKV_WIDTH = RWKV_HEADS * RWKV_HD
RWKV_PAIRS = RWKV_HEADS // 2
RWKV_COLS = 2176
RWKV_GN_EPS = 64e-5

HGRN_HEADS, HGRN_D = 5, 128
HGRN_WIDTH = HGRN_HEADS * HGRN_D
HGRN_COLS = 4 * HGRN_WIDTH

D_FF = 5632
FFN_TF = 512
LANE = 128
LOG2_E = 1.4426950408889634

VMEM_LIMIT = 56 * 1024 * 1024


def _sigmoid(x):
    return 1.0 / (1.0 + jnp.exp(-x))


def _silu(x):
    return x * _sigmoid(x)


def _log_sigmoid(x):
    return jnp.minimum(x, 0.0) - jnp.log(1.0 + jnp.exp(-jnp.abs(x)))


def _dot(a, b):
    return jnp.dot(a.astype(BF16), b.astype(BF16), preferred_element_type=F32)


def _dot_nt(a, b):
    return lax.dot_general(a.astype(BF16), b.astype(BF16), (((1,), (1,)), ((), ())),
                           preferred_element_type=F32)


def _dot_tn(a, b):
    return lax.dot_general(a.astype(BF16), b.astype(BF16), (((0,), (0,)), ((), ())),
                           preferred_element_type=F32)


def _split_bf16(x, parts):
    out = []
    r = x
    for i in range(parts):
        p = r.astype(BF16)
        out.append(p)
        if i + 1 < parts:
            r = r - p.astype(F32)
    return out


def _dot_exact_lhs(m_bf16, x, parts=3):
    acc = None
    for p in _split_bf16(x, parts):
        t = jnp.dot(m_bf16, p, preferred_element_type=F32)
        acc = t if acc is None else acc + t
    return acc


def _iota(shape, dim):
    return lax.broadcasted_iota(jnp.int32, shape, dim)


def _tril_ones(c):
    return jnp.where(_iota((c, c), 0) >= _iota((c, c), 1), 1.0, 0.0).astype(BF16)


def _rms_rows(x, g):
    ms = jnp.mean(x * x, axis=-1, keepdims=True)
    return x * lax.rsqrt(ms + EPS) * g


def _compiler_params(sem, flags=None):
    return pltpu.CompilerParams(dimension_semantics=sem, vmem_limit_bytes=VMEM_LIMIT, flags=flags)


def _norm_kernel(x_ref, g_ref, h_ref):
    h_ref[...] = _rms_rows(x_ref[...], g_ref[...]).astype(BF16)


def _norm(x, g, tm):
    n, d = x.shape
    return pl.pallas_call(
        _norm_kernel,
        grid=(n // tm,),
        in_specs=[pl.BlockSpec((tm, d), lambda i: (i, 0)), pl.BlockSpec((1, d), lambda i: (0, 0))],
        out_specs=pl.BlockSpec((tm, d), lambda i: (i, 0)),
        out_shape=jax.ShapeDtypeStruct((n, d), BF16),
        compiler_params=_compiler_params(("arbitrary",)),
    )(x, g)


def _out_proj_kernel(oa_ref, ob_ref, oc_ref, wa_ref, wb_ref, wc_ref, x_ref, g_ref, gnext_ref,
                     y_ref, hn_ref):
    half = x_ref.shape[0] // 2
    for r in (slice(0, half), slice(half, 2 * half)):
        m = (jnp.dot(oa_ref[r, :], wa_ref[...], preferred_element_type=F32)
             + jnp.dot(ob_ref[r, :], wb_ref[...], preferred_element_type=F32)
             + jnp.dot(oc_ref[r, :], wc_ref[...], preferred_element_type=F32))
        y = x_ref[r, :] + _rms_rows(m, g_ref[...])
        y_ref[r, :] = y
        hn_ref[r, :] = _rms_rows(y, gnext_ref[...]).astype(BF16)


def _out_proj(oa, ob, oc, wa, wb, wc, x, g, gnext, tm):
    n, d = x.shape
    row = lambda w: pl.BlockSpec((tm, w), lambda i: (i, 0))
    full = lambda a: pl.BlockSpec(a.shape, lambda i: (0, 0))
    return pl.pallas_call(
        _out_proj_kernel,
        grid=(n // tm,),
        in_specs=[row(oa.shape[1]), row(ob.shape[1]), row(oc.shape[1]),
                  full(wa), full(wb), full(wc), row(d), full(g), full(gnext)],
        out_specs=[row(d), row(d)],
        out_shape=[jax.ShapeDtypeStruct((n, d), F32), jax.ShapeDtypeStruct((n, d), BF16)],
        compiler_params=_compiler_params(("arbitrary",)),
    )(oa, ob, oc, wa, wb, wc, x, g, gnext)


def _ffn_kernel(h_ref, x_ref, wg_ref, wu_ref, wd_ref, gpost_ref, gnext_ref, y_ref, hn_ref, acc_scr):
    j = pl.program_id(1)

    @pl.when(j == 0)
    def _():
        acc_scr[...] = jnp.zeros_like(acc_scr)

    h = h_ref[...]
    a = jnp.dot(h, wg_ref[...], preferred_element_type=F32)
    u = jnp.dot(h, wu_ref[...], preferred_element_type=F32)
    acc_scr[...] += jnp.dot((_silu(a) * u).astype(BF16), wd_ref[...], preferred_element_type=F32)

    @pl.when(j == pl.num_programs(1) - 1)
    def _():
        y = x_ref[...] + _rms_rows(acc_scr[...], gpost_ref[...])
        y_ref[...] = y
        hn_ref[...] = _rms_rows(y, gnext_ref[...]).astype(BF16)


def _ffn(h, x, wg, wu, wd, gpost, gnext, tm):
    n, d = x.shape
    dff = wg.shape[1]
    tf = FFN_TF
    return pl.pallas_call(
        _ffn_kernel,
        grid=(n // tm, dff // tf),
        in_specs=[pl.BlockSpec((tm, d), lambda i, j: (i, 0)),
                  pl.BlockSpec((tm, d), lambda i, j: (i, 0)),
                  pl.BlockSpec((d, tf), lambda i, j: (0, j)),
                  pl.BlockSpec((d, tf), lambda i, j: (0, j)),
                  pl.BlockSpec((tf, d), lambda i, j: (j, 0)),
                  pl.BlockSpec((1, d), lambda i, j: (0, 0)),
                  pl.BlockSpec((1, d), lambda i, j: (0, 0))],
        out_specs=[pl.BlockSpec((tm, d), lambda i, j: (i, 0)),
                   pl.BlockSpec((tm, d), lambda i, j: (i, 0))],
        out_shape=[jax.ShapeDtypeStruct((n, d), F32), jax.ShapeDtypeStruct((n, d), BF16)],
        scratch_shapes=[pltpu.VMEM((tm, d), F32)],
        compiler_params=_compiler_params(("arbitrary", "arbitrary")),
    )(h, x, wg, wu, wd, gpost, gnext)


class _GlrMasks:
    def __init__(self, c):
        self.c = c
        row_a = _iota((c, c), 0)
        col_a = _iota((c, c), 1)
        self.tril = _tril_ones(c)
        self.level = {}
        h = 1
        while 2 * h <= c:
            self.level[h] = (((row_a // (2 * h)) == (col_a // (2 * h)))
                             & ((row_a // h) % 2 == 1) & ((col_a // h) % 2 == 0))
            h *= 2
        self.eye = row_a == col_a


def _small_level_factors(g):
    c = g.shape[0]
    r4 = _iota(g.shape, 0) % 4
    g_prev = pltpu.roll(g, 1, 0)
    g_next = pltpu.roll(g, c - 1, 0)
    e1 = jnp.exp2(jnp.where(r4 % 2 == 1, g, 0.0))
    e2 = jnp.exp2(jnp.where(r4 == 3, g + g_prev, jnp.where(r4 == 2, g, jnp.where(r4 == 0, g_next, 0.0))))
    return e1, e2


def _glr_chunk(q, k, v, b, e1, e2, st, mk):
    c, dk = q.shape
    att = jnp.where(mk.eye, _dot_nt(q, k), 0.0)
    for h, pair_mask in mk.level.items():
        if h == 1:
            e = e1
        elif h == 2:
            e = e2
        else:
            n = c // (2 * h)
            ref = b.reshape(n, 2 * h, dk)[:, h - 1:h, :]
            refb = jnp.broadcast_to(ref, (n, 2 * h, dk)).reshape(c, dk)
            e = jnp.exp2(-jnp.abs(b - refb))
        att = jnp.where(pair_mask, _dot_nt(q * e, k * e), att)
    o = _dot(att, v) + _dot_nt(q * jnp.exp2(b), st)
    b_end = b[c - 1:c, :]
    st_new = st * jnp.exp2(b_end) + _dot_tn(v, k * jnp.exp2(b_end - b))
    return o, st_new


def _ahead_groups(cols, slots):
    tiles = -(-cols // (2 * LANE))
    bounds = [min(cols, 2 * LANE * ((k * tiles) // slots)) for k in range(slots)] + [cols]
    return [slice(bounds[k], bounds[k + 1]) for k in range(slots)]


def _gla_kernel(hc_ref, hn_ref, w_ref, s0_ref, gup_ref, gbias_ref, gnorm_ref, o_ref, sout_ref,
                s_scr, pa_scr, pb_scr, *, c, nsub):
    ci = pl.program_id(1)
    half = max(nsub // 2, 1)
    hrows = c * half
    groups = _ahead_groups(GLA_COLS_P, GLA_HEADS * half)

    @pl.when(ci == 0)
    def _():
        s_scr[...] = s0_ref[0]

    @pl.when(jnp.logical_or(ci == 0, nsub == 1))
    def _():
        pa_scr[...] = jnp.dot(hc_ref[0:hrows, :], w_ref[...], preferred_element_type=F32)

    mk = _GlrMasks(c)
    gnorm = gnorm_ref[...]
    for i in range(nsub):
        rs = slice(i * c, (i + 1) * c)
        src = pa_scr if i < half else pb_scr
        p = src[(i % half) * c:(i % half + 1) * c, :]
        q_all = p[:, 0:GLA_QW] * (GLA_DK ** -0.5)
        k_all = p[:, GLA_QW:2 * GLA_QW]
        v_all = p[:, 2 * GLA_QW:2 * GLA_QW + GLA_VW]
        gout = p[:, 2 * GLA_QW + GLA_VW:2 * GLA_QW + 2 * GLA_VW]
        gdown = p[:, 2 * GLA_QW + 2 * GLA_VW:]
        g_all = (_log_sigmoid(_dot(gdown, gup_ref[...]) + gbias_ref[...])
                 * (LOG2_E / GLA_GATE_NORMALIZER))
        b_all = _dot_exact_lhs(mk.tril, g_all)
        e1_all, e2_all = _small_level_factors(g_all)
        for h in range(GLA_HEADS):
            ks = slice(h * GLA_DKP, (h + 1) * GLA_DKP)
            vs = slice(h * GLA_DVP, (h + 1) * GLA_DVP)
            o, st = _glr_chunk(q_all[:, ks], k_all[:, ks], v_all[:, vs], b_all[:, ks],
                               e1_all[:, ks], e2_all[:, ks], s_scr[h], mk)
            s_scr[h] = st
            ms = jnp.sum(o * o, axis=-1, keepdims=True) * (1.0 / GLA_DV)
            o = o * lax.rsqrt(ms + EPS) * gnorm * _silu(gout[:, vs])
            o_ref[rs, vs] = o.astype(BF16)
            if nsub > 1:
                g = groups[(i % half) * GLA_HEADS + h]
                ahead, dst = (hc_ref[hrows:, :], pb_scr) if i < half else (hn_ref[0:hrows, :], pa_scr)
                dst[:, g] = jnp.dot(ahead, w_ref[:, g], preferred_element_type=F32)

    @pl.when(ci == pl.num_programs(1) - 1)
    def _():
        sout_ref[0] = s_scr[...]


def _gla_mixer(hx, w, s0t, gup, gbias, gnorm, b, t, c):
    nsub = 4 if t % (4 * c) == 0 else 1
    rows = c * nsub
    hrows = c * max(nsub // 2, 1)
    nc = t // rows
    nblk = b * nc
    kern = functools.partial(_gla_kernel, c=c, nsub=nsub)
    full = lambda a: pl.BlockSpec(a.shape, lambda i, j: (0,) * a.ndim)
    st_spec = pl.BlockSpec((1, GLA_HEADS, GLA_DVP, GLA_DKP), lambda i, j: (i, 0, 0, 0))
    d = hx.shape[1]
    return pl.pallas_call(
        kern,
        grid=(b, nc),
        in_specs=[pl.BlockSpec((rows, d), lambda i, j: (i * nc + j, 0)),
                  pl.BlockSpec((rows, d), lambda i, j: (jnp.minimum(i * nc + j + 1, nblk - 1), 0)),
                  pl.BlockSpec(w.shape, lambda i, j: (0, 0), pipeline_mode=pl.Buffered(1)),
                  st_spec, full(gup), full(gbias), full(gnorm)],
        out_specs=[pl.BlockSpec((rows, GLA_VW), lambda i, j: (i * nc + j, 0)), st_spec],
        out_shape=[jax.ShapeDtypeStruct((b * t, GLA_VW), BF16),
                   jax.ShapeDtypeStruct((b, GLA_HEADS, GLA_DVP, GLA_DKP), F32)],
        scratch_shapes=[pltpu.VMEM((GLA_HEADS, GLA_DVP, GLA_DKP), F32),
                        pltpu.VMEM((hrows, GLA_COLS_P), F32), pltpu.VMEM((hrows, GLA_COLS_P), F32)],
        compiler_params=_compiler_params(("arbitrary", "arbitrary")),
    )(hx, hx, w, s0t, gup, gbias, gnorm)


def _hgrn_kernel(hc_ref, hn_ref, w_ref, s0_ref, lbl_ref, hnorm_ref, o_ref, sout_ref,
                 s_scr, pa_scr, pb_scr, *, c, nsub, layer):
    ci = pl.program_id(1)
    half = max(nsub // 2, 1)
    hrows = c * half
    gw = HGRN_COLS // (HGRN_HEADS * half)

    @pl.when(ci == 0)
    def _():
        s_scr[...] = s0_ref[0]

    @pl.when(jnp.logical_or(ci == 0, nsub == 1))
    def _():
        pa_scr[...] = jnp.dot(hc_ref[0:hrows, :], w_ref[...], preferred_element_type=F32)

    lg = lbl_ref[...]
    e = jnp.exp(lg - jnp.max(lg, axis=0, keepdims=True))
    prob = e / jnp.sum(e, axis=0, keepdims=True)
    lb = jnp.zeros((1, HGRN_WIDTH), F32)
    for i in range(1, layer + 1):
        lb = lb + prob[i:i + 1, :]

    mk = _GlrMasks(c)
    hnorm = hnorm_ref[...]

    for i in range(nsub):
        rs = slice(i * c, (i + 1) * c)
        src = pa_scr if i < half else pb_scr
        p = src[(i % half) * c:(i % half + 1) * c, :]
        hq = p[:, 0:HGRN_WIDTH]
        hf = p[:, HGRN_WIDTH:2 * HGRN_WIDTH]
        hi = p[:, 2 * HGRN_WIDTH:3 * HGRN_WIDTH]
        hg = p[:, 3 * HGRN_WIDTH:]
        q_all = _silu(hq)
        f_all = lb + (1.0 - lb) * _sigmoid(hf)
        k_all = 1.0 - f_all
        g_all = jnp.log2(f_all)
        b_all = _dot_exact_lhs(mk.tril, g_all)
        e1_all, e2_all = _small_level_factors(g_all)
        for h in range(HGRN_HEADS):
            s = slice(h * HGRN_D, (h + 1) * HGRN_D)
            o, st = _glr_chunk(q_all[:, s], k_all[:, s], hi[:, s], b_all[:, s],
                               e1_all[:, s], e2_all[:, s], s_scr[h], mk)
            s_scr[h] = st
            o = _rms_rows(o, hnorm) * _silu(hg[:, s])
            o_ref[rs, s] = o.astype(BF16)
            if nsub > 1:
                k = (i % half) * HGRN_HEADS + h
                g = slice(k * gw, (k + 1) * gw)
                ahead, dst = (hc_ref[hrows:, :], pb_scr) if i < half else (hn_ref[0:hrows, :], pa_scr)
                dst[:, g] = jnp.dot(ahead, w_ref[:, g], preferred_element_type=F32)

    @pl.when(ci == pl.num_programs(1) - 1)
    def _():
        sout_ref[0] = s_scr[...]


def _hgrn_mixer(hx, w, s0t, lb_logits, hnorm, b, t, c, layer):
    nsub = 4 if t % (4 * c) == 0 else 1
    rows = c * nsub
    hrows = c * max(nsub // 2, 1)
    nc = t // rows
    nblk = b * nc
    kern = functools.partial(_hgrn_kernel, c=c, nsub=nsub, layer=layer)
    full = lambda a: pl.BlockSpec(a.shape, lambda i, j: (0,) * a.ndim)
    st_spec = pl.BlockSpec((1, HGRN_HEADS, HGRN_D, HGRN_D), lambda i, j: (i, 0, 0, 0))
    d = hx.shape[1]
    return pl.pallas_call(
        kern,
        grid=(b, nc),
        in_specs=[pl.BlockSpec((rows, d), lambda i, j: (i * nc + j, 0)),
                  pl.BlockSpec((rows, d), lambda i, j: (jnp.minimum(i * nc + j + 1, nblk - 1), 0)),
                  pl.BlockSpec(w.shape, lambda i, j: (0, 0), pipeline_mode=pl.Buffered(1)),
                  st_spec, full(lb_logits), full(hnorm)],
        out_specs=[pl.BlockSpec((rows, HGRN_WIDTH), lambda i, j: (i * nc + j, 0)), st_spec],
        out_shape=[jax.ShapeDtypeStruct((b * t, HGRN_WIDTH), BF16),
                   jax.ShapeDtypeStruct((b, HGRN_HEADS, HGRN_D, HGRN_D), F32)],
        scratch_shapes=[pltpu.VMEM((HGRN_HEADS, HGRN_D, HGRN_D), F32),
                        pltpu.VMEM((hrows, HGRN_COLS), F32), pltpu.VMEM((hrows, HGRN_COLS), F32)],
        compiler_params=_compiler_params(("arbitrary", "arbitrary")),
    )(hx, hx, w, s0t, lb_logits, hnorm)


def _rwkv_kernel(hc_ref, hn_ref, w_ref, shift0_ref, s0_ref, mu_ref, w0_ref, wup_ref, a0_ref,
                 aup_ref, gup_ref, kk_ref, ka_ref, rk_ref, lnw_ref, lnb_ref,
                 o_ref, shift_ref, sout_ref, s_scr, carry_scr, pa_scr, pb_scr, *, c, nsub):
    ci = pl.program_id(1)
    w = RWKV_WIDTH
    rows = c * nsub
    grp = min(nsub, 2)
    grows = c * grp

    @pl.when(ci == 0)
    def _():
        s_scr[...] = s0_ref[0]
        carry_scr[...] = shift0_ref[0]

    @pl.when(jnp.logical_or(ci == 0, nsub == 1))
    def _():
        pa_scr[...] = jnp.dot(hc_ref[0:grows, :], w_ref[...], preferred_element_type=F32)

    first_prev = carry_scr[...]
    grp_count = rows // grows
    col_tiles = _ahead_groups(RWKV_COLS, -(-RWKV_COLS // (2 * LANE)))

    c2 = 2 * c
    lane = _iota((c, LANE), 1)
    lo_half = lane < RWKV_HD
    ri = _iota((c2, c2), 0)
    cj = _iota((c2, c2), 1)
    strict = (ri % c) > (cj % c)
    incl = (ri % c) >= (cj % c)
    eye = jnp.where(ri == cj, 1.0, 0.0)
    inv_levels = []
    blk = 1
    while blk < c:
        inv_levels.append(((ri // (2 * blk)) == (cj // (2 * blk)))
                          & ((ri // blk) % 2 == 1) & ((cj // blk) % 2 == 0))
        blk *= 2
    tr, tc = _iota((grows, grows), 0), _iota((grows, grows), 1)
    tril = jnp.where((tr >= tc) & (tr // c == tc // c), 1.0, 0.0).astype(BF16)
    row = _iota((grows, RWKV_COLS), 0)
    seg = jnp.where((_iota((LANE, LANE), 0) // RWKV_HD) == (_iota((LANE, LANE), 1) // RWKV_HD),
                    1.0, 0.0).astype(BF16)

    def stack(x):
        return jnp.concatenate([jnp.where(lo_half, x, 0.0), jnp.where(lo_half, 0.0, x)], axis=0)

    def head_sum(x):
        return jnp.concatenate([_dot(x[:, j * LANE:(j + 1) * LANE], seg)
                                for j in range(RWKV_PAIRS)], axis=1)

    def prepare(gi):
        g0 = gi * grows
        src = pb_scr if gi else pa_scr
        p = src[...]
        before = pa_scr[grows - 1:grows, :] if gi else first_prev
        if nsub == 1:
            todo = iter(())
        elif gi == 0:
            todo = iter([(hc_ref, slice(grows, 2 * grows), pb_scr, g) for g in col_tiles])
        else:
            todo = iter([(hn_ref, slice(0, grows), pa_scr, g) for g in col_tiles])

        def project_one():
            t = next(todo, None)
            if t is not None:
                h_ref, h_rows, dst, g = t
                dst[:, g] = jnp.dot(h_ref[h_rows, :], w_ref[:, g], preferred_element_type=F32)

        prev = jnp.where(row == 0, before, pltpu.roll(p, 1, 0))
        pm = p + (prev - p) * mu_ref[...]
        project_one()
        r_all = pm[:, 0:w]
        k_all = pm[:, w:2 * w]
        v_all = pm[:, 2 * w:3 * w]
        wa = pm[:, 3 * w:3 * w + LANE]
        gd = pm[:, 3 * w + LANE:]
        lw_all = -jnp.exp(_log_sigmoid(w0_ref[...] + _dot(jnp.tanh(wa), wup_ref[...])) - 0.5)
        a_all = _sigmoid(a0_ref[...] + _dot(wa, aup_ref[...]))
        project_one()
        gate_all = _dot(_sigmoid(gd), gup_ref[...])
        kk_all = k_all * kk_ref[...]
        kk_all = kk_all * lax.rsqrt(jnp.maximum(head_sum(kk_all * kk_all), 1e-24))
        project_one()
        k_all = k_all * (1.0 + (a_all - 1.0) * ka_ref[...])
        beta_all = kk_all * a_all
        bonus_all = head_sum(r_all * k_all * rk_ref[...]) * v_all
        project_one()
        cin_all = _dot_exact_lhs(tril, lw_all, parts=2)
        ginv = jnp.exp(-cin_all)
        kap_all = kk_all * jnp.exp(cin_all - lw_all)
        project_one()
        rt_all = r_all * jnp.exp(cin_all)
        kt_all = k_all * ginv
        bt_all = beta_all * ginv
        out = []
        for i in range(grp):
            rs = slice(i * c, (i + 1) * c)
            cend = cin_all[(i + 1) * c - 1:(i + 1) * c, :]
            gend = jnp.exp(cend - cin_all[rs])
            kend, bend, dec = k_all[rs] * gend, beta_all[rs] * gend, jnp.exp(cend)
            for j in range(RWKV_PAIRS):
                s = slice(j * LANE, (j + 1) * LANE)
                out.append(dict(
                    s=s, rs=slice(g0 + i * c, g0 + (i + 1) * c),
                    kap_s=stack(kap_all[rs, s]), r_s=stack(rt_all[rs, s]), k_s=stack(kt_all[rs, s]),
                    b_s=stack(bt_all[rs, s]), v_s=stack(v_all[rs, s]), kend_s=stack(kend[:, s]),
                    bend_s=stack(bend[:, s]), dec=dec[:, s], bonus=bonus_all[rs, s],
                    gate=gate_all[rs, s]))
                if j % 2 == 0:
                    project_one()
        for _ in col_tiles:
            project_one()
        if gi == grp_count - 1:
            last = src[grows - 1:grows, :]
            carry_scr[...] = last
            shift_ref[0] = last
        return out

    def dot_nt2(lhs, r1, r2):
        if r1.shape[0] % LANE:
            return _dot_nt(lhs, r1), _dot_nt(lhs, r2)
        both = _dot_nt(lhs, jnp.concatenate([r1, r2], axis=0))
        return both[:, :r1.shape[0]], both[:, r1.shape[0]:]

    def dot2(lhs, r1, r2):
        both = _dot(lhs, jnp.concatenate([r1, r2], axis=1))
        return both[:, :r1.shape[1]], both[:, r1.shape[1]:]

    pairs = []
    for gi in range(grp_count):
        group = prepare(gi)
        for pr in group:
            a1, a2 = dot_nt2(pr['kap_s'], pr['k_s'], pr['b_s'])
            pr['a1'], pr['a2'] = jnp.where(strict, a1, 0.0), jnp.where(strict, a2, 0.0)
        for pr in group:
            a3, a4 = dot_nt2(pr['r_s'], pr['k_s'], pr['b_s'])
            pr['a3'], pr['a4'] = jnp.where(incl, a3, 0.0), jnp.where(incl, a4, 0.0)
        for pr in group:
            pr['a1v'] = _dot(pr['a1'], pr['v_s'])
            pr['a3v'] = _dot(pr['a3'], pr['v_s'])
            pr['kv'] = _dot_tn(pr['v_s'], pr['kend_s'])
        pairs += group
    if c2 % LANE == 0 and len(pairs) % 2 == 0:
        left = _iota((c2, 2 * c2), 1) < c2
        tile2 = lambda x: jnp.concatenate([x, x], axis=1)

        def block_diag(x, keep=None):
            top = left if keep is None else keep & left
            bot = ~left if keep is None else keep & ~left
            x = x.astype(BF16)
            return jnp.concatenate([jnp.where(top, x, 0.0), jnp.where(bot, x, 0.0)], axis=0)

        a2w = [jnp.concatenate([pa['a2'], pb['a2']], axis=1) for pa, pb in zip(pairs[0::2], pairs[1::2])]
        tw = [tile2(eye) - jnp.where(tile2(inv_levels[0]), a, 0.0) for a in a2w]
        for m in inv_levels[1:]:
            mw = tile2(m)
            half = [_dot(t, block_diag(a, mw)) for t, a in zip(tw, a2w)]
            tw = [t - _dot(hf, block_diag(t)) for t, hf in zip(tw, half)]
        tinv = [x for t in tw for x in (t[:, :c2], t[:, c2:])]
    else:
        tinv = [eye - jnp.where(inv_levels[0], pr['a2'], 0.0) for pr in pairs]
        for m in inv_levels[1:]:
            half = [_dot(t, jnp.where(m, pr['a2'], 0.0)) for t, pr in zip(tinv, pairs)]
            tinv = [t - _dot(hf, t) for t, hf in zip(tinv, half)]
    for pr, tj in zip(pairs, tinv):
        pr['tk'], pr['tv'] = dot2(tj, pr['kap_s'], pr['a1v'])
    for pr in pairs:
        a4tk, a4tv = dot2(pr['a4'], pr['tk'], pr['tv'])
        pr['reff'] = pr['r_s'] - a4tk
        pr['oc'] = pr['a3v'] - a4tv
        pr['m'] = _dot_tn(pr['tk'], pr['bend_s'])
        pr['q'] = pr['kv'] - _dot_tn(pr['tv'], pr['bend_s'])
    sts = [s_scr[j] for j in range(RWKV_PAIRS)]
    os_ = []
    for i in range(nsub):
        sub = list(zip(pairs[i * RWKV_PAIRS:(i + 1) * RWKV_PAIRS], sts))
        o2s = [_dot_nt(pr['reff'], st) + pr['oc'] for pr, st in sub]
        sts = [st * pr['dec'] - _dot(st, pr['m']) + pr['q'] for pr, st in sub]
        os_ += [o2[0:c] + o2[c:c2] for o2 in o2s]
    for j, st in enumerate(sts):
        s_scr[j] = st
    ds = [o - _dot(o, seg) * (1.0 / RWKV_HD) for o in os_]
    vars_ = [_dot(d * d, seg) * (1.0 / RWKV_HD) for d in ds]
    for pr, d, var in zip(pairs, ds, vars_):
        s, rs = pr['s'], pr['rs']
        on = d * lax.rsqrt(var + RWKV_GN_EPS) * lnw_ref[:, s] + lnb_ref[:, s]
        o_ref[rs, s] = ((on + pr['bonus']) * pr['gate']).astype(BF16)

    @pl.when(ci == pl.num_programs(1) - 1)
    def _():
        sout_ref[0] = s_scr[...]


def _rwkv_mixer(hx, w, shift0, s0, wts, b, t, c, nsub):
    rows = c * nsub
    grows = c * min(nsub, 2)
    nc = t // rows
    nblk = b * nc
    d = hx.shape[1]
    kern = functools.partial(_rwkv_kernel, c=c, nsub=nsub)
    full = lambda a: pl.BlockSpec(a.shape, lambda i, j: (0,) * a.ndim)
    st_spec = pl.BlockSpec((1, RWKV_PAIRS, LANE, LANE), lambda i, j: (i, 0, 0, 0))
    sh_spec = pl.BlockSpec((1, 1, RWKV_COLS), lambda i, j: (i, 0, 0))
    return pl.pallas_call(
        kern,
        grid=(b, nc),
        in_specs=[pl.BlockSpec((rows, d), lambda i, j: (i * nc + j, 0)),
                  pl.BlockSpec((rows, d), lambda i, j: (jnp.minimum(i * nc + j + 1, nblk - 1), 0)),
                  pl.BlockSpec(w.shape, lambda i, j: (0, 0), pipeline_mode=pl.Buffered(1)),
                  sh_spec, st_spec] + [full(a) for a in wts],
        out_specs=[pl.BlockSpec((rows, RWKV_WIDTH), lambda i, j: (i * nc + j, 0)), sh_spec, st_spec],
        out_shape=[jax.ShapeDtypeStruct((b * t, RWKV_WIDTH), BF16),
                   jax.ShapeDtypeStruct((b, 1, RWKV_COLS), F32),
                   jax.ShapeDtypeStruct((b, RWKV_PAIRS, LANE, LANE), F32)],
        scratch_shapes=[pltpu.VMEM((RWKV_PAIRS, LANE, LANE), F32), pltpu.VMEM((1, RWKV_COLS), F32),
                        pltpu.VMEM((grows, RWKV_COLS), F32), pltpu.VMEM((grows, RWKV_COLS), F32)],
        compiler_params=_compiler_params(("arbitrary", "arbitrary")),
    )(hx, hx, w, shift0, s0, *wts)


def _rwkv_cols_in(a):
    return jnp.concatenate([a[..., 0:640], a[..., 704:1984], a[..., 640:704], a[..., 1984:2176]], axis=-1)


def _rwkv_cols_out(a):
    return jnp.concatenate([a[..., 0:640], a[..., 1920:1984], a[..., 640:1920], a[..., 1984:2176]], axis=-1)


def _pad_heads(w, heads, d, dp):
    lead = w.shape[:-1]
    w = w.reshape(lead + (heads, d))
    w = jnp.pad(w, [(0, 0)] * len(lead) + [(0, 0), (0, dp - d)])
    return w.reshape(lead + (heads * dp,))


def _prep_weights(w_in, gla_gate_up, gla_gate_bias, gla_norm, rwkv_mu, rwkv_w_up, rwkv_a_up,
                  rwkv_r_k, w_out):
    wb = w_in.astype(BF16)
    zeros = lambda n: jnp.zeros(wb.shape[:-1] + (n,), BF16)

    def padded_heads(start, d, dp):
        return [piece for h in range(GLA_HEADS)
                for piece in (wb[..., start + h * d:start + (h + 1) * d], zeros(dp - d))]

    w_gla = jnp.concatenate(
        padded_heads(0, GLA_DK, GLA_DKP) + padded_heads(384, GLA_DK, GLA_DKP)
        + padded_heads(768, GLA_DV, GLA_DVP) + padded_heads(1552, GLA_DV, GLA_DVP)
        + [wb[..., 1536:1552], zeros(GLA_RANKP - GLA_RANK)], axis=-1)
    w_rwkv = _rwkv_cols_in(wb[..., 2320:2320 + RWKV_COLS])
    w_hgrn = wb[..., 2320 + RWKV_COLS:]
    gup = jnp.pad(_pad_heads(gla_gate_up, GLA_HEADS, GLA_DK, GLA_DKP),
                  ((0, 0), (0, GLA_RANKP - GLA_RANK), (0, 0))).astype(BF16)
    gbias = _pad_heads(gla_gate_bias, GLA_HEADS, GLA_DK, GLA_DKP)[:, None, :]
    gnorm = jnp.pad(gla_norm, ((0, 0), (0, GLA_DVP - GLA_DV)))[:, None, :]
    mu = _rwkv_cols_in(rwkv_mu)[:, None, :]
    wup = jnp.pad(rwkv_w_up, ((0, 0), (0, 64), (0, 0))).astype(BF16)
    aup = jnp.pad(rwkv_a_up, ((0, 0), (64, 0), (0, 0))).astype(BF16)
    nl = w_in.shape[0]
    rk = rwkv_r_k.reshape(nl, 1, RWKV_WIDTH)
    wo_a = w_out[:, :768].reshape(nl, GLA_HEADS, GLA_DV, D_MODEL)
    wo_a = jnp.pad(wo_a, ((0, 0), (0, 0), (0, GLA_DVP - GLA_DV), (0, 0)))
    wo_a = wo_a.reshape(nl, GLA_VW, D_MODEL).astype(BF16)
    wo_b = w_out[:, 768:1408].astype(BF16)
    wo_c = w_out[:, 1408:].astype(BF16)
    return w_gla, w_rwkv, w_hgrn, gup, gbias, gnorm, mu, wup, aup, rk, wo_a, wo_b, wo_c


def _build_weights(norm_mix_pre, norm_mix_post, norm_ffn_pre, norm_ffn_post, w_in, gla_gate_up,
                   gla_gate_bias, gla_norm, rwkv_mu, rwkv_w0, rwkv_w_up, rwkv_a0, rwkv_a_up,
                   rwkv_g_up, rwkv_k_k, rwkv_k_a, rwkv_r_k, rwkv_ln_w, rwkv_ln_b,
                   hgrn_lb_logits, hgrn_norm, w_out, ffn_w_gate, ffn_w_up, ffn_w_down):
    names = ('w_gla', 'w_rwkv', 'w_hgrn', 'gup', 'gbias', 'gnorm', 'mu', 'wup', 'aup', 'rk',
             'wo_a', 'wo_b', 'wo_c')
    one = lambda a, l: a[l:l + 1]
    per_layer = [_prep_weights(one(w_in, l), one(gla_gate_up, l), one(gla_gate_bias, l),
                               one(gla_norm, l), one(rwkv_mu, l), one(rwkv_w_up, l),
                               one(rwkv_a_up, l), one(rwkv_r_k, l), one(w_out, l))
                 for l in range(DEPTH)]
    wt = {name: [per_layer[l][i][0] for l in range(DEPTH)] for i, name in enumerate(names)}
    vec = lambda a: a[:, None, :]
    wt.update(
        norm_mix_pre=vec(norm_mix_pre), norm_mix_post=vec(norm_mix_post),
        norm_ffn_pre=vec(norm_ffn_pre), norm_ffn_post=vec(norm_ffn_post),
        w0=vec(rwkv_w0), a0=vec(rwkv_a0), g_up=rwkv_g_up.astype(BF16),
        k_k=vec(rwkv_k_k), k_a=vec(rwkv_k_a), ln_w=vec(rwkv_ln_w), ln_b=vec(rwkv_ln_b),
        lb_logits=hgrn_lb_logits, hnorm=vec(hgrn_norm),
        ffn_w_gate=[ffn_w_gate[l].astype(BF16) for l in range(DEPTH)],
        ffn_w_up=[ffn_w_up[l].astype(BF16) for l in range(DEPTH)],
        ffn_w_down=[ffn_w_down[l].astype(BF16) for l in range(DEPTH)])
    return wt


def _gla_state_in(s):
    s = jnp.swapaxes(s, -1, -2)
    return jnp.pad(s, [(0, 0)] * 3 + [(0, GLA_DVP - GLA_DV), (0, GLA_DKP - GLA_DK)])


def _gla_state_out(s):
    return jnp.swapaxes(s[..., :GLA_DV, :GLA_DK], -1, -2)


def _rwkv_state_in(s):
    d, b = s.shape[:2]
    s = s.reshape(d, b, RWKV_PAIRS, 2, RWKV_HD, RWKV_HD)
    z = jnp.zeros_like(s[:, :, :, 0])
    top = jnp.concatenate([s[:, :, :, 0], z], axis=-1)
    bot = jnp.concatenate([z, s[:, :, :, 1]], axis=-1)
    return jnp.concatenate([top, bot], axis=-2)


def _rwkv_state_out(s):
    d, b = s.shape[:2]
    out = jnp.stack([s[..., :RWKV_HD, :RWKV_HD], s[..., RWKV_HD:, RWKV_HD:]], axis=3)
    return out.reshape(d, b, RWKV_HEADS, RWKV_HD, RWKV_HD)


def _row_tile(n, cap):
    t = cap
    while n % t:
        t //= 2
    return t


def _trunk(x, s_gla, s_rwkv, s_shift, s_hgrn, wt, b, t):
    n = b * t
    tm = _row_tile(n, 512)
    c_glr = min(128, t)
    c_rwkv = min(64, t)
    n_rwkv = 4 if t % (4 * c_rwkv) == 0 else 1
    new_gla, new_rwkv, new_shift, new_hgrn = [], [], [], []
    h = _norm(x, wt['norm_mix_pre'][0], tm)
    for l in range(DEPTH):
        oa, g1 = _gla_mixer(h, wt['w_gla'][l], s_gla[l], wt['gup'][l], wt['gbias'][l],
                            wt['gnorm'][l], b, t, c_glr)
        rw = [wt[k][l] for k in ('mu', 'w0', 'wup', 'a0', 'aup', 'g_up', 'k_k', 'k_a', 'rk',
                                 'ln_w', 'ln_b')]
        ob, sh1, r1 = _rwkv_mixer(h, wt['w_rwkv'][l], s_shift[l], s_rwkv[l], rw, b, t, c_rwkv, n_rwkv)
        oc, h1 = _hgrn_mixer(h, wt['w_hgrn'][l], s_hgrn[l], wt['lb_logits'], wt['hnorm'][l],
                             b, t, c_glr, l)
        x, h = _out_proj(oa, ob, oc, wt['wo_a'][l], wt['wo_b'][l], wt['wo_c'][l], x,
                         wt['norm_mix_post'][l], wt['norm_ffn_pre'][l], tm)
        x, h = _ffn(h, x, wt['ffn_w_gate'][l], wt['ffn_w_up'][l], wt['ffn_w_down'][l],
                    wt['norm_ffn_post'][l], wt['norm_mix_pre'][(l + 1) % DEPTH], tm)
        new_gla.append(g1)
        new_rwkv.append(r1)
        new_shift.append(sh1)
        new_hgrn.append(h1)
    return x, jnp.stack(new_gla), jnp.stack(new_rwkv), jnp.stack(new_shift), jnp.stack(new_hgrn)


def _run_path(x, state_gla, state_rwkv, state_shift, state_hgrn, wt):
    b, t, d = x.shape
    y, g, r, sh, h = _trunk(
        x.reshape(b * t, d), _gla_state_in(state_gla), _rwkv_state_in(state_rwkv),
        _rwkv_cols_in(state_shift)[:, :, None, :], jnp.swapaxes(state_hgrn, -1, -2), wt, b, t)
    return (y.reshape(b, t, d), _gla_state_out(g), _rwkv_state_out(r),
            _rwkv_cols_out(sh[:, :, 0, :]), jnp.swapaxes(h, -1, -2))


def kernel(x_prompt, x_sample, state_gla, state_rwkv, state_rwkv_shift, state_hgrn, norm_mix_pre, norm_mix_post, norm_ffn_pre, norm_ffn_post, w_in, gla_gate_up, gla_gate_bias, gla_norm, rwkv_mu, rwkv_w0, rwkv_w_up, rwkv_a0, rwkv_a_up, rwkv_g_up, rwkv_k_k, rwkv_k_a, rwkv_r_k, rwkv_ln_w, rwkv_ln_b, hgrn_lb_logits, hgrn_norm, w_out, ffn_w_gate, ffn_w_up, ffn_w_down):
    wt = _build_weights(norm_mix_pre, norm_mix_post, norm_ffn_pre, norm_ffn_post, w_in, gla_gate_up,
                        gla_gate_bias, gla_norm, rwkv_mu, rwkv_w0, rwkv_w_up, rwkv_a0, rwkv_a_up,
                        rwkv_g_up, rwkv_k_k, rwkv_k_a, rwkv_r_k, rwkv_ln_w, rwkv_ln_b,
                        hgrn_lb_logits, hgrn_norm, w_out, ffn_w_gate, ffn_w_up, ffn_w_down)
    bp = x_prompt.shape[0]
    zeros = lambda s: jnp.zeros((DEPTH, bp) + s.shape[2:], x_prompt.dtype)
    y_p, gla_p, rwkv_p, shift_p, hgrn_p = _run_path(
        x_prompt, zeros(state_gla), zeros(state_rwkv), zeros(state_rwkv_shift), zeros(state_hgrn), wt)
    y_s, gla_s, rwkv_s, shift_s, hgrn_s = _run_path(
        x_sample, state_gla, state_rwkv, state_rwkv_shift, state_hgrn, wt)
    return (y_p, y_s, gla_p, rwkv_p, shift_p, hgrn_p, gla_s, rwkv_s, shift_s, hgrn_s)
```

```python
import functools

import jax
import jax.numpy as jnp
from jax import lax
from jax.experimental import pallas as pl
from jax.experimental.pallas import tpu as pltpu

F32 = jnp.float32
BF16 = jnp.bfloat16

D_MODEL = 2048
DEPTH = 4
EPS = 1e-6

GLA_HEADS, GLA_DK, GLA_DV = 4, 96, 192
GLA_DKP, GLA_DVP = 128, 256
GLA_RANK, GLA_RANKP = 16, 128
GLA_GATE_NORMALIZER = 16.0
GLA_QW = GLA_HEADS * GLA_DKP
GLA_VW = GLA_HEADS * GLA_DVP
GLA_COLS_P = 2 * GLA_QW + 2 * GLA_VW + GLA_RANKP

RWKV_HEADS, RWKV_HD = 10, 64
RWKV_WIDTH = RWKV_HEADS * RWKV_HD
RWKV_PAIRS = RWKV_HEADS // 2
RWKV_COLS = 2176
RWKV_GN_EPS = 64e-5

HGRN_HEADS, HGRN_D = 5, 128
HGRN_WIDTH = HGRN_HEADS * HGRN_D
HGRN_COLS = 4 * HGRN_WIDTH

D_FF = 5632
FFN_TF = 512
LANE = 128
LOG2_E = 1.4426950408889634

VMEM_LIMIT = 56 * 1024 * 1024


def _sigmoid(x):
    return 1.0 / (1.0 + jnp.exp(-x))


def _silu(x):
    return x * _sigmoid(x)


def _log_sigmoid(x):
    return jnp.minimum(x, 0.0) - jnp.log(1.0 + jnp.exp(-jnp.abs(x)))


def _dot(a, b):
    return jnp.dot(a.astype(BF16), b.astype(BF16), preferred_element_type=F32)


def _dot_nt(a, b):
    return lax.dot_general(a.astype(BF16), b.astype(BF16), (((1,), (1,)), ((), ())),
                           preferred_element_type=F32)


def _dot_tn(a, b):
    return lax.dot_general(a.astype(BF16), b.astype(BF16), (((0,), (0,)), ((), ())),
                           preferred_element_type=F32)


def _split_bf16(x, parts):
    out = []
    r = x
    for i in range(parts):
        p = r.astype(BF16)
        out.append(p)
        if i + 1 < parts:
            r = r - p.astype(F32)
    return out


def _dot_exact_lhs(m_bf16, x, parts=3):
    acc = None
    for p in _split_bf16(x, parts):
        t = jnp.dot(m_bf16, p, preferred_element_type=F32)
        acc = t if acc is None else acc + t
    return acc


def _iota(shape, dim):
    return lax.broadcasted_iota(jnp.int32, shape, dim)


def _tril_ones(c):
    return jnp.where(_iota((c, c), 0) >= _iota((c, c), 1), 1.0, 0.0).astype(BF16)


def _rms_rows(x, g):
    ms = jnp.mean(x * x, axis=-1, keepdims=True)
    return x * lax.rsqrt(ms + EPS) * g


def _compiler_params(sem, flags=None):
    return pltpu.CompilerParams(dimension_semantics=sem, vmem_limit_bytes=VMEM_LIMIT, flags=flags)


def _norm_kernel(x_ref, g_ref, h_ref):
    h_ref[...] = _rms_rows(x_ref[...], g_ref[...]).astype(BF16)


def _norm(x, g, tm):
    n, d = x.shape
    return pl.pallas_call(
        _norm_kernel,
        grid=(n // tm,),
        in_specs=[pl.BlockSpec((tm, d), lambda i: (i, 0)), pl.BlockSpec((1, d), lambda i: (0, 0))],
        out_specs=pl.BlockSpec((tm, d), lambda i: (i, 0)),
        out_shape=jax.ShapeDtypeStruct((n, d), BF16),
        compiler_params=_compiler_params(("arbitrary",)),
    )(x, g)


def _out_proj_kernel(oa_ref, ob_ref, oc_ref, wa_ref, wb_ref, wc_ref, x_ref, g_ref, gnext_ref,
                     y_ref, hn_ref):
    half = x_ref.shape[0] // 2
    for r in (slice(0, half), slice(half, 2 * half)):
        m = (jnp.dot(oa_ref[r, :], wa_ref[...], preferred_element_type=F32)
             + jnp.dot(ob_ref[r, :], wb_ref[...], preferred_element_type=F32)
             + jnp.dot(oc_ref[r, :], wc_ref[...], preferred_element_type=F32))
        y = x_ref[r, :] + _rms_rows(m, g_ref[...])
        y_ref[r, :] = y
        hn_ref[r, :] = _rms_rows(y, gnext_ref[...]).astype(BF16)


def _out_proj(oa, ob, oc, wa, wb, wc, x, g, gnext, tm):
    n, d = x.shape
    row = lambda w: pl.BlockSpec((tm, w), lambda i: (i, 0))
    full = lambda a: pl.BlockSpec(a.shape, lambda i: (0, 0))
    return pl.pallas_call(
        _out_proj_kernel,
        grid=(n // tm,),
        in_specs=[row(oa.shape[1]), row(ob.shape[1]), row(oc.shape[1]),
                  full(wa), full(wb), full(wc), row(d), full(g), full(gnext)],
        out_specs=[row(d), row(d)],
        out_shape=[jax.ShapeDtypeStruct((n, d), F32), jax.ShapeDtypeStruct((n, d), BF16)],
        compiler_params=_compiler_params(("arbitrary",)),
    )(oa, ob, oc, wa, wb, wc, x, g, gnext)


def _ffn_kernel(h_ref, x_ref, wg_ref, wu_ref, wd_ref, gpost_ref, gnext_ref, y_ref, hn_ref, acc_scr):
    j = pl.program_id(1)

    @pl.when(j == 0)
    def _():
        acc_scr[...] = jnp.zeros_like(acc_scr)

    h = h_ref[...]
    a = jnp.dot(h, wg_ref[...], preferred_element_type=F32)
    u = jnp.dot(h, wu_ref[...], preferred_element_type=F32)
    acc_scr[...] += jnp.dot((_silu(a) * u).astype(BF16), wd_ref[...], preferred_element_type=F32)

    @pl.when(j == pl.num_programs(1) - 1)
    def _():
        y = x_ref[...] + _rms_rows(acc_scr[...], gpost_ref[...])
        y_ref[...] = y
        hn_ref[...] = _rms_rows(y, gnext_ref[...]).astype(BF16)


def _ffn(h, x, wg, wu, wd, layer, gpost, gnext, tm):
    n, d = x.shape
    dff = wg.shape[2]
    tf = FFN_TF
    return pl.pallas_call(
        _ffn_kernel,
        grid=(n // tm, dff // tf),
        in_specs=[pl.BlockSpec((tm, d), lambda i, j: (i, 0)),
                  pl.BlockSpec((tm, d), lambda i, j: (i, 0)),
                  pl.BlockSpec((None, d, tf), lambda i, j: (layer, 0, j)),
                  pl.BlockSpec((None, d, tf), lambda i, j: (layer, 0, j)),
                  pl.BlockSpec((None, tf, d), lambda i, j: (layer, j, 0)),
                  pl.BlockSpec((1, d), lambda i, j: (0, 0)),
                  pl.BlockSpec((1, d), lambda i, j: (0, 0))],
        out_specs=[pl.BlockSpec((tm, d), lambda i, j: (i, 0)),
                   pl.BlockSpec((tm, d), lambda i, j: (i, 0))],
        out_shape=[jax.ShapeDtypeStruct((n, d), F32), jax.ShapeDtypeStruct((n, d), BF16)],
        scratch_shapes=[pltpu.VMEM((tm, d), F32)],
        compiler_params=_compiler_params(("arbitrary", "arbitrary")),
    )(h, x, wg, wu, wd, gpost, gnext)


class _GlrMasks:
    def __init__(self, c):
        self.c = c
        row_a = _iota((c, c), 0)
        col_a = _iota((c, c), 1)
        self.tril = _tril_ones(c)
        self.level = {}
        h = 1
        while 2 * h <= c:
            self.level[h] = (((row_a // (2 * h)) == (col_a // (2 * h)))
                             & ((row_a // h) % 2 == 1) & ((col_a // h) % 2 == 0))
            h *= 2
        self.eye = row_a == col_a


def _small_level_factors(g):
    c = g.shape[0]
    r4 = _iota(g.shape, 0) % 4
    g_prev = pltpu.roll(g, 1, 0)
    g_next = pltpu.roll(g, c - 1, 0)
    e1 = jnp.exp2(jnp.where(r4 % 2 == 1, g, 0.0))
    e2 = jnp.exp2(jnp.where(r4 == 3, g + g_prev, jnp.where(r4 == 2, g, jnp.where(r4 == 0, g_next, 0.0))))
    return e1, e2


def _glr_chunk(q, k, v, b, e1, e2, st, mk):
    c, dk = q.shape
    att = jnp.where(mk.eye, _dot_nt(q, k), 0.0)
    for h, pair_mask in mk.level.items():
        if h == 1:
            e = e1
        elif h == 2:
            e = e2
        else:
            n = c // (2 * h)
            ref = b.reshape(n, 2 * h, dk)[:, h - 1:h, :]
            refb = jnp.broadcast_to(ref, (n, 2 * h, dk)).reshape(c, dk)
            e = jnp.exp2(-jnp.abs(b - refb))
        att = jnp.where(pair_mask, _dot_nt(q * e, k * e), att)
    o = _dot(att, v) + _dot_nt(q * jnp.exp2(b), st)
    b_end = b[c - 1:c, :]
    st_new = st * jnp.exp2(b_end) + _dot_tn(v, k * jnp.exp2(b_end - b))
    return o, st_new


def _ahead_groups(cols, slots):
    tiles = -(-cols // (2 * LANE))
    bounds = [min(cols, 2 * LANE * ((k * tiles) // slots)) for k in range(slots)] + [cols]
    return [slice(bounds[k], bounds[k + 1]) for k in range(slots)]


def _gla_kernel(hc_ref, hn_ref, w_ref, s0_ref, gup_ref, gbias_ref, gnorm_ref, o_ref, sout_ref,
                s_scr, pa_scr, pb_scr, *, c, nsub):
    ci = pl.program_id(1)
    half = max(nsub // 2, 1)
    hrows = c * half
    groups = _ahead_groups(GLA_COLS_P, GLA_HEADS * half)

    @pl.when(ci == 0)
    def _():
        s_scr[...] = s0_ref[0]

    @pl.when(jnp.logical_or(ci == 0, nsub == 1))
    def _():
        pa_scr[...] = jnp.dot(hc_ref[0:hrows, :], w_ref[...], preferred_element_type=F32)

    mk = _GlrMasks(c)
    gnorm = gnorm_ref[...]
    for i in range(nsub):
        rs = slice(i * c, (i + 1) * c)
        src = pa_scr if i < half else pb_scr
        p = src[(i % half) * c:(i % half + 1) * c, :]
        q_all = p[:, 0:GLA_QW] * (GLA_DK ** -0.5)
        k_all = p[:, GLA_QW:2 * GLA_QW]
        v_all = p[:, 2 * GLA_QW:2 * GLA_QW + GLA_VW]
        gout = p[:, 2 * GLA_QW + GLA_VW:2 * GLA_QW + 2 * GLA_VW]
        gdown = p[:, 2 * GLA_QW + 2 * GLA_VW:]
        g_all = (_log_sigmoid(_dot(gdown, gup_ref[...]) + gbias_ref[...])
                 * (LOG2_E / GLA_GATE_NORMALIZER))
        b_all = _dot_exact_lhs(mk.tril, g_all)
        e1_all, e2_all = _small_level_factors(g_all)
        for h in range(GLA_HEADS):
            ks = slice(h * GLA_DKP, (h + 1) * GLA_DKP)
            vs = slice(h * GLA_DVP, (h + 1) * GLA_DVP)
            o, st = _glr_chunk(q_all[:, ks], k_all[:, ks], v_all[:, vs], b_all[:, ks],
                               e1_all[:, ks], e2_all[:, ks], s_scr[h], mk)
            s_scr[h] = st
            ms = jnp.sum(o * o, axis=-1, keepdims=True) * (1.0 / GLA_DV)
            o = o * lax.rsqrt(ms + EPS) * gnorm * _silu(gout[:, vs])
            o_ref[rs, vs] = o.astype(BF16)
            if nsub > 1:
                g = groups[(i % half) * GLA_HEADS + h]
                ahead, dst = (hc_ref[hrows:, :], pb_scr) if i < half else (hn_ref[0:hrows, :], pa_scr)
                dst[:, g] = jnp.dot(ahead, w_ref[:, g], preferred_element_type=F32)

    @pl.when(ci == pl.num_programs(1) - 1)
    def _():
        sout_ref[0] = s_scr[...]


def _gla_mixer(hx, w, s0t, gup, gbias, gnorm, b, t, c):
    nsub = 4 if t % (4 * c) == 0 else 1
    rows = c * nsub
    hrows = c * max(nsub // 2, 1)
    nc = t // rows
    nblk = b * nc
    kern = functools.partial(_gla_kernel, c=c, nsub=nsub)
    full = lambda a: pl.BlockSpec(a.shape, lambda i, j: (0,) * a.ndim)
    st_spec = pl.BlockSpec((1, GLA_HEADS, GLA_DVP, GLA_DKP), lambda i, j: (i, 0, 0, 0))
    d = hx.shape[1]
    return pl.pallas_call(
        kern,
        grid=(b, nc),
        in_specs=[pl.BlockSpec((rows, d), lambda i, j: (i * nc + j, 0)),
                  pl.BlockSpec((rows, d), lambda i, j: (jnp.minimum(i * nc + j + 1, nblk - 1), 0)),
                  pl.BlockSpec(w.shape, lambda i, j: (0, 0), pipeline_mode=pl.Buffered(1)),
                  st_spec, full(gup), full(gbias), full(gnorm)],
        out_specs=[pl.BlockSpec((rows, GLA_VW), lambda i, j: (i * nc + j, 0)), st_spec],
        out_shape=[jax.ShapeDtypeStruct((b * t, GLA_VW), BF16),
                   jax.ShapeDtypeStruct((b, GLA_HEADS, GLA_DVP, GLA_DKP), F32)],
        scratch_shapes=[pltpu.VMEM((GLA_HEADS, GLA_DVP, GLA_DKP), F32),
                        pltpu.VMEM((hrows, GLA_COLS_P), F32), pltpu.VMEM((hrows, GLA_COLS_P), F32)],
        compiler_params=_compiler_params(("arbitrary", "arbitrary")),
    )(hx, hx, w, s0t, gup, gbias, gnorm)


def _hgrn_kernel(hc_ref, hn_ref, w_ref, s0_ref, lbl_ref, hnorm_ref, o_ref, sout_ref,
                 s_scr, pa_scr, pb_scr, *, c, nsub, layer):
    ci = pl.program_id(1)
    half = max(nsub // 2, 1)
    hrows = c * half
    gw = HGRN_COLS // (HGRN_HEADS * half)

    @pl.when(ci == 0)
    def _():
        s_scr[...] = s0_ref[0]

    @pl.when(jnp.logical_or(ci == 0, nsub == 1))
    def _():
        pa_scr[...] = jnp.dot(hc_ref[0:hrows, :], w_ref[...], preferred_element_type=F32)

    lg = lbl_ref[...]
    e = jnp.exp(lg - jnp.max(lg, axis=0, keepdims=True))
    prob = e / jnp.sum(e, axis=0, keepdims=True)
    lb = jnp.zeros((1, HGRN_WIDTH), F32)
    for i in range(1, layer + 1):
        lb = lb + prob[i:i + 1, :]

    mk = _GlrMasks(c)
    hnorm = hnorm_ref[...]

    for i in range(nsub):
        rs = slice(i * c, (i + 1) * c)
        src = pa_scr if i < half else pb_scr
        p = src[(i % half) * c:(i % half + 1) * c, :]
        hq = p[:, 0:HGRN_WIDTH]
        hf = p[:, HGRN_WIDTH:2 * HGRN_WIDTH]
        hi = p[:, 2 * HGRN_WIDTH:3 * HGRN_WIDTH]
        hg = p[:, 3 * HGRN_WIDTH:]
        q_all = _silu(hq)
        f_all = lb + (1.0 - lb) * _sigmoid(hf)
        k_all = 1.0 - f_all
        g_all = jnp.log2(f_all)
        b_all = _dot_exact_lhs(mk.tril, g_all)
        e1_all, e2_all = _small_level_factors(g_all)
        for h in range(HGRN_HEADS):
            s = slice(h * HGRN_D, (h + 1) * HGRN_D)
            o, st = _glr_chunk(q_all[:, s], k_all[:, s], hi[:, s], b_all[:, s],
                               e1_all[:, s], e2_all[:, s], s_scr[h], mk)
            s_scr[h] = st
            o = _rms_rows(o, hnorm) * _silu(hg[:, s])
            o_ref[rs, s] = o.astype(BF16)
            if nsub > 1:
                k = (i % half) * HGRN_HEADS + h
                g = slice(k * gw, (k + 1) * gw)
                ahead, dst = (hc_ref[hrows:, :], pb_scr) if i < half else (hn_ref[0:hrows, :], pa_scr)
                dst[:, g] = jnp.dot(ahead, w_ref[:, g], preferred_element_type=F32)

    @pl.when(ci == pl.num_programs(1) - 1)
    def _():
        sout_ref[0] = s_scr[...]


def _hgrn_mixer(hx, w, s0t, lb_logits, hnorm, b, t, c, layer):
    nsub = 4 if t % (4 * c) == 0 else 1
    rows = c * nsub
    hrows = c * max(nsub // 2, 1)
    nc = t // rows
    nblk = b * nc
    kern = functools.partial(_hgrn_kernel, c=c, nsub=nsub, layer=layer)
    full = lambda a: pl.BlockSpec(a.shape, lambda i, j: (0,) * a.ndim)
    st_spec = pl.BlockSpec((1, HGRN_HEADS, HGRN_D, HGRN_D), lambda i, j: (i, 0, 0, 0))
    d = hx.shape[1]
    return pl.pallas_call(
        kern,
        grid=(b, nc),
        in_specs=[pl.BlockSpec((rows, d), lambda i, j: (i * nc + j, 0)),
                  pl.BlockSpec((rows, d), lambda i, j: (jnp.minimum(i * nc + j + 1, nblk - 1), 0)),
                  pl.BlockSpec(w.shape, lambda i, j: (0, 0), pipeline_mode=pl.Buffered(1)),
                  st_spec, full(lb_logits), full(hnorm)],
        out_specs=[pl.BlockSpec((rows, HGRN_WIDTH), lambda i, j: (i * nc + j, 0)), st_spec],
        out_shape=[jax.ShapeDtypeStruct((b * t, HGRN_WIDTH), BF16),
                   jax.ShapeDtypeStruct((b, HGRN_HEADS, HGRN_D, HGRN_D), F32)],
        scratch_shapes=[pltpu.VMEM((HGRN_HEADS, HGRN_D, HGRN_D), F32),
                        pltpu.VMEM((hrows, HGRN_COLS), F32), pltpu.VMEM((hrows, HGRN_COLS), F32)],
        compiler_params=_compiler_params(("arbitrary", "arbitrary")),
    )(hx, hx, w, s0t, lb_logits, hnorm)


def _rwkv_kernel(hc_ref, hn_ref, w_ref, shift0_ref, s0_ref, mu_ref, w0_ref, wup_ref, a0_ref,
                 aup_ref, gup_ref, kk_ref, ka_ref, rk_ref, lnw_ref, lnb_ref,
                 o_ref, shift_ref, sout_ref, s_scr, carry_scr, pa_scr, pb_scr, *, c, nsub):
    ci = pl.program_id(1)
    w = RWKV_WIDTH
    rows = c * nsub
    grp = min(nsub, 2)
    grows = c * grp

    @pl.when(ci == 0)
    def _():
        s_scr[...] = s0_ref[0]
        carry_scr[...] = shift0_ref[0]

    @pl.when(jnp.logical_or(ci == 0, nsub == 1))
    def _():
        pa_scr[...] = jnp.dot(hc_ref[0:grows, :], w_ref[...], preferred_element_type=F32)

    first_prev = carry_scr[...]
    grp_count = rows // grows
    col_tiles = _ahead_groups(RWKV_COLS, -(-RWKV_COLS // (2 * LANE)))

    c2 = 2 * c
    lane = _iota((c, LANE), 1)
    lo_half = lane < RWKV_HD
    ri = _iota((c2, c2), 0)
    cj = _iota((c2, c2), 1)
    strict = (ri % c) > (cj % c)
    incl = (ri % c) >= (cj % c)
    eye = jnp.where(ri == cj, 1.0, 0.0)
    inv_levels = []
    blk = 1
    while blk < c:
        inv_levels.append(((ri // (2 * blk)) == (cj // (2 * blk)))
                          & ((ri // blk) % 2 == 1) & ((cj // blk) % 2 == 0))
        blk *= 2
    tr, tc = _iota((grows, grows), 0), _iota((grows, grows), 1)
    tril = jnp.where((tr >= tc) & (tr // c == tc // c), 1.0, 0.0).astype(BF16)
    row = _iota((grows, RWKV_COLS), 0)
    seg = jnp.where((_iota((LANE, LANE), 0) // RWKV_HD) == (_iota((LANE, LANE), 1) // RWKV_HD),
                    1.0, 0.0).astype(BF16)

    def stack(x):
        return jnp.concatenate([jnp.where(lo_half, x, 0.0), jnp.where(lo_half, 0.0, x)], axis=0)

    def head_sum(x):
        return jnp.concatenate([_dot(x[:, j * LANE:(j + 1) * LANE], seg)
                                for j in range(RWKV_PAIRS)], axis=1)

    def prepare(gi):
        g0 = gi * grows
        src = pb_scr if gi else pa_scr
        p = src[...]
        before = pa_scr[grows - 1:grows, :] if gi else first_prev
        if nsub == 1:
            todo = iter(())
        elif gi == 0:
            todo = iter([(hc_ref, slice(grows, 2 * grows), pb_scr, g) for g in col_tiles])
        else:
            todo = iter([(hn_ref, slice(0, grows), pa_scr, g) for g in col_tiles])

        def project_one():
            t = next(todo, None)
            if t is not None:
                h_ref, h_rows, dst, g = t
                dst[:, g] = jnp.dot(h_ref[h_rows, :], w_ref[:, g], preferred_element_type=F32)

        prev = jnp.where(row == 0, before, pltpu.roll(p, 1, 0))
        pm = p + (prev - p) * mu_ref[...]
        project_one()
        r_all = pm[:, 0:w]
        k_all = pm[:, w:2 * w]
        v_all = pm[:, 2 * w:3 * w]
        wa = pm[:, 3 * w:3 * w + LANE]
        gd = pm[:, 3 * w + LANE:]
        lw_all = -jnp.exp(_log_sigmoid(w0_ref[...] + _dot(jnp.tanh(wa), wup_ref[...])) - 0.5)
        a_all = _sigmoid(a0_ref[...] + _dot(wa, aup_ref[...]))
        project_one()
        gate_all = _dot(_sigmoid(gd), gup_ref[...])
        kk_all = k_all * kk_ref[...]
        kk_all = kk_all * lax.rsqrt(jnp.maximum(head_sum(kk_all * kk_all), 1e-24))
        project_one()
        k_all = k_all * (1.0 + (a_all - 1.0) * ka_ref[...])
        beta_all = kk_all * a_all
        bonus_all = head_sum(r_all * k_all * rk_ref[...]) * v_all
        project_one()
        cin_all = _dot_exact_lhs(tril, lw_all, parts=2)
        ginv = jnp.exp(-cin_all)
        kap_all = kk_all * jnp.exp(cin_all - lw_all)
        project_one()
        rt_all = r_all * jnp.exp(cin_all)
        kt_all = k_all * ginv
        bt_all = beta_all * ginv
        out = []
        for i in range(grp):
            rs = slice(i * c, (i + 1) * c)
            cend = cin_all[(i + 1) * c - 1:(i + 1) * c, :]
            gend = jnp.exp(cend - cin_all[rs])
            kend, bend, dec = k_all[rs] * gend, beta_all[rs] * gend, jnp.exp(cend)
            for j in range(RWKV_PAIRS):
                s = slice(j * LANE, (j + 1) * LANE)
                out.append(dict(
                    s=s, rs=slice(g0 + i * c, g0 + (i + 1) * c),
                    kap_s=stack(kap_all[rs, s]), r_s=stack(rt_all[rs, s]), k_s=stack(kt_all[rs, s]),
                    b_s=stack(bt_all[rs, s]), v_s=stack(v_all[rs, s]), kend_s=stack(kend[:, s]),
                    bend_s=stack(bend[:, s]), dec=dec[:, s], bonus=bonus_all[rs, s],
                    gate=gate_all[rs, s]))
                if j % 2 == 0:
                    project_one()
        for _ in col_tiles:
            project_one()
        if gi == grp_count - 1:
            last = src[grows - 1:grows, :]
            carry_scr[...] = last
            shift_ref[0] = last
        return out

    def dot_nt2(lhs, r1, r2):
        if r1.shape[0] % LANE:
            return _dot_nt(lhs, r1), _dot_nt(lhs, r2)
        both = _dot_nt(lhs, jnp.concatenate([r1, r2], axis=0))
        return both[:, :r1.shape[0]], both[:, r1.shape[0]:]

    def dot2(lhs, r1, r2):
        both = _dot(lhs, jnp.concatenate([r1, r2], axis=1))
        return both[:, :r1.shape[1]], both[:, r1.shape[1]:]

    pairs = []
    for gi in range(grp_count):
        group = prepare(gi)
        for pr in group:
            a1, a2 = dot_nt2(pr['kap_s'], pr['k_s'], pr['b_s'])
            pr['a1'], pr['a2'] = jnp.where(strict, a1, 0.0), jnp.where(strict, a2, 0.0)
        for pr in group:
            a3, a4 = dot_nt2(pr['r_s'], pr['k_s'], pr['b_s'])
            pr['a3'], pr['a4'] = jnp.where(incl, a3, 0.0), jnp.where(incl, a4, 0.0)
        for pr in group:
            pr['a1v'] = _dot(pr['a1'], pr['v_s'])
            pr['a3v'] = _dot(pr['a3'], pr['v_s'])
            pr['kv'] = _dot_tn(pr['v_s'], pr['kend_s'])
        pairs += group
    if c2 % LANE == 0 and len(pairs) % 2 == 0:
        left = _iota((c2, 2 * c2), 1) < c2
        tile2 = lambda x: jnp.concatenate([x, x], axis=1)

        def block_diag(x, keep=None):
            top = left if keep is None else keep & left
            bot = ~left if keep is None else keep & ~left
            x = x.astype(BF16)
            return jnp.concatenate([jnp.where(top, x, 0.0), jnp.where(bot, x, 0.0)], axis=0)

        a2w = [jnp.concatenate([pa['a2'], pb['a2']], axis=1) for pa, pb in zip(pairs[0::2], pairs[1::2])]
        tw = [tile2(eye) - jnp.where(tile2(inv_levels[0]), a, 0.0) for a in a2w]
        for m in inv_levels[1:]:
            mw = tile2(m)
            half = [_dot(t, block_diag(a, mw)) for t, a in zip(tw, a2w)]
            tw = [t - _dot(hf, block_diag(t)) for t, hf in zip(tw, half)]
        tinv = [x for t in tw for x in (t[:, :c2], t[:, c2:])]
    else:
        tinv = [eye - jnp.where(inv_levels[0], pr['a2'], 0.0) for pr in pairs]
        for m in inv_levels[1:]:
            half = [_dot(t, jnp.where(m, pr['a2'], 0.0)) for t, pr in zip(tinv, pairs)]
            tinv = [t - _dot(hf, t) for t, hf in zip(tinv, half)]
    for pr, tj in zip(pairs, tinv):
        pr['tk'], pr['tv'] = dot2(tj, pr['kap_s'], pr['a1v'])
    for pr in pairs:
        a4tk, a4tv = dot2(pr['a4'], pr['tk'], pr['tv'])
        pr['reff'] = pr['r_s'] - a4tk
        pr['oc'] = pr['a3v'] - a4tv
        pr['m'] = _dot_tn(pr['tk'], pr['bend_s'])
        pr['q'] = pr['kv'] - _dot_tn(pr['tv'], pr['bend_s'])
    sts = [s_scr[j] for j in range(RWKV_PAIRS)]
    os_ = []
    for i in range(nsub):
        sub = list(zip(pairs[i * RWKV_PAIRS:(i + 1) * RWKV_PAIRS], sts))
        o2s = [_dot_nt(pr['reff'], st) + pr['oc'] for pr, st in sub]
        sts = [st * pr['dec'] - _dot(st, pr['m']) + pr['q'] for pr, st in sub]
        os_ += [o2[0:c] + o2[c:c2] for o2 in o2s]
    for j, st in enumerate(sts):
        s_scr[j] = st
    ds = [o - _dot(o, seg) * (1.0 / RWKV_HD) for o in os_]
    vars_ = [_dot(d * d, seg) * (1.0 / RWKV_HD) for d in ds]
    for pr, d, var in zip(pairs, ds, vars_):
        s, rs = pr['s'], pr['rs']
        on = d * lax.rsqrt(var + RWKV_GN_EPS) * lnw_ref[:, s] + lnb_ref[:, s]
        o_ref[rs, s] = ((on + pr['bonus']) * pr['gate']).astype(BF16)

    @pl.when(ci == pl.num_programs(1) - 1)
    def _():
        sout_ref[0] = s_scr[...]


def _rwkv_mixer(hx, w, shift0, s0, wts, b, t, c, nsub):
    rows = c * nsub
    grows = c * min(nsub, 2)
    nc = t // rows
    nblk = b * nc
    d = hx.shape[1]
    kern = functools.partial(_rwkv_kernel, c=c, nsub=nsub)
    full = lambda a: pl.BlockSpec(a.shape, lambda i, j: (0,) * a.ndim)
    st_spec = pl.BlockSpec((1, RWKV_PAIRS, LANE, LANE), lambda i, j: (i, 0, 0, 0))
    sh_spec = pl.BlockSpec((1, 1, RWKV_COLS), lambda i, j: (i, 0, 0))
    return pl.pallas_call(
        kern,
        grid=(b, nc),
        in_specs=[pl.BlockSpec((rows, d), lambda i, j: (i * nc + j, 0)),
                  pl.BlockSpec((rows, d), lambda i, j: (jnp.minimum(i * nc + j + 1, nblk - 1), 0)),
                  pl.BlockSpec(w.shape, lambda i, j: (0, 0), pipeline_mode=pl.Buffered(1)),
                  sh_spec, st_spec] + [full(a) for a in wts],
        out_specs=[pl.BlockSpec((rows, RWKV_WIDTH), lambda i, j: (i * nc + j, 0)), sh_spec, st_spec],
        out_shape=[jax.ShapeDtypeStruct((b * t, RWKV_WIDTH), BF16),
                   jax.ShapeDtypeStruct((b, 1, RWKV_COLS), F32),
                   jax.ShapeDtypeStruct((b, RWKV_PAIRS, LANE, LANE), F32)],
        scratch_shapes=[pltpu.VMEM((RWKV_PAIRS, LANE, LANE), F32), pltpu.VMEM((1, RWKV_COLS), F32),
                        pltpu.VMEM((grows, RWKV_COLS), F32), pltpu.VMEM((grows, RWKV_COLS), F32)],
        compiler_params=_compiler_params(("arbitrary", "arbitrary")),
    )(hx, hx, w, shift0, s0, *wts)


def _rwkv_cols_in(a):
    return jnp.concatenate([a[..., 0:640], a[..., 704:1984], a[..., 640:704], a[..., 1984:2176]], axis=-1)


def _rwkv_cols_out(a):
    return jnp.concatenate([a[..., 0:640], a[..., 1920:1984], a[..., 640:1920], a[..., 1984:2176]], axis=-1)


def _pad_heads(w, heads, d, dp):
    lead = w.shape[:-1]
    w = w.reshape(lead + (heads, d))
    w = jnp.pad(w, [(0, 0)] * len(lead) + [(0, 0), (0, dp - d)])
    return w.reshape(lead + (heads * dp,))


def _prep_weights(w_in, gla_gate_up, gla_gate_bias, gla_norm, rwkv_mu, rwkv_w_up, rwkv_a_up,
                  rwkv_r_k, w_out):
    wb = w_in.astype(BF16)
    zeros = lambda n: jnp.zeros(wb.shape[:-1] + (n,), BF16)

    def padded_heads(start, d, dp):
        return [piece for h in range(GLA_HEADS)
                for piece in (wb[..., start + h * d:start + (h + 1) * d], zeros(dp - d))]

    w_gla = jnp.concatenate(
        padded_heads(0, GLA_DK, GLA_DKP) + padded_heads(384, GLA_DK, GLA_DKP)
        + padded_heads(768, GLA_DV, GLA_DVP) + padded_heads(1552, GLA_DV, GLA_DVP)
        + [wb[..., 1536:1552], zeros(GLA_RANKP - GLA_RANK)], axis=-1)
    w_rwkv = _rwkv_cols_in(wb[..., 2320:2320 + RWKV_COLS])
    w_hgrn = wb[..., 2320 + RWKV_COLS:]
    gup = jnp.pad(_pad_heads(gla_gate_up, GLA_HEADS, GLA_DK, GLA_DKP),
                  ((0, 0), (0, GLA_RANKP - GLA_RANK), (0, 0))).astype(BF16)
    gbias = _pad_heads(gla_gate_bias, GLA_HEADS, GLA_DK, GLA_DKP)[:, None, :]
    gnorm = jnp.pad(gla_norm, ((0, 0), (0, GLA_DVP - GLA_DV)))[:, None, :]
    mu = _rwkv_cols_in(rwkv_mu)[:, None, :]
    wup = jnp.pad(rwkv_w_up, ((0, 0), (0, 64), (0, 0))).astype(BF16)
    aup = jnp.pad(rwkv_a_up, ((0, 0), (64, 0), (0, 0))).astype(BF16)
    nl = w_in.shape[0]
    rk = rwkv_r_k.reshape(nl, 1, RWKV_WIDTH)
    wo_a = w_out[:, :768].reshape(nl, GLA_HEADS, GLA_DV, D_MODEL)
    wo_a = jnp.pad(wo_a, ((0, 0), (0, 0), (0, GLA_DVP - GLA_DV), (0, 0)))
    wo_a = wo_a.reshape(nl, GLA_VW, D_MODEL).astype(BF16)
    wo_b = w_out[:, 768:1408].astype(BF16)
    wo_c = w_out[:, 1408:].astype(BF16)
    return w_gla, w_rwkv, w_hgrn, gup, gbias, gnorm, mu, wup, aup, rk, wo_a, wo_b, wo_c


def _build_weights(norm_mix_pre, norm_mix_post, norm_ffn_pre, norm_ffn_post, w_in, gla_gate_up,
                   gla_gate_bias, gla_norm, rwkv_mu, rwkv_w0, rwkv_w_up, rwkv_a0, rwkv_a_up,
                   rwkv_g_up, rwkv_k_k, rwkv_k_a, rwkv_r_k, rwkv_ln_w, rwkv_ln_b,
                   hgrn_lb_logits, hgrn_norm, w_out, ffn_w_gate, ffn_w_up, ffn_w_down):
    names = ('w_gla', 'w_rwkv', 'w_hgrn', 'gup', 'gbias', 'gnorm', 'mu', 'wup', 'aup', 'rk',
             'wo_a', 'wo_b', 'wo_c')
    one = lambda a, l: a[l:l + 1]
    per_layer = [_prep_weights(one(w_in, l), one(gla_gate_up, l), one(gla_gate_bias, l),
                               one(gla_norm, l), one(rwkv_mu, l), one(rwkv_w_up, l),
                               one(rwkv_a_up, l), one(rwkv_r_k, l), one(w_out, l))
                 for l in range(DEPTH)]
    wt = {name: [per_layer[l][i][0] for l in range(DEPTH)] for i, name in enumerate(names)}
    vec = lambda a: a[:, None, :]
    wt.update(
        norm_mix_pre=vec(norm_mix_pre), norm_mix_post=vec(norm_mix_post),
        norm_ffn_pre=vec(norm_ffn_pre), norm_ffn_post=vec(norm_ffn_post),
        w0=vec(rwkv_w0), a0=vec(rwkv_a0), g_up=rwkv_g_up.astype(BF16),
        k_k=vec(rwkv_k_k), k_a=vec(rwkv_k_a), ln_w=vec(rwkv_ln_w), ln_b=vec(rwkv_ln_b),
        lb_logits=hgrn_lb_logits, hnorm=vec(hgrn_norm),
        ffn_w_gate=ffn_w_gate.astype(BF16), ffn_w_up=ffn_w_up.astype(BF16),
        ffn_w_down=ffn_w_down.astype(BF16))
    return wt


def _gla_state_in(s):
    s = jnp.swapaxes(s, -1, -2)
    return jnp.pad(s, [(0, 0)] * 3 + [(0, GLA_DVP - GLA_DV), (0, GLA_DKP - GLA_DK)])


def _gla_state_out(s):
    return jnp.swapaxes(s[..., :GLA_DV, :GLA_DK], -1, -2)


def _rwkv_state_in(s):
    d, b = s.shape[:2]
    s = s.reshape(d, b, RWKV_PAIRS, 2, RWKV_HD, RWKV_HD)
    z = jnp.zeros_like(s[:, :, :, 0])
    top = jnp.concatenate([s[:, :, :, 0], z], axis=-1)
    bot = jnp.concatenate([z, s[:, :, :, 1]], axis=-1)
    return jnp.concatenate([top, bot], axis=-2)


def _rwkv_state_out(s):
    d, b = s.shape[:2]
    out = jnp.stack([s[..., :RWKV_HD, :RWKV_HD], s[..., RWKV_HD:, RWKV_HD:]], axis=3)
    return out.reshape(d, b, RWKV_HEADS, RWKV_HD, RWKV_HD)


def _row_tile(n, cap):
    t = cap
    while n % t:
        t //= 2
    return t


def _trunk(x, s_gla, s_rwkv, s_shift, s_hgrn, wt, b, t):
    n = b * t
    tm = _row_tile(n, 512)
    c_glr = min(128, t)
    c_rwkv = min(64, t)
    n_rwkv = 4 if t % (4 * c_rwkv) == 0 else 1
    new_gla, new_rwkv, new_shift, new_hgrn = [], [], [], []
    h = _norm(x, wt['norm_mix_pre'][0], tm)
    for l in range(DEPTH):
        oa, g1 = _gla_mixer(h, wt['w_gla'][l], s_gla[l], wt['gup'][l], wt['gbias'][l],
                            wt['gnorm'][l], b, t, c_glr)
        rw = [wt[k][l] for k in ('mu', 'w0', 'wup', 'a0', 'aup', 'g_up', 'k_k', 'k_a', 'rk',
                                 'ln_w', 'ln_b')]
        ob, sh1, r1 = _rwkv_mixer(h, wt['w_rwkv'][l], s_shift[l], s_rwkv[l], rw, b, t, c_rwkv, n_rwkv)
        oc, h1 = _hgrn_mixer(h, wt['w_hgrn'][l], s_hgrn[l], wt['lb_logits'], wt['hnorm'][l],
                             b, t, c_glr, l)
        x, h = _out_proj(oa, ob, oc, wt['wo_a'][l], wt['wo_b'][l], wt['wo_c'][l], x,
                         wt['norm_mix_post'][l], wt['norm_ffn_pre'][l], tm)
        x, h = _ffn(h, x, wt['ffn_w_gate'], wt['ffn_w_up'], wt['ffn_w_down'], l,
                    wt['norm_ffn_post'][l], wt['norm_mix_pre'][(l + 1) % DEPTH], tm)
        new_gla.append(g1)
        new_rwkv.append(r1)
        new_shift.append(sh1)
        new_hgrn.append(h1)
    return x, jnp.stack(new_gla), jnp.stack(new_rwkv), jnp.stack(new_shift), jnp.stack(new_hgrn)


def _run_path(x, state_gla, state_rwkv, state_shift, state_hgrn, wt):
    b, t, d = x.shape
    y, g, r, sh, h = _trunk(
        x.reshape(b * t, d), _gla_state_in(state_gla), _rwkv_state_in(state_rwkv),
        _rwkv_cols_in(state_shift)[:, :, None, :], jnp.swapaxes(state_hgrn, -1, -2), wt, b, t)
    return (y.reshape(b, t, d), _gla_state_out(g), _rwkv_state_out(r),
            _rwkv_cols_out(sh[:, :, 0, :]), jnp.swapaxes(h, -1, -2))


def kernel(x_prompt, x_sample, state_gla, state_rwkv, state_rwkv_shift, state_hgrn, norm_mix_pre, norm_mix_post, norm_ffn_pre, norm_ffn_post, w_in, gla_gate_up, gla_gate_bias, gla_norm, rwkv_mu, rwkv_w0, rwkv_w_up, rwkv_a0, rwkv_a_up, rwkv_g_up, rwkv_k_k, rwkv_k_a, rwkv_r_k, rwkv_ln_w, rwkv_ln_b, hgrn_lb_logits, hgrn_norm, w_out, ffn_w_gate, ffn_w_up, ffn_w_down):
    wt = _build_weights(norm_mix_pre, norm_mix_post, norm_ffn_pre, norm_ffn_post, w_in, gla_gate_up,
                        gla_gate_bias, gla_norm, rwkv_mu, rwkv_w0, rwkv_w_up, rwkv_a0, rwkv_a_up,
                        rwkv_g_up, rwkv_k_k, rwkv_k_a, rwkv_r_k, rwkv_ln_w, rwkv_ln_b,
                        hgrn_lb_logits, hgrn_norm, w_out, ffn_w_gate, ffn_w_up, ffn_w_down)
    bp = x_prompt.shape[0]
    zeros = lambda s: jnp.zeros((DEPTH, bp) + s.shape[2:], x_prompt.dtype)
    y_p, gla_p, rwkv_p, shift_p, hgrn_p = _run_path(
        x_prompt, zeros(state_gla), zeros(state_rwkv), zeros(state_rwkv_shift), zeros(state_hgrn), wt)
    y_s, gla_s, rwkv_s, shift_s, hgrn_s = _run_path(
        x_sample, state_gla, state_rwkv, state_rwkv_shift, state_hgrn, wt)
    return (y_p, y_s, gla_p, rwkv_p, shift_p, hgrn_p, gla_s, rwkv_s, shift_s, hgrn_s)
```

```python
import functools

import jax
import jax.numpy as jnp
from jax import lax
from jax.experimental import pallas as pl
from jax.experimental.pallas import tpu as pltpu

F32 = jnp.float32
BF16 = jnp.bfloat16

D_MODEL = 2048
DEPTH = 4
EPS = 1e-6

GLA_HEADS, GLA_DK, GLA_DV = 4, 96, 192
GLA_DKP, GLA_DVP = 128, 256
GLA_RANK, GLA_RANKP = 16, 128
GLA_GATE_NORMALIZER = 16.0
GLA_QW = GLA_HEADS * GLA_DKP
GLA_VW = GLA_HEADS * GLA_DVP
GLA_COLS_P = 2 * GLA_QW + 2 * GLA_VW + GLA_RANKP

RWKV_HEADS, RWKV_HD = 10, 64
RWKV_WIDTH = RWKV_HEADS * RWKV_HD
RWKV_PAIRS = RWKV_HEADS // 2
RWKV_COLS = 2176
RWKV_GN_EPS = 64e-5

HGRN_HEADS, HGRN_D = 5, 128
HGRN_WIDTH = HGRN_HEADS * HGRN_D
HGRN_COLS = 4 * HGRN_WIDTH

D_FF = 5632
FFN_TF = 512
LANE = 128
LOG2_E = 1.4426950408889634

VMEM_LIMIT = 56 * 1024 * 1024


def _sigmoid(x):
    return 1.0 / (1.0 + jnp.exp(-x))


def _silu(x):
    return x * _sigmoid(x)


def _log_sigmoid(x):
    return jnp.minimum(x, 0.0) - jnp.log(1.0 + jnp.exp(-jnp.abs(x)))


def _dot(a, b):
    return jnp.dot(a.astype(BF16), b.astype(BF16), preferred_element_type=F32)


def _dot_nt(a, b):
    return lax.dot_general(a.astype(BF16), b.astype(BF16), (((1,), (1,)), ((), ())),
                           preferred_element_type=F32)


def _dot_tn(a, b):
    return lax.dot_general(a.astype(BF16), b.astype(BF16), (((0,), (0,)), ((), ())),
                           preferred_element_type=F32)


def _split_bf16(x, parts):
    out = []
    r = x
    for i in range(parts):
        p = r.astype(BF16)
        out.append(p)
        if i + 1 < parts:
            r = r - p.astype(F32)
    return out


def _dot_exact_lhs(m_bf16, x, parts=3):
    acc = None
    for p in _split_bf16(x, parts):
        t = jnp.dot(m_bf16, p, preferred_element_type=F32)
        acc = t if acc is None else acc + t
    return acc


def _iota(shape, dim):
    return lax.broadcasted_iota(jnp.int32, shape, dim)


def _tril_ones(c):
    return jnp.where(_iota((c, c), 0) >= _iota((c, c), 1), 1.0, 0.0).astype(BF16)


def _rms_rows(x, g):
    ms = jnp.mean(x * x, axis=-1, keepdims=True)
    return x * lax.rsqrt(ms + EPS) * g


def _compiler_params(sem, flags=None):
    return pltpu.CompilerParams(dimension_semantics=sem, vmem_limit_bytes=VMEM_LIMIT, flags=flags)


def _norm_kernel(x_ref, g_ref, h_ref):
    h_ref[...] = _rms_rows(x_ref[...], g_ref[...]).astype(BF16)


def _norm(x, g, tm):
    n, d = x.shape
    return pl.pallas_call(
        _norm_kernel,
        grid=(n // tm,),
        in_specs=[pl.BlockSpec((tm, d), lambda i: (i, 0)), pl.BlockSpec((1, d), lambda i: (0, 0))],
        out_specs=pl.BlockSpec((tm, d), lambda i: (i, 0)),
        out_shape=jax.ShapeDtypeStruct((n, d), BF16),
        compiler_params=_compiler_params(("arbitrary",)),
    )(x, g)


def _out_proj_kernel(oa_ref, ob_ref, oc_ref, wa_ref, wb_ref, wc_ref, x_ref, g_ref, gnext_ref,
                     y_ref, hn_ref):
    half = x_ref.shape[0] // 2
    for r in (slice(0, half), slice(half, 2 * half)):
        m = (jnp.dot(oa_ref[r, :], wa_ref[...], preferred_element_type=F32)
             + jnp.dot(ob_ref[r, :], wb_ref[...], preferred_element_type=F32)
             + jnp.dot(oc_ref[r, :], wc_ref[...], preferred_element_type=F32))
        y = x_ref[r, :] + _rms_rows(m, g_ref[...])
        y_ref[r, :] = y
        hn_ref[r, :] = _rms_rows(y, gnext_ref[...]).astype(BF16)


def _out_proj(oa, ob, oc, wa, wb, wc, x, g, gnext, tm):
    n, d = x.shape
    row = lambda w: pl.BlockSpec((tm, w), lambda i: (i, 0))
    full = lambda a: pl.BlockSpec(a.shape, lambda i: (0, 0))
    return pl.pallas_call(
        _out_proj_kernel,
        grid=(n // tm,),
        in_specs=[row(oa.shape[1]), row(ob.shape[1]), row(oc.shape[1]),
                  full(wa), full(wb), full(wc), row(d), full(g), full(gnext)],
        out_specs=[row(d), row(d)],
        out_shape=[jax.ShapeDtypeStruct((n, d), F32), jax.ShapeDtypeStruct((n, d), BF16)],
        compiler_params=_compiler_params(("arbitrary",)),
    )(oa, ob, oc, wa, wb, wc, x, g, gnext)


def _ffn_kernel(h_ref, x_ref, wg_ref, wu_ref, wd_ref, gpost_ref, gnext_ref, y_ref, hn_ref, acc_scr):
    j = pl.program_id(1)

    @pl.when(j == 0)
    def _():
        acc_scr[...] = jnp.zeros_like(acc_scr)

    h = h_ref[...]
    a = jnp.dot(h, wg_ref[...], preferred_element_type=F32)
    u = jnp.dot(h, wu_ref[...], preferred_element_type=F32)
    acc_scr[...] += jnp.dot((_silu(a) * u).astype(BF16), wd_ref[...], preferred_element_type=F32)

    @pl.when(j == pl.num_programs(1) - 1)
    def _():
        y = x_ref[...] + _rms_rows(acc_scr[...], gpost_ref[...])
        y_ref[...] = y
        hn_ref[...] = _rms_rows(y, gnext_ref[...]).astype(BF16)


def _ffn(h, x, wg, wu, wd, layer, gpost, gnext, tm):
    n, d = x.shape
    dff = wg.shape[2]
    tf = FFN_TF
    return pl.pallas_call(
        _ffn_kernel,
        grid=(n // tm, dff // tf),
        in_specs=[pl.BlockSpec((tm, d), lambda i, j: (i, 0)),
                  pl.BlockSpec((tm, d), lambda i, j: (i, 0)),
                  pl.BlockSpec((None, d, tf), lambda i, j: (layer, 0, j)),
                  pl.BlockSpec((None, d, tf), lambda i, j: (layer, 0, j)),
                  pl.BlockSpec((None, tf, d), lambda i, j: (layer, j, 0)),
                  pl.BlockSpec((1, d), lambda i, j: (0, 0)),
                  pl.BlockSpec((1, d), lambda i, j: (0, 0))],
        out_specs=[pl.BlockSpec((tm, d), lambda i, j: (i, 0)),
                   pl.BlockSpec((tm, d), lambda i, j: (i, 0))],
        out_shape=[jax.ShapeDtypeStruct((n, d), F32), jax.ShapeDtypeStruct((n, d), BF16)],
        scratch_shapes=[pltpu.VMEM((tm, d), F32)],
        compiler_params=_compiler_params(("arbitrary", "arbitrary")),
    )(h, x, wg, wu, wd, gpost, gnext)


class _GlrMasks:
    def __init__(self, c):
        self.c = c
        row_a = _iota((c, c), 0)
        col_a = _iota((c, c), 1)
        self.tril = _tril_ones(c)
        self.level = {}
        h = 1
        while 2 * h <= c:
            self.level[h] = (((row_a // (2 * h)) == (col_a // (2 * h)))
                             & ((row_a // h) % 2 == 1) & ((col_a // h) % 2 == 0))
            h *= 2
        self.eye = row_a == col_a


def _small_level_factors(g):
    c = g.shape[0]
    r4 = _iota(g.shape, 0) % 4
    g_prev = pltpu.roll(g, 1, 0)
    g_next = pltpu.roll(g, c - 1, 0)
    e1 = jnp.exp2(jnp.where(r4 % 2 == 1, g, 0.0))
    e2 = jnp.exp2(jnp.where(r4 == 3, g + g_prev, jnp.where(r4 == 2, g, jnp.where(r4 == 0, g_next, 0.0))))
    return e1, e2


def _glr_chunk(q, k, v, b, e1, e2, st, mk):
    c, dk = q.shape
    att = jnp.where(mk.eye, _dot_nt(q, k), 0.0)
    for h, pair_mask in mk.level.items():
        if h == 1:
            e = e1
        elif h == 2:
            e = e2
        else:
            n = c // (2 * h)
            ref = b.reshape(n, 2 * h, dk)[:, h - 1:h, :]
            refb = jnp.broadcast_to(ref, (n, 2 * h, dk)).reshape(c, dk)
            e = jnp.exp2(-jnp.abs(b - refb))
        att = jnp.where(pair_mask, _dot_nt(q * e, k * e), att)
    o = _dot(att, v) + _dot_nt(q * jnp.exp2(b), st)
    b_end = b[c - 1:c, :]
    st_new = st * jnp.exp2(b_end) + _dot_tn(v, k * jnp.exp2(b_end - b))
    return o, st_new


def _ahead_groups(cols, slots):
    tiles = -(-cols // (2 * LANE))
    bounds = [min(cols, 2 * LANE * ((k * tiles) // slots)) for k in range(slots)] + [cols]
    return [slice(bounds[k], bounds[k + 1]) for k in range(slots)]


MAX_WHOLE_CALL_ROWS = 1024


def _input_row_specs(b, t, rows, d, nsub):
    if nsub == 1:
        assert b * t <= MAX_WHOLE_CALL_ROWS, (b, t)
        whole = pl.BlockSpec((b * t, d), lambda i, j: (0, 0))
        return whole, whole
    nc = t // rows
    nblk = b * nc
    return (pl.BlockSpec((rows, d), lambda i, j: (i * nc + j, 0)),
            pl.BlockSpec((rows, d), lambda i, j: (jnp.minimum(i * nc + j + 1, nblk - 1), 0)))


def _first_projection(hc_ref, w_ref, pa_scr, c, nsub, first_rows):
    i, ci = pl.program_id(0), pl.program_id(1)
    if nsub > 1:
        @pl.when(ci == 0)
        def _():
            pa_scr[...] = jnp.dot(hc_ref[0:first_rows, :], w_ref[...], preferred_element_type=F32)
        return 0

    @pl.when(jnp.logical_and(i == 0, ci == 0))
    def _():
        pa_scr[...] = jnp.dot(hc_ref[...], w_ref[...], preferred_element_type=F32)
    return pl.multiple_of((i * pl.num_programs(1) + ci) * c, c)


def _gla_kernel(hc_ref, hn_ref, w_ref, s0_ref, gup_ref, gbias_ref, gnorm_ref, o_ref, sout_ref,
                s_scr, pa_scr, pb_scr, *, c, nsub):
    ci = pl.program_id(1)
    half = max(nsub // 2, 1)
    hrows = c * half
    groups = _ahead_groups(GLA_COLS_P, GLA_HEADS * half)

    @pl.when(ci == 0)
    def _():
        s_scr[...] = s0_ref[0]

    row0 = _first_projection(hc_ref, w_ref, pa_scr, c, nsub, hrows)

    mk = _GlrMasks(c)
    gnorm = gnorm_ref[...]
    for i in range(nsub):
        rs = slice(i * c, (i + 1) * c)
        src = pa_scr if i < half else pb_scr
        p = src[(i % half) * c:(i % half + 1) * c, :] if nsub > 1 else pa_scr[pl.ds(row0, c), :]
        q_all = p[:, 0:GLA_QW] * (GLA_DK ** -0.5)
        k_all = p[:, GLA_QW:2 * GLA_QW]
        v_all = p[:, 2 * GLA_QW:2 * GLA_QW + GLA_VW]
        gout = p[:, 2 * GLA_QW + GLA_VW:2 * GLA_QW + 2 * GLA_VW]
        gdown = p[:, 2 * GLA_QW + 2 * GLA_VW:]
        g_all = (_log_sigmoid(_dot(gdown, gup_ref[...]) + gbias_ref[...])
                 * (LOG2_E / GLA_GATE_NORMALIZER))
        b_all = _dot_exact_lhs(mk.tril, g_all)
        e1_all, e2_all = _small_level_factors(g_all)
        for h in range(GLA_HEADS):
            ks = slice(h * GLA_DKP, (h + 1) * GLA_DKP)
            vs = slice(h * GLA_DVP, (h + 1) * GLA_DVP)
            o, st = _glr_chunk(q_all[:, ks], k_all[:, ks], v_all[:, vs], b_all[:, ks],
                               e1_all[:, ks], e2_all[:, ks], s_scr[h], mk)
            s_scr[h] = st
            ms = jnp.sum(o * o, axis=-1, keepdims=True) * (1.0 / GLA_DV)
            o = o * lax.rsqrt(ms + EPS) * gnorm * _silu(gout[:, vs])
            o_ref[rs, vs] = o.astype(BF16)
            if nsub > 1:
                g = groups[(i % half) * GLA_HEADS + h]
                ahead, dst = (hc_ref[hrows:, :], pb_scr) if i < half else (hn_ref[0:hrows, :], pa_scr)
                dst[:, g] = jnp.dot(ahead, w_ref[:, g], preferred_element_type=F32)

    @pl.when(ci == pl.num_programs(1) - 1)
    def _():
        sout_ref[0] = s_scr[...]


def _gla_mixer(hx, w, s0t, gup, gbias, gnorm, b, t, c):
    nsub = 4 if t % (4 * c) == 0 else 1
    rows = c * nsub
    hrows = c * max(nsub // 2, 1)
    nc = t // rows
    nblk = b * nc
    kern = functools.partial(_gla_kernel, c=c, nsub=nsub)
    full = lambda a: pl.BlockSpec(a.shape, lambda i, j: (0,) * a.ndim)
    st_spec = pl.BlockSpec((1, GLA_HEADS, GLA_DVP, GLA_DKP), lambda i, j: (i, 0, 0, 0))
    d = hx.shape[1]
    return pl.pallas_call(
        kern,
        grid=(b, nc),
        in_specs=list(_input_row_specs(b, t, rows, d, nsub)) + [
                  pl.BlockSpec(w.shape, lambda i, j: (0, 0), pipeline_mode=pl.Buffered(1)),
                  st_spec, full(gup), full(gbias), full(gnorm)],
        out_specs=[pl.BlockSpec((rows, GLA_VW), lambda i, j: (i * nc + j, 0)), st_spec],
        out_shape=[jax.ShapeDtypeStruct((b * t, GLA_VW), BF16),
                   jax.ShapeDtypeStruct((b, GLA_HEADS, GLA_DVP, GLA_DKP), F32)],
        scratch_shapes=[pltpu.VMEM((GLA_HEADS, GLA_DVP, GLA_DKP), F32),
                        pltpu.VMEM((hrows if nsub > 1 else b * t, GLA_COLS_P), F32),
                        pltpu.VMEM((hrows, GLA_COLS_P), F32)],
        compiler_params=_compiler_params(("arbitrary", "arbitrary")),
    )(hx, hx, w, s0t, gup, gbias, gnorm)


def _hgrn_kernel(hc_ref, hn_ref, w_ref, s0_ref, lbl_ref, hnorm_ref, o_ref, sout_ref,
                 s_scr, pa_scr, pb_scr, *, c, nsub, layer):
    ci = pl.program_id(1)
    half = max(nsub // 2, 1)
    hrows = c * half
    gw = HGRN_COLS // (HGRN_HEADS * half)

    @pl.when(ci == 0)
    def _():
        s_scr[...] = s0_ref[0]

    row0 = _first_projection(hc_ref, w_ref, pa_scr, c, nsub, hrows)

    lg = lbl_ref[...]
    e = jnp.exp(lg - jnp.max(lg, axis=0, keepdims=True))
    prob = e / jnp.sum(e, axis=0, keepdims=True)
    lb = jnp.zeros((1, HGRN_WIDTH), F32)
    for i in range(1, layer + 1):
        lb = lb + prob[i:i + 1, :]

    mk = _GlrMasks(c)
    hnorm = hnorm_ref[...]

    for i in range(nsub):
        rs = slice(i * c, (i + 1) * c)
        src = pa_scr if i < half else pb_scr
        p = src[(i % half) * c:(i % half + 1) * c, :] if nsub > 1 else pa_scr[pl.ds(row0, c), :]
        hq = p[:, 0:HGRN_WIDTH]
        hf = p[:, HGRN_WIDTH:2 * HGRN_WIDTH]
        hi = p[:, 2 * HGRN_WIDTH:3 * HGRN_WIDTH]
        hg = p[:, 3 * HGRN_WIDTH:]
        q_all = _silu(hq)
        f_all = lb + (1.0 - lb) * _sigmoid(hf)
        k_all = 1.0 - f_all
        g_all = jnp.log2(f_all)
        b_all = _dot_exact_lhs(mk.tril, g_all)
        e1_all, e2_all = _small_level_factors(g_all)
        for h in range(HGRN_HEADS):
            s = slice(h * HGRN_D, (h + 1) * HGRN_D)
            o, st = _glr_chunk(q_all[:, s], k_all[:, s], hi[:, s], b_all[:, s],
                               e1_all[:, s], e2_all[:, s], s_scr[h], mk)
            s_scr[h] = st
            o = _rms_rows(o, hnorm) * _silu(hg[:, s])
            o_ref[rs, s] = o.astype(BF16)
            if nsub > 1:
                k = (i % half) * HGRN_HEADS + h
                g = slice(k * gw, (k + 1) * gw)
                ahead, dst = (hc_ref[hrows:, :], pb_scr) if i < half else (hn_ref[0:hrows, :], pa_scr)
                dst[:, g] = jnp.dot(ahead, w_ref[:, g], preferred_element_type=F32)

    @pl.when(ci == pl.num_programs(1) - 1)
    def _():
        sout_ref[0] = s_scr[...]


def _hgrn_mixer(hx, w, s0t, lb_logits, hnorm, b, t, c, layer):
    nsub = 4 if t % (4 * c) == 0 else 1
    rows = c * nsub
    hrows = c * max(nsub // 2, 1)
    nc = t // rows
    nblk = b * nc
    kern = functools.partial(_hgrn_kernel, c=c, nsub=nsub, layer=layer)
    full = lambda a: pl.BlockSpec(a.shape, lambda i, j: (0,) * a.ndim)
    st_spec = pl.BlockSpec((1, HGRN_HEADS, HGRN_D, HGRN_D), lambda i, j: (i, 0, 0, 0))
    d = hx.shape[1]
    return pl.pallas_call(
        kern,
        grid=(b, nc),
        in_specs=list(_input_row_specs(b, t, rows, d, nsub)) + [
                  pl.BlockSpec(w.shape, lambda i, j: (0, 0), pipeline_mode=pl.Buffered(1)),
                  st_spec, full(lb_logits), full(hnorm)],
        out_specs=[pl.BlockSpec((rows, HGRN_WIDTH), lambda i, j: (i * nc + j, 0)), st_spec],
        out_shape=[jax.ShapeDtypeStruct((b * t, HGRN_WIDTH), BF16),
                   jax.ShapeDtypeStruct((b, HGRN_HEADS, HGRN_D, HGRN_D), F32)],
        scratch_shapes=[pltpu.VMEM((HGRN_HEADS, HGRN_D, HGRN_D), F32),
                        pltpu.VMEM((hrows if nsub > 1 else b * t, HGRN_COLS), F32),
                        pltpu.VMEM((hrows, HGRN_COLS), F32)],
        compiler_params=_compiler_params(("arbitrary", "arbitrary")),
    )(hx, hx, w, s0t, lb_logits, hnorm)


def _rwkv_kernel(hc_ref, hn_ref, w_ref, shift0_ref, s0_ref, mu_ref, w0_ref, wup_ref, a0_ref,
                 aup_ref, gup_ref, kk_ref, ka_ref, rk_ref, lnw_ref, lnb_ref,
                 o_ref, shift_ref, sout_ref, s_scr, carry_scr, pa_scr, pb_scr, *, c, nsub):
    ci = pl.program_id(1)
    w = RWKV_WIDTH
    rows = c * nsub
    grp = min(nsub, 2)
    grows = c * grp

    @pl.when(ci == 0)
    def _():
        s_scr[...] = s0_ref[0]
        carry_scr[...] = shift0_ref[0]

    row0 = _first_projection(hc_ref, w_ref, pa_scr, c, nsub, grows)

    first_prev = carry_scr[...]
    grp_count = rows // grows
    col_tiles = _ahead_groups(RWKV_COLS, -(-RWKV_COLS // (2 * LANE)))

    c2 = 2 * c
    lane = _iota((c, LANE), 1)
    lo_half = lane < RWKV_HD
    ri = _iota((c2, c2), 0)
    cj = _iota((c2, c2), 1)
    strict = (ri % c) > (cj % c)
    incl = (ri % c) >= (cj % c)
    eye = jnp.where(ri == cj, 1.0, 0.0)
    inv_levels = []
    blk = 1
    while blk < c:
        inv_levels.append(((ri // (2 * blk)) == (cj // (2 * blk)))
                          & ((ri // blk) % 2 == 1) & ((cj // blk) % 2 == 0))
        blk *= 2
    tr, tc = _iota((grows, grows), 0), _iota((grows, grows), 1)
    tril = jnp.where((tr >= tc) & (tr // c == tc // c), 1.0, 0.0).astype(BF16)
    row = _iota((grows, RWKV_COLS), 0)
    seg = jnp.where((_iota((LANE, LANE), 0) // RWKV_HD) == (_iota((LANE, LANE), 1) // RWKV_HD),
                    1.0, 0.0).astype(BF16)

    def stack(x):
        return jnp.concatenate([jnp.where(lo_half, x, 0.0), jnp.where(lo_half, 0.0, x)], axis=0)

    def head_sum(x):
        return jnp.concatenate([_dot(x[:, j * LANE:(j + 1) * LANE], seg)
                                for j in range(RWKV_PAIRS)], axis=1)

    def prepare(gi):
        g0 = gi * grows
        src = pb_scr if gi else pa_scr
        p = src[...] if nsub > 1 else pa_scr[pl.ds(row0, grows), :]
        before = pa_scr[grows - 1:grows, :] if gi else first_prev
        if nsub == 1:
            todo = iter(())
        elif gi == 0:
            todo = iter([(hc_ref, slice(grows, 2 * grows), pb_scr, g) for g in col_tiles])
        else:
            todo = iter([(hn_ref, slice(0, grows), pa_scr, g) for g in col_tiles])

        def project_one():
            t = next(todo, None)
            if t is not None:
                h_ref, h_rows, dst, g = t
                dst[:, g] = jnp.dot(h_ref[h_rows, :], w_ref[:, g], preferred_element_type=F32)

        prev = jnp.where(row == 0, before, pltpu.roll(p, 1, 0))
        pm = p + (prev - p) * mu_ref[...]
        project_one()
        r_all = pm[:, 0:w]
        k_all = pm[:, w:2 * w]
        v_all = pm[:, 2 * w:3 * w]
        wa = pm[:, 3 * w:3 * w + LANE]
        gd = pm[:, 3 * w + LANE:]
        lw_all = -jnp.exp(_log_sigmoid(w0_ref[...] + _dot(jnp.tanh(wa), wup_ref[...])) - 0.5)
        a_all = _sigmoid(a0_ref[...] + _dot(wa, aup_ref[...]))
        project_one()
        gate_all = _dot(_sigmoid(gd), gup_ref[...])
        kk_all = k_all * kk_ref[...]
        kk_all = kk_all * lax.rsqrt(jnp.maximum(head_sum(kk_all * kk_all), 1e-24))
        project_one()
        k_all = k_all * (1.0 + (a_all - 1.0) * ka_ref[...])
        beta_all = kk_all * a_all
        bonus_all = head_sum(r_all * k_all * rk_ref[...]) * v_all
        project_one()
        cin_all = _dot_exact_lhs(tril, lw_all, parts=2)
        ginv = jnp.exp(-cin_all)
        kap_all = kk_all * jnp.exp(cin_all - lw_all)
        project_one()
        rt_all = r_all * jnp.exp(cin_all)
        kt_all = k_all * ginv
        bt_all = beta_all * ginv
        out = []
        for i in range(grp):
            rs = slice(i * c, (i + 1) * c)
            cend = cin_all[(i + 1) * c - 1:(i + 1) * c, :]
            gend = jnp.exp(cend - cin_all[rs])
            kend, bend, dec = k_all[rs] * gend, beta_all[rs] * gend, jnp.exp(cend)
            for j in range(RWKV_PAIRS):
                s = slice(j * LANE, (j + 1) * LANE)
                out.append(dict(
                    s=s, rs=slice(g0 + i * c, g0 + (i + 1) * c),
                    kap_s=stack(kap_all[rs, s]), r_s=stack(rt_all[rs, s]), k_s=stack(kt_all[rs, s]),
                    b_s=stack(bt_all[rs, s]), v_s=stack(v_all[rs, s]), kend_s=stack(kend[:, s]),
                    bend_s=stack(bend[:, s]), dec=dec[:, s], bonus=bonus_all[rs, s],
                    gate=gate_all[rs, s]))
                if j % 2 == 0:
                    project_one()
        for _ in col_tiles:
            project_one()
        if gi == grp_count - 1:
            last = p[grows - 1:grows, :]
            carry_scr[...] = last
            shift_ref[0] = last
        return out

    def dot_nt2(lhs, r1, r2):
        if r1.shape[0] % LANE:
            return _dot_nt(lhs, r1), _dot_nt(lhs, r2)
        both = _dot_nt(lhs, jnp.concatenate([r1, r2], axis=0))
        return both[:, :r1.shape[0]], both[:, r1.shape[0]:]

    def dot2(lhs, r1, r2):
        both = _dot(lhs, jnp.concatenate([r1, r2], axis=1))
        return both[:, :r1.shape[1]], both[:, r1.shape[1]:]

    pairs = []
    for gi in range(grp_count):
        group = prepare(gi)
        for pr in group:
            a1, a2 = dot_nt2(pr['kap_s'], pr['k_s'], pr['b_s'])
            pr['a1'], pr['a2'] = jnp.where(strict, a1, 0.0), jnp.where(strict, a2, 0.0)
        for pr in group:
            a3, a4 = dot_nt2(pr['r_s'], pr['k_s'], pr['b_s'])
            pr['a3'], pr['a4'] = jnp.where(incl, a3, 0.0), jnp.where(incl, a4, 0.0)
        for pr in group:
            pr['a1v'] = _dot(pr['a1'], pr['v_s'])
            pr['a3v'] = _dot(pr['a3'], pr['v_s'])
            pr['kv'] = _dot_tn(pr['v_s'], pr['kend_s'])
        pairs += group
    if c2 % LANE == 0 and len(pairs) % 2 == 0:
        left = _iota((c2, 2 * c2), 1) < c2
        tile2 = lambda x: jnp.concatenate([x, x], axis=1)

        def block_diag(x, keep=None):
            top = left if keep is None else keep & left
            bot = ~left if keep is None else keep & ~left
            x = x.astype(BF16)
            return jnp.concatenate([jnp.where(top, x, 0.0), jnp.where(bot, x, 0.0)], axis=0)

        a2w = [jnp.concatenate([pa['a2'], pb['a2']], axis=1) for pa, pb in zip(pairs[0::2], pairs[1::2])]
        tw = [tile2(eye) - jnp.where(tile2(inv_levels[0]), a, 0.0) for a in a2w]
        for m in inv_levels[1:]:
            mw = tile2(m)
            half = [_dot(t, block_diag(a, mw)) for t, a in zip(tw, a2w)]
            tw = [t - _dot(hf, block_diag(t)) for t, hf in zip(tw, half)]
        tinv = [x for t in tw for x in (t[:, :c2], t[:, c2:])]
    else:
        tinv = [eye - jnp.where(inv_levels[0], pr['a2'], 0.0) for pr in pairs]
        for m in inv_levels[1:]:
            half = [_dot(t, jnp.where(m, pr['a2'], 0.0)) for t, pr in zip(tinv, pairs)]
            tinv = [t - _dot(hf, t) for t, hf in zip(tinv, half)]
    for pr, tj in zip(pairs, tinv):
        pr['tk'], pr['tv'] = dot2(tj, pr['kap_s'], pr['a1v'])
    for pr in pairs:
        a4tk, a4tv = dot2(pr['a4'], pr['tk'], pr['tv'])
        pr['reff'] = pr['r_s'] - a4tk
        pr['oc'] = pr['a3v'] - a4tv
        pr['m'] = _dot_tn(pr['tk'], pr['bend_s'])
        pr['q'] = pr['kv'] - _dot_tn(pr['tv'], pr['bend_s'])
    sts = [s_scr[j] for j in range(RWKV_PAIRS)]
    os_ = []
    for i in range(nsub):
        sub = list(zip(pairs[i * RWKV_PAIRS:(i + 1) * RWKV_PAIRS], sts))
        o2s = [_dot_nt(pr['reff'], st) + pr['oc'] for pr, st in sub]
        sts = [st * pr['dec'] - _dot(st, pr['m']) + pr['q'] for pr, st in sub]
        os_ += [o2[0:c] + o2[c:c2] for o2 in o2s]
    for j, st in enumerate(sts):
        s_scr[j] = st
    ds = [o - _dot(o, seg) * (1.0 / RWKV_HD) for o in os_]
    vars_ = [_dot(d * d, seg) * (1.0 / RWKV_HD) for d in ds]
    for pr, d, var in zip(pairs, ds, vars_):
        s, rs = pr['s'], pr['rs']
        on = d * lax.rsqrt(var + RWKV_GN_EPS) * lnw_ref[:, s] + lnb_ref[:, s]
        o_ref[rs, s] = ((on + pr['bonus']) * pr['gate']).astype(BF16)

    @pl.when(ci == pl.num_programs(1) - 1)
    def _():
        sout_ref[0] = s_scr[...]


def _rwkv_mixer(hx, w, shift0, s0, wts, b, t, c, nsub):
    rows = c * nsub
    grows = c * min(nsub, 2)
    nc = t // rows
    nblk = b * nc
    d = hx.shape[1]
    kern = functools.partial(_rwkv_kernel, c=c, nsub=nsub)
    full = lambda a: pl.BlockSpec(a.shape, lambda i, j: (0,) * a.ndim)
    st_spec = pl.BlockSpec((1, RWKV_PAIRS, LANE, LANE), lambda i, j: (i, 0, 0, 0))
    sh_spec = pl.BlockSpec((1, 1, RWKV_COLS), lambda i, j: (i, 0, 0))
    return pl.pallas_call(
        kern,
        grid=(b, nc),
        in_specs=list(_input_row_specs(b, t, rows, d, nsub)) + [
                  pl.BlockSpec(w.shape, lambda i, j: (0, 0), pipeline_mode=pl.Buffered(1)),
                  sh_spec, st_spec] + [full(a) for a in wts],
        out_specs=[pl.BlockSpec((rows, RWKV_WIDTH), lambda i, j: (i * nc + j, 0)), sh_spec, st_spec],
        out_shape=[jax.ShapeDtypeStruct((b * t, RWKV_WIDTH), BF16),
                   jax.ShapeDtypeStruct((b, 1, RWKV_COLS), F32),
                   jax.ShapeDtypeStruct((b, RWKV_PAIRS, LANE, LANE), F32)],
        scratch_shapes=[pltpu.VMEM((RWKV_PAIRS, LANE, LANE), F32), pltpu.VMEM((1, RWKV_COLS), F32),
                        pltpu.VMEM((grows if nsub > 1 else b * t, RWKV_COLS), F32),
                        pltpu.VMEM((grows, RWKV_COLS), F32)],
        compiler_params=_compiler_params(("arbitrary", "arbitrary")),
    )(hx, hx, w, shift0, s0, *wts)


def _rwkv_cols_in(a):
    return jnp.concatenate([a[..., 0:640], a[..., 704:1984], a[..., 640:704], a[..., 1984:2176]], axis=-1)


def _rwkv_cols_out(a):
    return jnp.concatenate([a[..., 0:640], a[..., 1920:1984], a[..., 640:1920], a[..., 1984:2176]], axis=-1)


def _pad_heads(w, heads, d, dp):
    lead = w.shape[:-1]
    w = w.reshape(lead + (heads, d))
    w = jnp.pad(w, [(0, 0)] * len(lead) + [(0, 0), (0, dp - d)])
    return w.reshape(lead + (heads * dp,))


def _prep_weights(w_in, gla_gate_up, gla_gate_bias, gla_norm, rwkv_mu, rwkv_w_up, rwkv_a_up,
                  rwkv_r_k, w_out):
    wb = w_in.astype(BF16)
    zeros = lambda n: jnp.zeros(wb.shape[:-1] + (n,), BF16)

    def padded_heads(start, d, dp):
        return [piece for h in range(GLA_HEADS)
                for piece in (wb[..., start + h * d:start + (h + 1) * d], zeros(dp - d))]

    w_gla = jnp.concatenate(
        padded_heads(0, GLA_DK, GLA_DKP) + padded_heads(384, GLA_DK, GLA_DKP)
        + padded_heads(768, GLA_DV, GLA_DVP) + padded_heads(1552, GLA_DV, GLA_DVP)
        + [wb[..., 1536:1552], zeros(GLA_RANKP - GLA_RANK)], axis=-1)
    w_rwkv = _rwkv_cols_in(wb[..., 2320:2320 + RWKV_COLS])
    w_hgrn = wb[..., 2320 + RWKV_COLS:]
    gup = jnp.pad(_pad_heads(gla_gate_up, GLA_HEADS, GLA_DK, GLA_DKP),
                  ((0, 0), (0, GLA_RANKP - GLA_RANK), (0, 0))).astype(BF16)
    gbias = _pad_heads(gla_gate_bias, GLA_HEADS, GLA_DK, GLA_DKP)[:, None, :]
    gnorm = jnp.pad(gla_norm, ((0, 0), (0, GLA_DVP - GLA_DV)))[:, None, :]
    mu = _rwkv_cols_in(rwkv_mu)[:, None, :]
    wup = jnp.pad(rwkv_w_up, ((0, 0), (0, 64), (0, 0))).astype(BF16)
    aup = jnp.pad(rwkv_a_up, ((0, 0), (64, 0), (0, 0))).astype(BF16)
    nl = w_in.shape[0]
    rk = rwkv_r_k.reshape(nl, 1, RWKV_WIDTH)
    wo_a = w_out[:, :768].reshape(nl, GLA_HEADS, GLA_DV, D_MODEL)
    wo_a = jnp.pad(wo_a, ((0, 0), (0, 0), (0, GLA_DVP - GLA_DV), (0, 0)))
    wo_a = wo_a.reshape(nl, GLA_VW, D_MODEL).astype(BF16)
    wo_b = w_out[:, 768:1408].astype(BF16)
    wo_c = w_out[:, 1408:].astype(BF16)
    return w_gla, w_rwkv, w_hgrn, gup, gbias, gnorm, mu, wup, aup, rk, wo_a, wo_b, wo_c


def _build_weights(norm_mix_pre, norm_mix_post, norm_ffn_pre, norm_ffn_post, w_in, gla_gate_up,
                   gla_gate_bias, gla_norm, rwkv_mu, rwkv_w0, rwkv_w_up, rwkv_a0, rwkv_a_up,
                   rwkv_g_up, rwkv_k_k, rwkv_k_a, rwkv_r_k, rwkv_ln_w, rwkv_ln_b,
                   hgrn_lb_logits, hgrn_norm, w_out, ffn_w_gate, ffn_w_up, ffn_w_down):
    names = ('w_gla', 'w_rwkv', 'w_hgrn', 'gup', 'gbias', 'gnorm', 'mu', 'wup', 'aup', 'rk',
             'wo_a', 'wo_b', 'wo_c')
    one = lambda a, l: a[l:l + 1]
    per_layer = [_prep_weights(one(w_in, l), one(gla_gate_up, l), one(gla_gate_bias, l),
                               one(gla_norm, l), one(rwkv_mu, l), one(rwkv_w_up, l),
                               one(rwkv_a_up, l), one(rwkv_r_k, l), one(w_out, l))
                 for l in range(DEPTH)]
    wt = {name: [per_layer[l][i][0] for l in range(DEPTH)] for i, name in enumerate(names)}
    vec = lambda a: a[:, None, :]
    wt.update(
        norm_mix_pre=vec(norm_mix_pre), norm_mix_post=vec(norm_mix_post),
        norm_ffn_pre=vec(norm_ffn_pre), norm_ffn_post=vec(norm_ffn_post),
        w0=vec(rwkv_w0), a0=vec(rwkv_a0), g_up=rwkv_g_up.astype(BF16),
        k_k=vec(rwkv_k_k), k_a=vec(rwkv_k_a), ln_w=vec(rwkv_ln_w), ln_b=vec(rwkv_ln_b),
        lb_logits=hgrn_lb_logits, hnorm=vec(hgrn_norm),
        ffn_w_gate=ffn_w_gate.astype(BF16), ffn_w_up=ffn_w_up.astype(BF16),
        ffn_w_down=ffn_w_down.astype(BF16))
    return wt


def _gla_state_in(s):
    s = jnp.swapaxes(s, -1, -2)
    return jnp.pad(s, [(0, 0)] * 3 + [(0, GLA_DVP - GLA_DV), (0, GLA_DKP - GLA_DK)])


def _gla_state_out(s):
    return jnp.swapaxes(s[..., :GLA_DV, :GLA_DK], -1, -2)


def _rwkv_state_in(s):
    d, b = s.shape[:2]
    s = s.reshape(d, b, RWKV_PAIRS, 2, RWKV_HD, RWKV_HD)
    z = jnp.zeros_like(s[:, :, :, 0])
    top = jnp.concatenate([s[:, :, :, 0], z], axis=-1)
    bot = jnp.concatenate([z, s[:, :, :, 1]], axis=-1)
    return jnp.concatenate([top, bot], axis=-2)


def _rwkv_state_out(s):
    d, b = s.shape[:2]
    out = jnp.stack([s[..., :RWKV_HD, :RWKV_HD], s[..., RWKV_HD:, RWKV_HD:]], axis=3)
    return out.reshape(d, b, RWKV_HEADS, RWKV_HD, RWKV_HD)


def _row_tile(n, cap):
    t = cap
    while n % t:
        t //= 2
    return t


def _trunk(x, s_gla, s_rwkv, s_shift, s_hgrn, wt, b, t):
    n = b * t
    tm = _row_tile(n, 512)
    c_glr = min(128, t)
    c_rwkv = min(64, t)
    n_rwkv = 4 if t % (4 * c_rwkv) == 0 else 1
    new_gla, new_rwkv, new_shift, new_hgrn = [], [], [], []
    h = _norm(x, wt['norm_mix_pre'][0], tm)
    for l in range(DEPTH):
        oa, g1 = _gla_mixer(h, wt['w_gla'][l], s_gla[l], wt['gup'][l], wt['gbias'][l],
                            wt['gnorm'][l], b, t, c_glr)
        rw = [wt[k][l] for k in ('mu', 'w0', 'wup', 'a0', 'aup', 'g_up', 'k_k', 'k_a', 'rk',
                                 'ln_w', 'ln_b')]
        ob, sh1, r1 = _rwkv_mixer(h, wt['w_rwkv'][l], s_shift[l], s_rwkv[l], rw, b, t, c_rwkv, n_rwkv)
        oc, h1 = _hgrn_mixer(h, wt['w_hgrn'][l], s_hgrn[l], wt['lb_logits'], wt['hnorm'][l],
                             b, t, c_glr, l)
        x, h = _out_proj(oa, ob, oc, wt['wo_a'][l], wt['wo_b'][l], wt['wo_c'][l], x,
                         wt['norm_mix_post'][l], wt['norm_ffn_pre'][l], tm)
        x, h = _ffn(h, x, wt['ffn_w_gate'], wt['ffn_w_up'], wt['ffn_w_down'], l,
                    wt['norm_ffn_post'][l], wt['norm_mix_pre'][(l + 1) % DEPTH], tm)
        new_gla.append(g1)
        new_rwkv.append(r1)
        new_shift.append(sh1)
        new_hgrn.append(h1)
    return x, jnp.stack(new_gla), jnp.stack(new_rwkv), jnp.stack(new_shift), jnp.stack(new_hgrn)


def _run_path(x, state_gla, state_rwkv, state_shift, state_hgrn, wt):
    b, t, d = x.shape
    y, g, r, sh, h = _trunk(
        x.reshape(b * t, d), _gla_state_in(state_gla), _rwkv_state_in(state_rwkv),
        _rwkv_cols_in(state_shift)[:, :, None, :], jnp.swapaxes(state_hgrn, -1, -2), wt, b, t)
    return (y.reshape(b, t, d), _gla_state_out(g), _rwkv_state_out(r),
            _rwkv_cols_out(sh[:, :, 0, :]), jnp.swapaxes(h, -1, -2))


def kernel(x_prompt, x_sample, state_gla, state_rwkv, state_rwkv_shift, state_hgrn, norm_mix_pre, norm_mix_post, norm_ffn_pre, norm_ffn_post, w_in, gla_gate_up, gla_gate_bias, gla_norm, rwkv_mu, rwkv_w0, rwkv_w_up, rwkv_a0, rwkv_a_up, rwkv_g_up, rwkv_k_k, rwkv_k_a, rwkv_r_k, rwkv_ln_w, rwkv_ln_b, hgrn_lb_logits, hgrn_norm, w_out, ffn_w_gate, ffn_w_up, ffn_w_down):
    wt = _build_weights(norm_mix_pre, norm_mix_post, norm_ffn_pre, norm_ffn_post, w_in, gla_gate_up,
                        gla_gate_bias, gla_norm, rwkv_mu, rwkv_w0, rwkv_w_up, rwkv_a0, rwkv_a_up,
                        rwkv_g_up, rwkv_k_k, rwkv_k_a, rwkv_r_k, rwkv_ln_w, rwkv_ln_b,
                        hgrn_lb_logits, hgrn_norm, w_out, ffn_w_gate, ffn_w_up, ffn_w_down)
    bp = x_prompt.shape[0]
    zeros = lambda s: jnp.zeros((DEPTH, bp) + s.shape[2:], x_prompt.dtype)
    y_p, gla_p, rwkv_p, shift_p, hgrn_p = _run_path(
        x_prompt, zeros(state_gla), zeros(state_rwkv), zeros(state_rwkv_shift), zeros(state_hgrn), wt)
    y_s, gla_s, rwkv_s, shift_s, hgrn_s = _run_path(
        x_sample, state_gla, state_rwkv, state_rwkv_shift, state_hgrn, wt)
    return (y_p, y_s, gla_p, rwkv_p, shift_p, hgrn_p, gla_s, rwkv_s, shift_s, hgrn_s)
```

```python
import functools

import jax
import jax.numpy as jnp
from jax import lax
from jax.experimental import pallas as pl
from jax.experimental.pallas import tpu as pltpu

F32 = jnp.float32
BF16 = jnp.bfloat16

D_MODEL = 2048
DEPTH = 4
EPS = 1e-6

GLA_HEADS, GLA_DK, GLA_DV = 4, 96, 192
GLA_DKP, GLA_DVP = 128, 256
GLA_RANK, GLA_RANKP = 16, 128
GLA_GATE_NORMALIZER = 16.0
GLA_QW = GLA_HEADS * GLA_DKP
GLA_VW = GLA_HEADS * GLA_DVP
GLA_COLS_P = 2 * GLA_QW + 2 * GLA_VW + GLA_RANKP

RWKV_HEADS, RWKV_HD = 10, 64
RWKV_WIDTH = RWKV_HEADS * RWKV_HD
RWKV_PAIRS = RWKV_HEADS // 2
RWKV_COLS = 2176
RWKV_GN_EPS = 64e-5

HGRN_HEADS, HGRN_D = 5, 128
HGRN_WIDTH = HGRN_HEADS * HGRN_D
HGRN_COLS = 4 * HGRN_WIDTH

D_FF = 5632
FFN_TF = 512
LANE = 128
LOG2_E = 1.4426950408889634

VMEM_LIMIT = 56 * 1024 * 1024


def _sigmoid(x):
    return 1.0 / (1.0 + jnp.exp(-x))


def _silu(x):
    return x * _sigmoid(x)


def _log_sigmoid(x):
    return jnp.minimum(x, 0.0) - jnp.log(1.0 + jnp.exp(-jnp.abs(x)))


def _dot(a, b):
    return jnp.dot(a.astype(BF16), b.astype(BF16), preferred_element_type=F32)


def _dot_nt(a, b):
    return lax.dot_general(a.astype(BF16), b.astype(BF16), (((1,), (1,)), ((), ())),
                           preferred_element_type=F32)


def _dot_tn(a, b):
    return lax.dot_general(a.astype(BF16), b.astype(BF16), (((0,), (0,)), ((), ())),
                           preferred_element_type=F32)


def _split_bf16(x, parts):
    out = []
    r = x
    for i in range(parts):
        p = r.astype(BF16)
        out.append(p)
        if i + 1 < parts:
            r = r - p.astype(F32)
    return out


def _dot_exact_lhs(m_bf16, x, parts=3):
    acc = None
    for p in _split_bf16(x, parts):
        t = jnp.dot(m_bf16, p, preferred_element_type=F32)
        acc = t if acc is None else acc + t
    return acc


def _iota(shape, dim):
    return lax.broadcasted_iota(jnp.int32, shape, dim)


def _tril_ones(c):
    return jnp.where(_iota((c, c), 0) >= _iota((c, c), 1), 1.0, 0.0).astype(BF16)


def _rms_rows(x, g):
    ms = jnp.mean(x * x, axis=-1, keepdims=True)
    return x * lax.rsqrt(ms + EPS) * g


def _compiler_params(sem):
    return pltpu.CompilerParams(dimension_semantics=sem, vmem_limit_bytes=VMEM_LIMIT)


def _norm_kernel(x_ref, g_ref, h_ref):
    h_ref[...] = _rms_rows(x_ref[...], g_ref[...]).astype(BF16)


def _norm(x, g, tm):
    n, d = x.shape
    return pl.pallas_call(
        _norm_kernel,
        grid=(n // tm,),
        in_specs=[pl.BlockSpec((tm, d), lambda i: (i, 0)), pl.BlockSpec((1, d), lambda i: (0, 0))],
        out_specs=pl.BlockSpec((tm, d), lambda i: (i, 0)),
        out_shape=jax.ShapeDtypeStruct((n, d), BF16),
        compiler_params=_compiler_params(("arbitrary",)),
    )(x, g)


def _out_proj_kernel(oa_ref, ob_ref, oc_ref, wa_ref, wb_ref, wc_ref, x_ref, g_ref, gnext_ref,
                     y_ref, hn_ref):
    half = x_ref.shape[0] // 2
    for r in (slice(0, half), slice(half, 2 * half)):
        m = (jnp.dot(oa_ref[r, :], wa_ref[...], preferred_element_type=F32)
             + jnp.dot(ob_ref[r, :], wb_ref[...], preferred_element_type=F32)
             + jnp.dot(oc_ref[r, :], wc_ref[...], preferred_element_type=F32))
        y = x_ref[r, :] + _rms_rows(m, g_ref[...])
        y_ref[r, :] = y
        hn_ref[r, :] = _rms_rows(y, gnext_ref[...]).astype(BF16)


def _out_proj(oa, ob, oc, wa, wb, wc, x, g, gnext, tm):
    n, d = x.shape
    row = lambda w: pl.BlockSpec((tm, w), lambda i: (i, 0))
    full = lambda a: pl.BlockSpec(a.shape, lambda i: (0, 0))
    return pl.pallas_call(
        _out_proj_kernel,
        grid=(n // tm,),
        in_specs=[row(oa.shape[1]), row(ob.shape[1]), row(oc.shape[1]),
                  full(wa), full(wb), full(wc), row(d), full(g), full(gnext)],
        out_specs=[row(d), row(d)],
        out_shape=[jax.ShapeDtypeStruct((n, d), F32), jax.ShapeDtypeStruct((n, d), BF16)],
        compiler_params=_compiler_params(("arbitrary",)),
    )(oa, ob, oc, wa, wb, wc, x, g, gnext)


def _ffn_kernel(h_ref, x_ref, wg_ref, wu_ref, wd_ref, gpost_ref, gnext_ref, y_ref, hn_ref, acc_scr):
    j = pl.program_id(1)

    @pl.when(j == 0)
    def _():
        acc_scr[...] = jnp.zeros_like(acc_scr)

    h = h_ref[...]
    a = jnp.dot(h, wg_ref[...], preferred_element_type=F32)
    u = jnp.dot(h, wu_ref[...], preferred_element_type=F32)
    acc_scr[...] += jnp.dot((_silu(a) * u).astype(BF16), wd_ref[...], preferred_element_type=F32)

    @pl.when(j == pl.num_programs(1) - 1)
    def _():
        y = x_ref[...] + _rms_rows(acc_scr[...], gpost_ref[...])
        y_ref[...] = y
        hn_ref[...] = _rms_rows(y, gnext_ref[...]).astype(BF16)


def _ffn(h, x, wg, wu, wd, layer, gpost, gnext, tm):
    n, d = x.shape
    dff = wg.shape[2]
    tf = FFN_TF
    return pl.pallas_call(
        _ffn_kernel,
        grid=(n // tm, dff // tf),
        in_specs=[pl.BlockSpec((tm, d), lambda i, j: (i, 0)),
                  pl.BlockSpec((tm, d), lambda i, j: (i, 0)),
                  pl.BlockSpec((None, d, tf), lambda i, j: (layer, 0, j)),
                  pl.BlockSpec((None, d, tf), lambda i, j: (layer, 0, j)),
                  pl.BlockSpec((None, tf, d), lambda i, j: (layer, j, 0)),
                  pl.BlockSpec((1, d), lambda i, j: (0, 0)),
                  pl.BlockSpec((1, d), lambda i, j: (0, 0))],
        out_specs=[pl.BlockSpec((tm, d), lambda i, j: (i, 0)),
                   pl.BlockSpec((tm, d), lambda i, j: (i, 0))],
        out_shape=[jax.ShapeDtypeStruct((n, d), F32), jax.ShapeDtypeStruct((n, d), BF16)],
        scratch_shapes=[pltpu.VMEM((tm, d), F32)],
        compiler_params=_compiler_params(("arbitrary", "arbitrary")),
    )(h, x, wg, wu, wd, gpost, gnext)


class _GlrMasks:
    def __init__(self, c):
        self.c = c
        row_a = _iota((c, c), 0)
        col_a = _iota((c, c), 1)
        self.tril = _tril_ones(c)
        self.level = {}
        h = 1
        while 2 * h <= c:
            self.level[h] = (((row_a // (2 * h)) == (col_a // (2 * h)))
                             & ((row_a // h) % 2 == 1) & ((col_a // h) % 2 == 0))
            h *= 2
        self.eye = row_a == col_a


def _small_level_factors(g):
    c = g.shape[0]
    r4 = _iota(g.shape, 0) % 4
    g_prev = pltpu.roll(g, 1, 0)
    g_next = pltpu.roll(g, c - 1, 0)
    e1 = jnp.exp2(jnp.where(r4 % 2 == 1, g, 0.0))
    e2 = jnp.exp2(jnp.where(r4 == 3, g + g_prev, jnp.where(r4 == 2, g, jnp.where(r4 == 0, g_next, 0.0))))
    return e1, e2


def _glr_chunk(q, k, v, b, e1, e2, st, mk):
    c, dk = q.shape
    att = jnp.where(mk.eye, _dot_nt(q, k), 0.0)
    for h, pair_mask in mk.level.items():
        if h == 1:
            e = e1
        elif h == 2:
            e = e2
        else:
            n = c // (2 * h)
            ref = b.reshape(n, 2 * h, dk)[:, h - 1:h, :]
            refb = jnp.broadcast_to(ref, (n, 2 * h, dk)).reshape(c, dk)
            e = jnp.exp2(-jnp.abs(b - refb))
        att = jnp.where(pair_mask, _dot_nt(q * e, k * e), att)
    o = _dot(att, v) + _dot_nt(q * jnp.exp2(b), st)
    b_end = b[c - 1:c, :]
    st_new = st * jnp.exp2(b_end) + _dot_tn(v, k * jnp.exp2(b_end - b))
    return o, st_new


def _ahead_groups(cols, slots):
    tiles = -(-cols // (2 * LANE))
    bounds = [min(cols, 2 * LANE * ((k * tiles) // slots)) for k in range(slots)] + [cols]
    return [slice(bounds[k], bounds[k + 1]) for k in range(slots)]


MAX_WHOLE_CALL_ROWS = 1024


def _input_row_specs(b, t, rows, d, nsub):
    if nsub == 1:
        assert b * t <= MAX_WHOLE_CALL_ROWS, (b, t)
        whole = pl.BlockSpec((b * t, d), lambda i, j: (0, 0))
        return whole, whole
    nc = t // rows
    nblk = b * nc
    return (pl.BlockSpec((rows, d), lambda i, j: (i * nc + j, 0)),
            pl.BlockSpec((rows, d), lambda i, j: (jnp.minimum(i * nc + j + 1, nblk - 1), 0)))


def _first_projection(hc_ref, w_ref, pa_scr, c, nsub, first_rows):
    i, ci = pl.program_id(0), pl.program_id(1)
    if nsub > 1:
        @pl.when(ci == 0)
        def _():
            pa_scr[...] = jnp.dot(hc_ref[0:first_rows, :], w_ref[...], preferred_element_type=F32)
        return 0

    @pl.when(jnp.logical_and(i == 0, ci == 0))
    def _():
        pa_scr[...] = jnp.dot(hc_ref[...], w_ref[...], preferred_element_type=F32)
    return pl.multiple_of((i * pl.num_programs(1) + ci) * c, c)


def _gla_kernel(hc_ref, hn_ref, w_ref, s0_ref, gup_ref, gbias_ref, gnorm_ref, o_ref, sout_ref,
                s_scr, pa_scr, pb_scr, *, c, nsub):
    ci = pl.program_id(1)
    half = max(nsub // 2, 1)
    hrows = c * half
    groups = _ahead_groups(GLA_COLS_P, GLA_HEADS * half)

    @pl.when(ci == 0)
    def _():
        s_scr[...] = s0_ref[0]

    row0 = _first_projection(hc_ref, w_ref, pa_scr, c, nsub, hrows)

    mk = _GlrMasks(c)
    gnorm = gnorm_ref[...]
    for i in range(nsub):
        rs = slice(i * c, (i + 1) * c)
        src = pa_scr if i < half else pb_scr
        p = src[(i % half) * c:(i % half + 1) * c, :] if nsub > 1 else pa_scr[pl.ds(row0, c), :]
        q_all = p[:, 0:GLA_QW] * (GLA_DK ** -0.5)
        k_all = p[:, GLA_QW:2 * GLA_QW]
        v_all = p[:, 2 * GLA_QW:2 * GLA_QW + GLA_VW]
        gout = p[:, 2 * GLA_QW + GLA_VW:2 * GLA_QW + 2 * GLA_VW]
        gdown = p[:, 2 * GLA_QW + 2 * GLA_VW:]
        g_all = (_log_sigmoid(_dot(gdown, gup_ref[...]) + gbias_ref[...])
                 * (LOG2_E / GLA_GATE_NORMALIZER))
        b_all = _dot_exact_lhs(mk.tril, g_all)
        e1_all, e2_all = _small_level_factors(g_all)
        for h in range(GLA_HEADS):
            ks = slice(h * GLA_DKP, (h + 1) * GLA_DKP)
            vs = slice(h * GLA_DVP, (h + 1) * GLA_DVP)
            o, st = _glr_chunk(q_all[:, ks], k_all[:, ks], v_all[:, vs], b_all[:, ks],
                               e1_all[:, ks], e2_all[:, ks], s_scr[h], mk)
            s_scr[h] = st
            ms = jnp.sum(o * o, axis=-1, keepdims=True) * (1.0 / GLA_DV)
            o = o * lax.rsqrt(ms + EPS) * gnorm * _silu(gout[:, vs])
            o_ref[rs, vs] = o.astype(BF16)
            if nsub > 1:
                g = groups[(i % half) * GLA_HEADS + h]
                ahead, dst = (hc_ref[hrows:, :], pb_scr) if i < half else (hn_ref[0:hrows, :], pa_scr)
                dst[:, g] = jnp.dot(ahead, w_ref[:, g], preferred_element_type=F32)

    @pl.when(ci == pl.num_programs(1) - 1)
    def _():
        sout_ref[0] = s_scr[...]


def _gla_mixer(hx, w, s0t, gup, gbias, gnorm, b, t, c):
    nsub = 4 if t % (4 * c) == 0 else 1
    rows = c * nsub
    hrows = c * max(nsub // 2, 1)
    nc = t // rows
    kern = functools.partial(_gla_kernel, c=c, nsub=nsub)
    full = lambda a: pl.BlockSpec(a.shape, lambda i, j: (0,) * a.ndim)
    st_spec = pl.BlockSpec((1, GLA_HEADS, GLA_DVP, GLA_DKP), lambda i, j: (i, 0, 0, 0))
    d = hx.shape[1]
    return pl.pallas_call(
        kern,
        grid=(b, nc),
        in_specs=list(_input_row_specs(b, t, rows, d, nsub)) + [
                  pl.BlockSpec(w.shape, lambda i, j: (0, 0), pipeline_mode=pl.Buffered(1)),
                  st_spec, full(gup), full(gbias), full(gnorm)],
        out_specs=[pl.BlockSpec((rows, GLA_VW), lambda i, j: (i * nc + j, 0)), st_spec],
        out_shape=[jax.ShapeDtypeStruct((b * t, GLA_VW), BF16),
                   jax.ShapeDtypeStruct((b, GLA_HEADS, GLA_DVP, GLA_DKP), F32)],
        scratch_shapes=[pltpu.VMEM((GLA_HEADS, GLA_DVP, GLA_DKP), F32),
                        pltpu.VMEM((hrows if nsub > 1 else b * t, GLA_COLS_P), F32),
                        pltpu.VMEM((hrows, GLA_COLS_P), F32)],
        compiler_params=_compiler_params(("arbitrary", "arbitrary")),
    )(hx, hx, w, s0t, gup, gbias, gnorm)


def _hgrn_kernel(hc_ref, hn_ref, w_ref, s0_ref, lbl_ref, hnorm_ref, o_ref, sout_ref,
                 s_scr, pa_scr, pb_scr, *, c, nsub, layer):
    ci = pl.program_id(1)
    half = max(nsub // 2, 1)
    hrows = c * half
    gw = HGRN_COLS // (HGRN_HEADS * half)

    @pl.when(ci == 0)
    def _():
        s_scr[...] = s0_ref[0]

    row0 = _first_projection(hc_ref, w_ref, pa_scr, c, nsub, hrows)

    lg = lbl_ref[...]
    e = jnp.exp(lg - jnp.max(lg, axis=0, keepdims=True))
    prob = e / jnp.sum(e, axis=0, keepdims=True)
    lb = jnp.zeros((1, HGRN_WIDTH), F32)
    for i in range(1, layer + 1):
        lb = lb + prob[i:i + 1, :]

    mk = _GlrMasks(c)
    hnorm = hnorm_ref[...]

    for i in range(nsub):
        rs = slice(i * c, (i + 1) * c)
        src = pa_scr if i < half else pb_scr
        p = src[(i % half) * c:(i % half + 1) * c, :] if nsub > 1 else pa_scr[pl.ds(row0, c), :]
        hq = p[:, 0:HGRN_WIDTH]
        hf = p[:, HGRN_WIDTH:2 * HGRN_WIDTH]
        hi = p[:, 2 * HGRN_WIDTH:3 * HGRN_WIDTH]
        hg = p[:, 3 * HGRN_WIDTH:]
        q_all = _silu(hq)
        f_all = lb + (1.0 - lb) * _sigmoid(hf)
        k_all = 1.0 - f_all
        g_all = jnp.log2(f_all)
        b_all = _dot_exact_lhs(mk.tril, g_all)
        e1_all, e2_all = _small_level_factors(g_all)
        for h in range(HGRN_HEADS):
            s = slice(h * HGRN_D, (h + 1) * HGRN_D)
            o, st = _glr_chunk(q_all[:, s], k_all[:, s], hi[:, s], b_all[:, s],
                               e1_all[:, s], e2_all[:, s], s_scr[h], mk)
            s_scr[h] = st
            o = _rms_rows(o, hnorm) * _silu(hg[:, s])
            o_ref[rs, s] = o.astype(BF16)
            if nsub > 1:
                k = (i % half) * HGRN_HEADS + h
                g = slice(k * gw, (k + 1) * gw)
                ahead, dst = (hc_ref[hrows:, :], pb_scr) if i < half else (hn_ref[0:hrows, :], pa_scr)
                dst[:, g] = jnp.dot(ahead, w_ref[:, g], preferred_element_type=F32)

    @pl.when(ci == pl.num_programs(1) - 1)
    def _():
        sout_ref[0] = s_scr[...]


def _hgrn_mixer(hx, w, s0t, lb_logits, hnorm, b, t, c, layer):
    nsub = 4 if t % (4 * c) == 0 else 1
    rows = c * nsub
    hrows = c * max(nsub // 2, 1)
    nc = t // rows
    kern = functools.partial(_hgrn_kernel, c=c, nsub=nsub, layer=layer)
    full = lambda a: pl.BlockSpec(a.shape, lambda i, j: (0,) * a.ndim)
    st_spec = pl.BlockSpec((1, HGRN_HEADS, HGRN_D, HGRN_D), lambda i, j: (i, 0, 0, 0))
    d = hx.shape[1]
    return pl.pallas_call(
        kern,
        grid=(b, nc),
        in_specs=list(_input_row_specs(b, t, rows, d, nsub)) + [
                  pl.BlockSpec(w.shape, lambda i, j: (0, 0), pipeline_mode=pl.Buffered(1)),
                  st_spec, full(lb_logits), full(hnorm)],
        out_specs=[pl.BlockSpec((rows, HGRN_WIDTH), lambda i, j: (i * nc + j, 0)), st_spec],
        out_shape=[jax.ShapeDtypeStruct((b * t, HGRN_WIDTH), BF16),
                   jax.ShapeDtypeStruct((b, HGRN_HEADS, HGRN_D, HGRN_D), F32)],
        scratch_shapes=[pltpu.VMEM((HGRN_HEADS, HGRN_D, HGRN_D), F32),
                        pltpu.VMEM((hrows if nsub > 1 else b * t, HGRN_COLS), F32),
                        pltpu.VMEM((hrows, HGRN_COLS), F32)],
        compiler_params=_compiler_params(("arbitrary", "arbitrary")),
    )(hx, hx, w, s0t, lb_logits, hnorm)


def _rwkv_kernel(hc_ref, hn_ref, w_ref, shift0_ref, s0_ref, mu_ref, w0_ref, wup_ref, a0_ref,
                 aup_ref, gup_ref, kk_ref, ka_ref, rk_ref, lnw_ref, lnb_ref,
                 o_ref, shift_ref, sout_ref, s_scr, carry_scr, pa_scr, pb_scr, *, c, nsub):
    ci = pl.program_id(1)
    w = RWKV_WIDTH
    rows = c * nsub
    grp = min(nsub, 2)
    grows = c * grp

    @pl.when(ci == 0)
    def _():
        s_scr[...] = s0_ref[0]
        carry_scr[...] = shift0_ref[0]

    row0 = _first_projection(hc_ref, w_ref, pa_scr, c, nsub, grows)

    first_prev = carry_scr[...]
    grp_count = rows // grows
    col_tiles = _ahead_groups(RWKV_COLS, -(-RWKV_COLS // (2 * LANE)))

    c2 = 2 * c
    lane = _iota((c, LANE), 1)
    lo_half = lane < RWKV_HD
    ri = _iota((c2, c2), 0)
    cj = _iota((c2, c2), 1)
    strict = (ri % c) > (cj % c)
    incl = (ri % c) >= (cj % c)
    eye = jnp.where(ri == cj, 1.0, 0.0)
    inv_levels = []
    blk = 1
    while blk < c:
        inv_levels.append(((ri // (2 * blk)) == (cj // (2 * blk)))
                          & ((ri // blk) % 2 == 1) & ((cj // blk) % 2 == 0))
        blk *= 2
    tr, tc = _iota((grows, grows), 0), _iota((grows, grows), 1)
    tril = jnp.where((tr >= tc) & (tr // c == tc // c), 1.0, 0.0).astype(BF16)
    row = _iota((grows, RWKV_COLS), 0)
    seg = jnp.where((_iota((LANE, LANE), 0) // RWKV_HD) == (_iota((LANE, LANE), 1) // RWKV_HD),
                    1.0, 0.0).astype(BF16)

    def stack(x):
        return jnp.concatenate([jnp.where(lo_half, x, 0.0), jnp.where(lo_half, 0.0, x)], axis=0)

    def head_sum(x):
        return jnp.concatenate([_dot(x[:, j * LANE:(j + 1) * LANE], seg)
                                for j in range(RWKV_PAIRS)], axis=1)

    def prepare(gi):
        g0 = gi * grows
        src = pb_scr if gi else pa_scr
        p = src[...] if nsub > 1 else pa_scr[pl.ds(row0, grows), :]
        before = pa_scr[grows - 1:grows, :] if gi else first_prev
        if nsub == 1:
            todo = iter(())
        elif gi == 0:
            todo = iter([(hc_ref, slice(grows, 2 * grows), pb_scr, g) for g in col_tiles])
        else:
            todo = iter([(hn_ref, slice(0, grows), pa_scr, g) for g in col_tiles])

        def project_one():
            t = next(todo, None)
            if t is not None:
                h_ref, h_rows, dst, g = t
                dst[:, g] = jnp.dot(h_ref[h_rows, :], w_ref[:, g], preferred_element_type=F32)

        prev = jnp.where(row == 0, before, pltpu.roll(p, 1, 0))
        pm = p + (prev - p) * mu_ref[...]
        project_one()
        r_all = pm[:, 0:w]
        k_all = pm[:, w:2 * w]
        v_all = pm[:, 2 * w:3 * w]
        wa = pm[:, 3 * w:3 * w + LANE]
        gd = pm[:, 3 * w + LANE:]
        lw_all = -jnp.exp(_log_sigmoid(w0_ref[...] + _dot(jnp.tanh(wa), wup_ref[...])) - 0.5)
        a_all = _sigmoid(a0_ref[...] + _dot(wa, aup_ref[...]))
        project_one()
        gate_all = _dot(_sigmoid(gd), gup_ref[...])
        kk_all = k_all * kk_ref[...]
        kk_all = kk_all * lax.rsqrt(jnp.maximum(head_sum(kk_all * kk_all), 1e-24))
        project_one()
        k_all = k_all * (1.0 + (a_all - 1.0) * ka_ref[...])
        beta_all = kk_all * a_all
        bonus_all = head_sum(r_all * k_all * rk_ref[...]) * v_all
        project_one()
        cin_all = _dot_exact_lhs(tril, lw_all, parts=2)
        ginv = jnp.exp(-cin_all)
        kap_all = kk_all * jnp.exp(cin_all - lw_all)
        project_one()
        rt_all = r_all * jnp.exp(cin_all)
        kt_all = k_all * ginv
        bt_all = beta_all * ginv
        out = []
        for i in range(grp):
            rs = slice(i * c, (i + 1) * c)
            cend = cin_all[(i + 1) * c - 1:(i + 1) * c, :]
            gend = jnp.exp(cend - cin_all[rs])
            kend, bend, dec = k_all[rs] * gend, beta_all[rs] * gend, jnp.exp(cend)
            for j in range(RWKV_PAIRS):
                s = slice(j * LANE, (j + 1) * LANE)
                out.append(dict(
                    s=s, rs=slice(g0 + i * c, g0 + (i + 1) * c),
                    kap_s=stack(kap_all[rs, s]), r_s=stack(rt_all[rs, s]), k_s=stack(kt_all[rs, s]),
                    b_s=stack(bt_all[rs, s]), v_s=stack(v_all[rs, s]), kend_s=stack(kend[:, s]),
                    bend_s=stack(bend[:, s]), dec=dec[:, s], bonus=bonus_all[rs, s],
                    gate=gate_all[rs, s]))
                if j % 2 == 0:
                    project_one()
        for _ in col_tiles:
            project_one()
        if gi == grp_count - 1:
            last = p[grows - 1:grows, :]
            carry_scr[...] = last
            shift_ref[0] = last
        return out

    def dot_nt2(lhs, r1, r2):
        if r1.shape[0] % LANE:
            return _dot_nt(lhs, r1), _dot_nt(lhs, r2)
        both = _dot_nt(lhs, jnp.concatenate([r1, r2], axis=0))
        return both[:, :r1.shape[0]], both[:, r1.shape[0]:]

    def dot2(lhs, r1, r2):
        both = _dot(lhs, jnp.concatenate([r1, r2], axis=1))
        return both[:, :r1.shape[1]], both[:, r1.shape[1]:]

    pairs = []
    for gi in range(grp_count):
        group = prepare(gi)
        for pr in group:
            a1, a2 = dot_nt2(pr['kap_s'], pr['k_s'], pr['b_s'])
            pr['a1'], pr['a2'] = jnp.where(strict, a1, 0.0), jnp.where(strict, a2, 0.0)
        for pr in group:
            a3, a4 = dot_nt2(pr['r_s'], pr['k_s'], pr['b_s'])
            pr['a3'], pr['a4'] = jnp.where(incl, a3, 0.0), jnp.where(incl, a4, 0.0)
        for pr in group:
            pr['a1v'] = _dot(pr['a1'], pr['v_s'])
            pr['a3v'] = _dot(pr['a3'], pr['v_s'])
            pr['kv'] = _dot_tn(pr['v_s'], pr['kend_s'])
        pairs += group
    if c2 % LANE == 0 and len(pairs) % 2 == 0:
        left = _iota((c2, 2 * c2), 1) < c2
        tile2 = lambda x: jnp.concatenate([x, x], axis=1)

        def block_diag(x, keep=None):
            top = left if keep is None else keep & left
            bot = ~left if keep is None else keep & ~left
            x = x.astype(BF16)
            return jnp.concatenate([jnp.where(top, x, 0.0), jnp.where(bot, x, 0.0)], axis=0)

        a2w = [jnp.concatenate([pa['a2'], pb['a2']], axis=1) for pa, pb in zip(pairs[0::2], pairs[1::2])]
        tw = [tile2(eye) - jnp.where(tile2(inv_levels[0]), a, 0.0) for a in a2w]
        for m in inv_levels[1:]:
            mw = tile2(m)
            half = [_dot(t, block_diag(a, mw)) for t, a in zip(tw, a2w)]
            tw = [t - _dot(hf, block_diag(t)) for t, hf in zip(tw, half)]
        tinv = [x for t in tw for x in (t[:, :c2], t[:, c2:])]
    else:
        tinv = [eye - jnp.where(inv_levels[0], pr['a2'], 0.0) for pr in pairs]
        for m in inv_levels[1:]:
            half = [_dot(t, jnp.where(m, pr['a2'], 0.0)) for t, pr in zip(tinv, pairs)]
            tinv = [t - _dot(hf, t) for t, hf in zip(tinv, half)]
    for pr, tj in zip(pairs, tinv):
        pr['tk'], pr['tv'] = dot2(tj, pr['kap_s'], pr['a1v'])
    for pr in pairs:
        a4tk, a4tv = dot2(pr['a4'], pr['tk'], pr['tv'])
        pr['reff'] = pr['r_s'] - a4tk
        pr['oc'] = pr['a3v'] - a4tv
        pr['m'] = _dot_tn(pr['tk'], pr['bend_s'])
        pr['q'] = pr['kv'] - _dot_tn(pr['tv'], pr['bend_s'])
    sts = [s_scr[j] for j in range(RWKV_PAIRS)]
    os_ = []
    for i in range(nsub):
        sub = list(zip(pairs[i * RWKV_PAIRS:(i + 1) * RWKV_PAIRS], sts))
        o2s = [_dot_nt(pr['reff'], st) + pr['oc'] for pr, st in sub]
        sts = [st * pr['dec'] - _dot(st, pr['m']) + pr['q'] for pr, st in sub]
        os_ += [o2[0:c] + o2[c:c2] for o2 in o2s]
    for j, st in enumerate(sts):
        s_scr[j] = st
    ds = [o - _dot(o, seg) * (1.0 / RWKV_HD) for o in os_]
    vars_ = [_dot(d * d, seg) * (1.0 / RWKV_HD) for d in ds]
    for pr, d, var in zip(pairs, ds, vars_):
        s, rs = pr['s'], pr['rs']
        on = d * lax.rsqrt(var + RWKV_GN_EPS) * lnw_ref[:, s] + lnb_ref[:, s]
        o_ref[rs, s] = ((on + pr['bonus']) * pr['gate']).astype(BF16)

    @pl.when(ci == pl.num_programs(1) - 1)
    def _():
        sout_ref[0] = s_scr[...]


def _rwkv_mixer(hx, w, shift0, s0, wts, b, t, c, nsub):
    rows = c * nsub
    grows = c * min(nsub, 2)
    nc = t // rows
    d = hx.shape[1]
    kern = functools.partial(_rwkv_kernel, c=c, nsub=nsub)
    full = lambda a: pl.BlockSpec(a.shape, lambda i, j: (0,) * a.ndim)
    st_spec = pl.BlockSpec((1, RWKV_PAIRS, LANE, LANE), lambda i, j: (i, 0, 0, 0))
    sh_spec = pl.BlockSpec((1, 1, RWKV_COLS), lambda i, j: (i, 0, 0))
    return pl.pallas_call(
        kern,
        grid=(b, nc),
        in_specs=list(_input_row_specs(b, t, rows, d, nsub)) + [
                  pl.BlockSpec(w.shape, lambda i, j: (0, 0), pipeline_mode=pl.Buffered(1)),
                  sh_spec, st_spec] + [full(a) for a in wts],
        out_specs=[pl.BlockSpec((rows, RWKV_WIDTH), lambda i, j: (i * nc + j, 0)), sh_spec, st_spec],
        out_shape=[jax.ShapeDtypeStruct((b * t, RWKV_WIDTH), BF16),
                   jax.ShapeDtypeStruct((b, 1, RWKV_COLS), F32),
                   jax.ShapeDtypeStruct((b, RWKV_PAIRS, LANE, LANE), F32)],
        scratch_shapes=[pltpu.VMEM((RWKV_PAIRS, LANE, LANE), F32), pltpu.VMEM((1, RWKV_COLS), F32),
                        pltpu.VMEM((grows if nsub > 1 else b * t, RWKV_COLS), F32),
                        pltpu.VMEM((grows, RWKV_COLS), F32)],
        compiler_params=_compiler_params(("arbitrary", "arbitrary")),
    )(hx, hx, w, shift0, s0, *wts)


def _rwkv_cols_in(a):
    return jnp.concatenate([a[..., 0:640], a[..., 704:1984], a[..., 640:704], a[..., 1984:2176]], axis=-1)


def _rwkv_cols_out(a):
    return jnp.concatenate([a[..., 0:640], a[..., 1920:1984], a[..., 640:1920], a[..., 1984:2176]], axis=-1)


def _pad_heads(w, heads, d, dp):
    lead = w.shape[:-1]
    w = w.reshape(lead + (heads, d))
    w = jnp.pad(w, [(0, 0)] * len(lead) + [(0, 0), (0, dp - d)])
    return w.reshape(lead + (heads * dp,))


def _prep_weights(w_in, gla_gate_up, gla_gate_bias, gla_norm, rwkv_mu, rwkv_w_up, rwkv_a_up,
                  rwkv_r_k, w_out):
    wb = w_in.astype(BF16)
    zeros = lambda n: jnp.zeros(wb.shape[:-1] + (n,), BF16)

    def padded_heads(start, d, dp):
        return [piece for h in range(GLA_HEADS)
                for piece in (wb[..., start + h * d:start + (h + 1) * d], zeros(dp - d))]

    w_gla = jnp.concatenate(
        padded_heads(0, GLA_DK, GLA_DKP) + padded_heads(384, GLA_DK, GLA_DKP)
        + padded_heads(768, GLA_DV, GLA_DVP) + padded_heads(1552, GLA_DV, GLA_DVP)
        + [wb[..., 1536:1552], zeros(GLA_RANKP - GLA_RANK)], axis=-1)
    w_rwkv = _rwkv_cols_in(wb[..., 2320:2320 + RWKV_COLS])
    w_hgrn = wb[..., 2320 + RWKV_COLS:]
    gup = jnp.pad(_pad_heads(gla_gate_up, GLA_HEADS, GLA_DK, GLA_DKP),
                  ((0, 0), (0, GLA_RANKP - GLA_RANK), (0, 0))).astype(BF16)
    gbias = _pad_heads(gla_gate_bias, GLA_HEADS, GLA_DK, GLA_DKP)[:, None, :]
    gnorm = jnp.pad(gla_norm, ((0, 0), (0, GLA_DVP - GLA_DV)))[:, None, :]
    mu = _rwkv_cols_in(rwkv_mu)[:, None, :]
    wup = jnp.pad(rwkv_w_up, ((0, 0), (0, 64), (0, 0))).astype(BF16)
    aup = jnp.pad(rwkv_a_up, ((0, 0), (64, 0), (0, 0))).astype(BF16)
    nl = w_in.shape[0]
    rk = rwkv_r_k.reshape(nl, 1, RWKV_WIDTH)
    wo_a = w_out[:, :768].reshape(nl, GLA_HEADS, GLA_DV, D_MODEL)
    wo_a = jnp.pad(wo_a, ((0, 0), (0, 0), (0, GLA_DVP - GLA_DV), (0, 0)))
    wo_a = wo_a.reshape(nl, GLA_VW, D_MODEL).astype(BF16)
    wo_b = w_out[:, 768:1408].astype(BF16)
    wo_c = w_out[:, 1408:].astype(BF16)
    return w_gla, w_rwkv, w_hgrn, gup, gbias, gnorm, mu, wup, aup, rk, wo_a, wo_b, wo_c


def _build_weights(norm_mix_pre, norm_mix_post, norm_ffn_pre, norm_ffn_post, w_in, gla_gate_up,
                   gla_gate_bias, gla_norm, rwkv_mu, rwkv_w0, rwkv_w_up, rwkv_a0, rwkv_a_up,
                   rwkv_g_up, rwkv_k_k, rwkv_k_a, rwkv_r_k, rwkv_ln_w, rwkv_ln_b,
                   hgrn_lb_logits, hgrn_norm, w_out, ffn_w_gate, ffn_w_up, ffn_w_down):
    names = ('w_gla', 'w_rwkv', 'w_hgrn', 'gup', 'gbias', 'gnorm', 'mu', 'wup', 'aup', 'rk',
             'wo_a', 'wo_b', 'wo_c')
    one = lambda a, l: a[l:l + 1]
    per_layer = [_prep_weights(one(w_in, l), one(gla_gate_up, l), one(gla_gate_bias, l),
                               one(gla_norm, l), one(rwkv_mu, l), one(rwkv_w_up, l),
                               one(rwkv_a_up, l), one(rwkv_r_k, l), one(w_out, l))
                 for l in range(DEPTH)]
    wt = {name: [per_layer[l][i][0] for l in range(DEPTH)] for i, name in enumerate(names)}
    vec = lambda a: a[:, None, :]
    wt.update(
        norm_mix_pre=vec(norm_mix_pre), norm_mix_post=vec(norm_mix_post),
        norm_ffn_pre=vec(norm_ffn_pre), norm_ffn_post=vec(norm_ffn_post),
        w0=vec(rwkv_w0), a0=vec(rwkv_a0), g_up=rwkv_g_up.astype(BF16),
        k_k=vec(rwkv_k_k), k_a=vec(rwkv_k_a), ln_w=vec(rwkv_ln_w), ln_b=vec(rwkv_ln_b),
        lb_logits=hgrn_lb_logits, hnorm=vec(hgrn_norm),
        ffn_w_gate=ffn_w_gate.astype(BF16), ffn_w_up=ffn_w_up.astype(BF16),
        ffn_w_down=ffn_w_down.astype(BF16))
    return wt


def _gla_state_in(s):
    s = jnp.swapaxes(s, -1, -2)
    return jnp.pad(s, [(0, 0)] * 3 + [(0, GLA_DVP - GLA_DV), (0, GLA_DKP - GLA_DK)])


def _gla_state_out(s):
    return jnp.swapaxes(s[..., :GLA_DV, :GLA_DK], -1, -2)


def _rwkv_state_in(s):
    d, b = s.shape[:2]
    s = s.reshape(d, b, RWKV_PAIRS, 2, RWKV_HD, RWKV_HD)
    z = jnp.zeros_like(s[:, :, :, 0])
    top = jnp.concatenate([s[:, :, :, 0], z], axis=-1)
    bot = jnp.concatenate([z, s[:, :, :, 1]], axis=-1)
    return jnp.concatenate([top, bot], axis=-2)


def _rwkv_state_out(s):
    d, b = s.shape[:2]
    out = jnp.stack([s[..., :RWKV_HD, :RWKV_HD], s[..., RWKV_HD:, RWKV_HD:]], axis=3)
    return out.reshape(d, b, RWKV_HEADS, RWKV_HD, RWKV_HD)


def _row_tile(n, cap):
    t = cap
    while n % t:
        t //= 2
    return t


def _trunk(x, s_gla, s_rwkv, s_shift, s_hgrn, wt, b, t):
    n = b * t
    tm = _row_tile(n, 512)
    c_glr = min(128, t)
    c_rwkv = min(64, t)
    n_rwkv = 4 if t % (4 * c_rwkv) == 0 else 1
    new_gla, new_rwkv, new_shift, new_hgrn = [], [], [], []
    h = _norm(x, wt['norm_mix_pre'][0], tm)
    for l in range(DEPTH):
        oa, g1 = _gla_mixer(h, wt['w_gla'][l], s_gla[l], wt['gup'][l], wt['gbias'][l],
                            wt['gnorm'][l], b, t, c_glr)
        rw = [wt[k][l] for k in ('mu', 'w0', 'wup', 'a0', 'aup', 'g_up', 'k_k', 'k_a', 'rk',
                                 'ln_w', 'ln_b')]
        ob, sh1, r1 = _rwkv_mixer(h, wt['w_rwkv'][l], s_shift[l], s_rwkv[l], rw, b, t, c_rwkv, n_rwkv)
        oc, h1 = _hgrn_mixer(h, wt['w_hgrn'][l], s_hgrn[l], wt['lb_logits'], wt['hnorm'][l],
                             b, t, c_glr, l)
        x, h = _out_proj(oa, ob, oc, wt['wo_a'][l], wt['wo_b'][l], wt['wo_c'][l], x,
                         wt['norm_mix_post'][l], wt['norm_ffn_pre'][l], tm)
        x, h = _ffn(h, x, wt['ffn_w_gate'], wt['ffn_w_up'], wt['ffn_w_down'], l,
                    wt['norm_ffn_post'][l], wt['norm_mix_pre'][(l + 1) % DEPTH], tm)
        new_gla.append(g1)
        new_rwkv.append(r1)
        new_shift.append(sh1)
        new_hgrn.append(h1)
    return x, jnp.stack(new_gla), jnp.stack(new_rwkv), jnp.stack(new_shift), jnp.stack(new_hgrn)


def _run_path(x, state_gla, state_rwkv, state_shift, state_hgrn, wt):
    b, t, d = x.shape
    y, g, r, sh, h = _trunk(
        x.reshape(b * t, d), _gla_state_in(state_gla), _rwkv_state_in(state_rwkv),
        _rwkv_cols_in(state_shift)[:, :, None, :], jnp.swapaxes(state_hgrn, -1, -2), wt, b, t)
    return (y.reshape(b, t, d), _gla_state_out(g), _rwkv_state_out(r),
            _rwkv_cols_out(sh[:, :, 0, :]), jnp.swapaxes(h, -1, -2))


def kernel(x_prompt, x_sample, state_gla, state_rwkv, state_rwkv_shift, state_hgrn, norm_mix_pre, norm_mix_post, norm_ffn_pre, norm_ffn_post, w_in, gla_gate_up, gla_gate_bias, gla_norm, rwkv_mu, rwkv_w0, rwkv_w_up, rwkv_a0, rwkv_a_up, rwkv_g_up, rwkv_k_k, rwkv_k_a, rwkv_r_k, rwkv_ln_w, rwkv_ln_b, hgrn_lb_logits, hgrn_norm, w_out, ffn_w_gate, ffn_w_up, ffn_w_down):
    wt = _build_weights(norm_mix_pre, norm_mix_post, norm_ffn_pre, norm_ffn_post, w_in, gla_gate_up,
                        gla_gate_bias, gla_norm, rwkv_mu, rwkv_w0, rwkv_w_up, rwkv_a0, rwkv_a_up,
                        rwkv_g_up, rwkv_k_k, rwkv_k_a, rwkv_r_k, rwkv_ln_w, rwkv_ln_b,
                        hgrn_lb_logits, hgrn_norm, w_out, ffn_w_gate, ffn_w_up, ffn_w_down)
    bp = x_prompt.shape[0]
    zeros = lambda s: jnp.zeros((DEPTH, bp) + s.shape[2:], x_prompt.dtype)
    y_p, gla_p, rwkv_p, shift_p, hgrn_p = _run_path(
        x_prompt, zeros(state_gla), zeros(state_rwkv), zeros(state_rwkv_shift), zeros(state_hgrn), wt)
    y_s, gla_s, rwkv_s, shift_s, hgrn_s = _run_path(
        x_sample, state_gla, state_rwkv, state_rwkv_shift, state_hgrn, wt)
    return (y_p, y_s, gla_p, rwkv_p, shift_p, hgrn_p, gla_s, rwkv_s, shift_s, hgrn_s)
```

```python
import functools

import jax
import jax.numpy as jnp
from jax import lax
from jax.experimental import pallas as pl
from jax.experimental.pallas import tpu as pltpu

F32 = jnp.float32
BF16 = jnp.bfloat16

D_MODEL = 2048
DEPTH = 4
EPS = 1e-6

GLA_HEADS, GLA_DK, GLA_DV = 4, 96, 192
GLA_DKP, GLA_DVP = 128, 256
GLA_RANK, GLA_RANKP = 16, 128
GLA_GATE_NORMALIZER = 16.0
GLA_QW = GLA_HEADS * GLA_DKP
GLA_VW = GLA_HEADS * GLA_DVP
GLA_COLS_P = 2 * GLA_QW + 2 * GLA_VW + GLA_RANKP

RWKV_HEADS, RWKV_HD = 10, 64
RWKV_WIDTH = RWKV_HEADS * RWKV_HD
RWKV_PAIRS = RWKV_HEADS // 2
RWKV_COLS = 2176
RWKV_GN_EPS = 64e-5

HGRN_HEADS, HGRN_D = 5, 128
HGRN_WIDTH = HGRN_HEADS * HGRN_D
HGRN_COLS = 4 * HGRN_WIDTH

D_FF = 5632
FFN_TF = 512
LANE = 128
LOG2_E = 1.4426950408889634

VMEM_LIMIT = 56 * 1024 * 1024


def _sigmoid(x):
    return 1.0 / (1.0 + jnp.exp(-x))


def _silu(x):
    return x * _sigmoid(x)


def _log_sigmoid(x):
    return jnp.minimum(x, 0.0) - jnp.log(1.0 + jnp.exp(-jnp.abs(x)))


def _dot(a, b):
    return jnp.dot(a.astype(BF16), b.astype(BF16), preferred_element_type=F32)


def _dot_nt(a, b):
    return lax.dot_general(a.astype(BF16), b.astype(BF16), (((1,), (1,)), ((), ())),
                           preferred_element_type=F32)


def _dot_tn(a, b):
    return lax.dot_general(a.astype(BF16), b.astype(BF16), (((0,), (0,)), ((), ())),
                           preferred_element_type=F32)


def _split_bf16(x, parts):
    out = []
    r = x
    for i in range(parts):
        p = r.astype(BF16)
        out.append(p)
        if i + 1 < parts:
            r = r - p.astype(F32)
    return out


def _dot_exact_lhs(m_bf16, x, parts=3):
    acc = None
    for p in _split_bf16(x, parts):
        t = jnp.dot(m_bf16, p, preferred_element_type=F32)
        acc = t if acc is None else acc + t
    return acc


def _iota(shape, dim):
    return lax.broadcasted_iota(jnp.int32, shape, dim)


def _tril_ones(c):
    return jnp.where(_iota((c, c), 0) >= _iota((c, c), 1), 1.0, 0.0).astype(BF16)


def _rms_rows(x, g):
    ms = jnp.mean(x * x, axis=-1, keepdims=True)
    return x * lax.rsqrt(ms + EPS) * g


def _compiler_params(sem):
    return pltpu.CompilerParams(dimension_semantics=sem, vmem_limit_bytes=VMEM_LIMIT)


def _norm_kernel(x_ref, g_ref, h_ref):
    h_ref[...] = _rms_rows(x_ref[...], g_ref[...]).astype(BF16)


def _norm(x, g, tm):
    n, d = x.shape
    return pl.pallas_call(
        _norm_kernel,
        grid=(n // tm,),
        in_specs=[pl.BlockSpec((tm, d), lambda i: (i, 0)), pl.BlockSpec((1, d), lambda i: (0, 0))],
        out_specs=pl.BlockSpec((tm, d), lambda i: (i, 0)),
        out_shape=jax.ShapeDtypeStruct((n, d), BF16),
        compiler_params=_compiler_params(("arbitrary",)),
    )(x, g)


def _out_proj_kernel(oa_ref, ob_ref, oc_ref, wa_ref, wb_ref, wc_ref, x_ref, g_ref, gnext_ref,
                     y_ref, hn_ref):
    half = x_ref.shape[0] // 2
    for r in (slice(0, half), slice(half, 2 * half)):
        m = (jnp.dot(oa_ref[r, :], wa_ref[...], preferred_element_type=F32)
             + jnp.dot(ob_ref[r, :], wb_ref[...], preferred_element_type=F32)
             + jnp.dot(oc_ref[r, :], wc_ref[...], preferred_element_type=F32))
        y = x_ref[r, :] + _rms_rows(m, g_ref[...])
        y_ref[r, :] = y
        hn_ref[r, :] = _rms_rows(y, gnext_ref[...]).astype(BF16)


def _out_proj(oa, ob, oc, wa, wb, wc, x, g, gnext, tm):
    n, d = x.shape
    row = lambda w: pl.BlockSpec((tm, w), lambda i: (i, 0))
    full = lambda a: pl.BlockSpec(a.shape, lambda i: (0, 0))
    return pl.pallas_call(
        _out_proj_kernel,
        grid=(n // tm,),
        in_specs=[row(oa.shape[1]), row(ob.shape[1]), row(oc.shape[1]),
                  full(wa), full(wb), full(wc), row(d), full(g), full(gnext)],
        out_specs=[row(d), row(d)],
        out_shape=[jax.ShapeDtypeStruct((n, d), F32), jax.ShapeDtypeStruct((n, d), BF16)],
        compiler_params=_compiler_params(("arbitrary",)),
    )(oa, ob, oc, wa, wb, wc, x, g, gnext)


def _ffn_kernel(h_ref, x_ref, wg_ref, wu_ref, wd_ref, gpost_ref, gnext_ref, y_ref, hn_ref, acc_scr):
    j = pl.program_id(1)

    @pl.when(j == 0)
    def _():
        acc_scr[...] = jnp.zeros_like(acc_scr)

    h = h_ref[...]
    a = jnp.dot(h, wg_ref[...], preferred_element_type=F32)
    u = jnp.dot(h, wu_ref[...], preferred_element_type=F32)
    acc_scr[...] += jnp.dot((_silu(a) * u).astype(BF16), wd_ref[...], preferred_element_type=F32)

    @pl.when(j == pl.num_programs(1) - 1)
    def _():
        y = x_ref[...] + _rms_rows(acc_scr[...], gpost_ref[...])
        y_ref[...] = y
        hn_ref[...] = _rms_rows(y, gnext_ref[...]).astype(BF16)


def _ffn(h, x, wg, wu, wd, layer, gpost, gnext, tm):
    n, d = x.shape
    dff = wg.shape[2]
    tf = FFN_TF
    return pl.pallas_call(
        _ffn_kernel,
        grid=(n // tm, dff // tf),
        in_specs=[pl.BlockSpec((tm, d), lambda i, j: (i, 0)),
                  pl.BlockSpec((tm, d), lambda i, j: (i, 0)),
                  pl.BlockSpec((None, d, tf), lambda i, j: (layer, 0, j)),
                  pl.BlockSpec((None, d, tf), lambda i, j: (layer, 0, j)),
                  pl.BlockSpec((None, tf, d), lambda i, j: (layer, j, 0)),
                  pl.BlockSpec((1, d), lambda i, j: (0, 0)),
                  pl.BlockSpec((1, d), lambda i, j: (0, 0))],
        out_specs=[pl.BlockSpec((tm, d), lambda i, j: (i, 0)),
                   pl.BlockSpec((tm, d), lambda i, j: (i, 0))],
        out_shape=[jax.ShapeDtypeStruct((n, d), F32), jax.ShapeDtypeStruct((n, d), BF16)],
        scratch_shapes=[pltpu.VMEM((tm, d), F32)],
        compiler_params=_compiler_params(("arbitrary", "arbitrary")),
    )(h, x, wg, wu, wd, gpost, gnext)


class _GlrMasks:
    def __init__(self, c):
        self.c = c
        row_a = _iota((c, c), 0)
        col_a = _iota((c, c), 1)
        self.tril = _tril_ones(c)
        self.level = {}
        h = 1
        while 2 * h <= c:
            self.level[h] = (((row_a // (2 * h)) == (col_a // (2 * h)))
                             & ((row_a // h) % 2 == 1) & ((col_a // h) % 2 == 0))
            h *= 2
        self.eye = row_a == col_a


def _small_level_factors(g):
    c = g.shape[0]
    r4 = _iota(g.shape, 0) % 4
    g_prev = pltpu.roll(g, 1, 0)
    g_next = pltpu.roll(g, c - 1, 0)
    e1 = jnp.exp2(jnp.where(r4 % 2 == 1, g, 0.0))
    e2 = jnp.exp2(jnp.where(r4 == 3, g + g_prev, jnp.where(r4 == 2, g, jnp.where(r4 == 0, g_next, 0.0))))
    return e1, e2


def _glr_chunk(q, k, v, b, e1, e2, st, mk):
    c, dk = q.shape
    att = jnp.where(mk.eye, _dot_nt(q, k), 0.0)
    for h, pair_mask in mk.level.items():
        if h == 1:
            e = e1
        elif h == 2:
            e = e2
        else:
            n = c // (2 * h)
            ref = b.reshape(n, 2 * h, dk)[:, h - 1:h, :]
            refb = jnp.broadcast_to(ref, (n, 2 * h, dk)).reshape(c, dk)
            e = jnp.exp2(-jnp.abs(b - refb))
        att = jnp.where(pair_mask, _dot_nt(q * e, k * e), att)
    o = _dot(att, v) + _dot_nt(q * jnp.exp2(b), st)
    b_end = b[c - 1:c, :]
    st_new = st * jnp.exp2(b_end) + _dot_tn(v, k * jnp.exp2(b_end - b))
    return o, st_new


def _ahead_groups(cols, slots):
    tiles = -(-cols // (2 * LANE))
    bounds = [min(cols, 2 * LANE * ((k * tiles) // slots)) for k in range(slots)] + [cols]
    return [slice(bounds[k], bounds[k + 1]) for k in range(slots)]


MAX_WHOLE_CALL_ROWS = 1024


def _input_row_specs(b, t, rows, d, nsub):
    if nsub == 1:
        assert b * t <= MAX_WHOLE_CALL_ROWS, (b, t)
        whole = pl.BlockSpec((b * t, d), lambda i, j: (0, 0))
        return whole, whole
    nc = t // rows
    nblk = b * nc
    return (pl.BlockSpec((rows, d), lambda i, j: (i * nc + j, 0)),
            pl.BlockSpec((rows, d), lambda i, j: (jnp.minimum(i * nc + j + 1, nblk - 1), 0)))


def _chunk_columns(scr, row_start, c):
    rows = pl.ds(row_start, c)
    return lambda col, width: scr[rows, col:col + width]


def _first_projection(hc_ref, w_ref, pa_scr, c, nsub, first_rows):
    i, ci = pl.program_id(0), pl.program_id(1)
    if nsub > 1:
        @pl.when(ci == 0)
        def _():
            pa_scr[...] = jnp.dot(hc_ref[0:first_rows, :], w_ref[...], preferred_element_type=F32)
        return 0

    @pl.when(jnp.logical_and(i == 0, ci == 0))
    def _():
        pa_scr[...] = jnp.dot(hc_ref[...], w_ref[...], preferred_element_type=F32)
    return pl.multiple_of((i * pl.num_programs(1) + ci) * c, c)


def _gla_kernel(hc_ref, hn_ref, w_ref, s0_ref, gup_ref, gbias_ref, gnorm_ref, o_ref, sout_ref,
                s_scr, pa_scr, pb_scr, *, c, nsub):
    ci = pl.program_id(1)
    half = max(nsub // 2, 1)
    hrows = c * half
    groups = _ahead_groups(GLA_COLS_P, GLA_HEADS * half)

    @pl.when(ci == 0)
    def _():
        s_scr[...] = s0_ref[0]

    row0 = _first_projection(hc_ref, w_ref, pa_scr, c, nsub, hrows)

    mk = _GlrMasks(c)
    gnorm = gnorm_ref[...]
    for i in range(nsub):
        rs = slice(i * c, (i + 1) * c)
        src = pa_scr if i < half else pb_scr
        take = _chunk_columns(src if nsub > 1 else pa_scr, (i % half) * c if nsub > 1 else row0, c)
        v0, g0 = 2 * GLA_QW, 2 * GLA_QW + GLA_VW
        q_all = take(0, GLA_QW) * (GLA_DK ** -0.5)
        k_all = take(GLA_QW, GLA_QW)
        gdown = take(2 * GLA_QW + 2 * GLA_VW, GLA_RANKP)
        g_all = (_log_sigmoid(_dot(gdown, gup_ref[...]) + gbias_ref[...])
                 * (LOG2_E / GLA_GATE_NORMALIZER))
        b_all = _dot_exact_lhs(mk.tril, g_all)
        e1_all, e2_all = _small_level_factors(g_all)
        for h in range(GLA_HEADS):
            ks = slice(h * GLA_DKP, (h + 1) * GLA_DKP)
            vs = slice(h * GLA_DVP, (h + 1) * GLA_DVP)
            o, st = _glr_chunk(q_all[:, ks], k_all[:, ks], take(v0 + vs.start, GLA_DVP), b_all[:, ks],
                               e1_all[:, ks], e2_all[:, ks], s_scr[h], mk)
            s_scr[h] = st
            ms = jnp.sum(o * o, axis=-1, keepdims=True) * (1.0 / GLA_DV)
            o = o * lax.rsqrt(ms + EPS) * gnorm * _silu(take(g0 + vs.start, GLA_DVP))
            o_ref[rs, vs] = o.astype(BF16)
            if nsub > 1:
                g = groups[(i % half) * GLA_HEADS + h]
                ahead, dst = (hc_ref[hrows:, :], pb_scr) if i < half else (hn_ref[0:hrows, :], pa_scr)
                dst[:, g] = jnp.dot(ahead, w_ref[:, g], preferred_element_type=F32)

    @pl.when(ci == pl.num_programs(1) - 1)
    def _():
        sout_ref[0] = s_scr[...]


def _gla_mixer(hx, w, s0t, gup, gbias, gnorm, b, t, c):
    nsub = 4 if t % (4 * c) == 0 else 1
    rows = c * nsub
    hrows = c * max(nsub // 2, 1)
    nc = t // rows
    kern = functools.partial(_gla_kernel, c=c, nsub=nsub)
    full = lambda a: pl.BlockSpec(a.shape, lambda i, j: (0,) * a.ndim)
    st_spec = pl.BlockSpec((1, GLA_HEADS, GLA_DVP, GLA_DKP), lambda i, j: (i, 0, 0, 0))
    d = hx.shape[1]
    return pl.pallas_call(
        kern,
        grid=(b, nc),
        in_specs=list(_input_row_specs(b, t, rows, d, nsub)) + [
                  pl.BlockSpec(w.shape, lambda i, j: (0, 0), pipeline_mode=pl.Buffered(1)),
                  st_spec, full(gup), full(gbias), full(gnorm)],
        out_specs=[pl.BlockSpec((rows, GLA_VW), lambda i, j: (i * nc + j, 0)), st_spec],
        out_shape=[jax.ShapeDtypeStruct((b * t, GLA_VW), BF16),
                   jax.ShapeDtypeStruct((b, GLA_HEADS, GLA_DVP, GLA_DKP), F32)],
        scratch_shapes=[pltpu.VMEM((GLA_HEADS, GLA_DVP, GLA_DKP), F32),
                        pltpu.VMEM((hrows if nsub > 1 else b * t, GLA_COLS_P), F32),
                        pltpu.VMEM((hrows, GLA_COLS_P), F32)],
        compiler_params=_compiler_params(("arbitrary", "arbitrary")),
    )(hx, hx, w, s0t, gup, gbias, gnorm)


def _hgrn_kernel(hc_ref, hn_ref, w_ref, s0_ref, lbl_ref, hnorm_ref, o_ref, sout_ref,
                 s_scr, pa_scr, pb_scr, *, c, nsub, layer):
    ci = pl.program_id(1)
    half = max(nsub // 2, 1)
    hrows = c * half
    gw = HGRN_COLS // (HGRN_HEADS * half)

    @pl.when(ci == 0)
    def _():
        s_scr[...] = s0_ref[0]

    row0 = _first_projection(hc_ref, w_ref, pa_scr, c, nsub, hrows)

    lg = lbl_ref[...]
    e = jnp.exp(lg - jnp.max(lg, axis=0, keepdims=True))
    prob = e / jnp.sum(e, axis=0, keepdims=True)
    lb = jnp.zeros((1, HGRN_WIDTH), F32)
    for i in range(1, layer + 1):
        lb = lb + prob[i:i + 1, :]

    mk = _GlrMasks(c)
    hnorm = hnorm_ref[...]

    for i in range(nsub):
        rs = slice(i * c, (i + 1) * c)
        src = pa_scr if i < half else pb_scr
        take = _chunk_columns(src if nsub > 1 else pa_scr, (i % half) * c if nsub > 1 else row0, c)
        hq = take(0, HGRN_WIDTH)
        hf = take(HGRN_WIDTH, HGRN_WIDTH)
        q_all = _silu(hq)
        f_all = lb + (1.0 - lb) * _sigmoid(hf)
        k_all = 1.0 - f_all
        g_all = jnp.log2(f_all)
        b_all = _dot_exact_lhs(mk.tril, g_all)
        e1_all, e2_all = _small_level_factors(g_all)
        for h in range(HGRN_HEADS):
            s = slice(h * HGRN_D, (h + 1) * HGRN_D)
            o, st = _glr_chunk(q_all[:, s], k_all[:, s], take(2 * HGRN_WIDTH + s.start, HGRN_D), b_all[:, s],
                               e1_all[:, s], e2_all[:, s], s_scr[h], mk)
            s_scr[h] = st
            o = _rms_rows(o, hnorm) * _silu(take(3 * HGRN_WIDTH + s.start, HGRN_D))
            o_ref[rs, s] = o.astype(BF16)
            if nsub > 1:
                k = (i % half) * HGRN_HEADS + h
                g = slice(k * gw, (k + 1) * gw)
                ahead, dst = (hc_ref[hrows:, :], pb_scr) if i < half else (hn_ref[0:hrows, :], pa_scr)
                dst[:, g] = jnp.dot(ahead, w_ref[:, g], preferred_element_type=F32)

    @pl.when(ci == pl.num_programs(1) - 1)
    def _():
        sout_ref[0] = s_scr[...]


def _hgrn_mixer(hx, w, s0t, lb_logits, hnorm, b, t, c, layer):
    nsub = 4 if t % (4 * c) == 0 else 1
    rows = c * nsub
    hrows = c * max(nsub // 2, 1)
    nc = t // rows
    kern = functools.partial(_hgrn_kernel, c=c, nsub=nsub, layer=layer)
    full = lambda a: pl.BlockSpec(a.shape, lambda i, j: (0,) * a.ndim)
    st_spec = pl.BlockSpec((1, HGRN_HEADS, HGRN_D, HGRN_D), lambda i, j: (i, 0, 0, 0))
    d = hx.shape[1]
    return pl.pallas_call(
        kern,
        grid=(b, nc),
        in_specs=list(_input_row_specs(b, t, rows, d, nsub)) + [
                  pl.BlockSpec(w.shape, lambda i, j: (0, 0), pipeline_mode=pl.Buffered(1)),
                  st_spec, full(lb_logits), full(hnorm)],
        out_specs=[pl.BlockSpec((rows, HGRN_WIDTH), lambda i, j: (i * nc + j, 0)), st_spec],
        out_shape=[jax.ShapeDtypeStruct((b * t, HGRN_WIDTH), BF16),
                   jax.ShapeDtypeStruct((b, HGRN_HEADS, HGRN_D, HGRN_D), F32)],
        scratch_shapes=[pltpu.VMEM((HGRN_HEADS, HGRN_D, HGRN_D), F32),
                        pltpu.VMEM((hrows if nsub > 1 else b * t, HGRN_COLS), F32),
                        pltpu.VMEM((hrows, HGRN_COLS), F32)],
        compiler_params=_compiler_params(("arbitrary", "arbitrary")),
    )(hx, hx, w, s0t, lb_logits, hnorm)


def _rwkv_kernel(hc_ref, hn_ref, w_ref, shift0_ref, s0_ref, mu_ref, w0_ref, wup_ref, a0_ref,
                 aup_ref, gup_ref, kk_ref, ka_ref, rk_ref, lnw_ref, lnb_ref,
                 o_ref, shift_ref, sout_ref, s_scr, carry_scr, pa_scr, pb_scr, *, c, nsub):
    ci = pl.program_id(1)
    w = RWKV_WIDTH
    rows = c * nsub
    grp = min(nsub, 2)
    grows = c * grp

    @pl.when(ci == 0)
    def _():
        s_scr[...] = s0_ref[0]
        carry_scr[...] = shift0_ref[0]

    row0 = _first_projection(hc_ref, w_ref, pa_scr, c, nsub, grows)

    first_prev = carry_scr[...]
    grp_count = rows // grows
    col_tiles = _ahead_groups(RWKV_COLS, -(-RWKV_COLS // (2 * LANE)))

    c2 = 2 * c
    lane = _iota((c, LANE), 1)
    lo_half = lane < RWKV_HD
    ri = _iota((c2, c2), 0)
    cj = _iota((c2, c2), 1)
    strict = (ri % c) > (cj % c)
    incl = (ri % c) >= (cj % c)
    eye = jnp.where(ri == cj, 1.0, 0.0)
    inv_levels = []
    blk = 1
    while blk < c:
        inv_levels.append(((ri // (2 * blk)) == (cj // (2 * blk)))
                          & ((ri // blk) % 2 == 1) & ((cj // blk) % 2 == 0))
        blk *= 2
    tr, tc = _iota((grows, grows), 0), _iota((grows, grows), 1)
    tril = jnp.where((tr >= tc) & (tr // c == tc // c), 1.0, 0.0).astype(BF16)
    row = _iota((grows, RWKV_COLS), 0)
    seg = jnp.where((_iota((LANE, LANE), 0) // RWKV_HD) == (_iota((LANE, LANE), 1) // RWKV_HD),
                    1.0, 0.0).astype(BF16)

    def stack(x):
        return jnp.concatenate([jnp.where(lo_half, x, 0.0), jnp.where(lo_half, 0.0, x)], axis=0)

    def head_sum(x):
        return jnp.concatenate([_dot(x[:, j * LANE:(j + 1) * LANE], seg)
                                for j in range(RWKV_PAIRS)], axis=1)

    def prepare(gi):
        g0 = gi * grows
        src = pb_scr if gi else pa_scr
        p = src[...] if nsub > 1 else pa_scr[pl.ds(row0, grows), :]
        before = pa_scr[grows - 1:grows, :] if gi else first_prev
        if nsub == 1:
            todo = iter(())
        elif gi == 0:
            todo = iter([(hc_ref, slice(grows, 2 * grows), pb_scr, g) for g in col_tiles])
        else:
            todo = iter([(hn_ref, slice(0, grows), pa_scr, g) for g in col_tiles])

        def project_one():
            t = next(todo, None)
            if t is not None:
                h_ref, h_rows, dst, g = t
                dst[:, g] = jnp.dot(h_ref[h_rows, :], w_ref[:, g], preferred_element_type=F32)

        prev = jnp.where(row == 0, before, pltpu.roll(p, 1, 0))
        pm = p + (prev - p) * mu_ref[...]
        project_one()
        r_all = pm[:, 0:w]
        k_all = pm[:, w:2 * w]
        v_all = pm[:, 2 * w:3 * w]
        wa = pm[:, 3 * w:3 * w + LANE]
        gd = pm[:, 3 * w + LANE:]
        lw_all = -jnp.exp(_log_sigmoid(w0_ref[...] + _dot(jnp.tanh(wa), wup_ref[...])) - 0.5)
        a_all = _sigmoid(a0_ref[...] + _dot(wa, aup_ref[...]))
        project_one()
        gate_all = _dot(_sigmoid(gd), gup_ref[...])
        kk_all = k_all * kk_ref[...]
        kk_all = kk_all * lax.rsqrt(jnp.maximum(head_sum(kk_all * kk_all), 1e-24))
        project_one()
        k_all = k_all * (1.0 + (a_all - 1.0) * ka_ref[...])
        beta_all = kk_all * a_all
        bonus_all = head_sum(r_all * k_all * rk_ref[...]) * v_all
        project_one()
        cin_all = _dot_exact_lhs(tril, lw_all, parts=2)
        ginv = jnp.exp(-cin_all)
        kap_all = kk_all * jnp.exp(cin_all - lw_all)
        project_one()
        rt_all = r_all * jnp.exp(cin_all)
        kt_all = k_all * ginv
        bt_all = beta_all * ginv
        out = []
        for i in range(grp):
            rs = slice(i * c, (i + 1) * c)
            cend = cin_all[(i + 1) * c - 1:(i + 1) * c, :]
            gend = jnp.exp(cend - cin_all[rs])
            kend, bend, dec = k_all[rs] * gend, beta_all[rs] * gend, jnp.exp(cend)
            for j in range(RWKV_PAIRS):
                s = slice(j * LANE, (j + 1) * LANE)
                out.append(dict(
                    s=s, rs=slice(g0 + i * c, g0 + (i + 1) * c),
                    kap_s=stack(kap_all[rs, s]), r_s=stack(rt_all[rs, s]), k_s=stack(kt_all[rs, s]),
                    b_s=stack(bt_all[rs, s]), v_s=stack(v_all[rs, s]), kend_s=stack(kend[:, s]),
                    bend_s=stack(bend[:, s]), dec=dec[:, s], bonus=bonus_all[rs, s],
                    gate=gate_all[rs, s]))
                if j % 2 == 0:
                    project_one()
        for _ in col_tiles:
            project_one()
        if gi == grp_count - 1:
            last = p[grows - 1:grows, :]
            carry_scr[...] = last
            shift_ref[0] = last
        return out

    def dot_nt2(lhs, r1, r2):
        if r1.shape[0] % LANE:
            return _dot_nt(lhs, r1), _dot_nt(lhs, r2)
        both = _dot_nt(lhs, jnp.concatenate([r1, r2], axis=0))
        return both[:, :r1.shape[0]], both[:, r1.shape[0]:]

    def dot2(lhs, r1, r2):
        both = _dot(lhs, jnp.concatenate([r1, r2], axis=1))
        return both[:, :r1.shape[1]], both[:, r1.shape[1]:]

    pairs = []
    for gi in range(grp_count):
        group = prepare(gi)
        for pr in group:
            a1, a2 = dot_nt2(pr['kap_s'], pr['k_s'], pr['b_s'])
            pr['a1'], pr['a2'] = jnp.where(strict, a1, 0.0), jnp.where(strict, a2, 0.0)
        for pr in group:
            a3, a4 = dot_nt2(pr['r_s'], pr['k_s'], pr['b_s'])
            pr['a3'], pr['a4'] = jnp.where(incl, a3, 0.0), jnp.where(incl, a4, 0.0)
        for pr in group:
            pr['a1v'] = _dot(pr['a1'], pr['v_s'])
            pr['a3v'] = _dot(pr['a3'], pr['v_s'])
            pr['kv'] = _dot_tn(pr['v_s'], pr['kend_s'])
        pairs += group
    if c2 % LANE == 0 and len(pairs) % 2 == 0:
        left = _iota((c2, 2 * c2), 1) < c2
        tile2 = lambda x: jnp.concatenate([x, x], axis=1)

        def block_diag(x, keep=None):
            top = left if keep is None else keep & left
            bot = ~left if keep is None else keep & ~left
            x = x.astype(BF16)
            return jnp.concatenate([jnp.where(top, x, 0.0), jnp.where(bot, x, 0.0)], axis=0)

        a2w = [jnp.concatenate([pa['a2'], pb['a2']], axis=1) for pa, pb in zip(pairs[0::2], pairs[1::2])]
        tw = [tile2(eye) - jnp.where(tile2(inv_levels[0]), a, 0.0) for a in a2w]
        for m in inv_levels[1:]:
            mw = tile2(m)
            half = [_dot(t, block_diag(a, mw)) for t, a in zip(tw, a2w)]
            tw = [t - _dot(hf, block_diag(t)) for t, hf in zip(tw, half)]
        tinv = [x for t in tw for x in (t[:, :c2], t[:, c2:])]
    else:
        tinv = [eye - jnp.where(inv_levels[0], pr['a2'], 0.0) for pr in pairs]
        for m in inv_levels[1:]:
            half = [_dot(t, jnp.where(m, pr['a2'], 0.0)) for t, pr in zip(tinv, pairs)]
            tinv = [t - _dot(hf, t) for t, hf in zip(tinv, half)]
    for pr, tj in zip(pairs, tinv):
        pr['tk'], pr['tv'] = dot2(tj, pr['kap_s'], pr['a1v'])
    for pr in pairs:
        a4tk, a4tv = dot2(pr['a4'], pr['tk'], pr['tv'])
        pr['reff'] = pr['r_s'] - a4tk
        pr['oc'] = pr['a3v'] - a4tv
        pr['m'] = _dot_tn(pr['tk'], pr['bend_s'])
        pr['q'] = pr['kv'] - _dot_tn(pr['tv'], pr['bend_s'])
    sts = [s_scr[j] for j in range(RWKV_PAIRS)]
    os_ = []
    for i in range(nsub):
        sub = list(zip(pairs[i * RWKV_PAIRS:(i + 1) * RWKV_PAIRS], sts))
        o2s = [_dot_nt(pr['reff'], st) + pr['oc'] for pr, st in sub]
        sts = [st * pr['dec'] - _dot(st, pr['m']) + pr['q'] for pr, st in sub]
        os_ += [o2[0:c] + o2[c:c2] for o2 in o2s]
    for j, st in enumerate(sts):
        s_scr[j] = st
    ds = [o - _dot(o, seg) * (1.0 / RWKV_HD) for o in os_]
    vars_ = [_dot(d * d, seg) * (1.0 / RWKV_HD) for d in ds]
    for pr, d, var in zip(pairs, ds, vars_):
        s, rs = pr['s'], pr['rs']
        on = d * lax.rsqrt(var + RWKV_GN_EPS) * lnw_ref[:, s] + lnb_ref[:, s]
        o_ref[rs, s] = ((on + pr['bonus']) * pr['gate']).astype(BF16)

    @pl.when(ci == pl.num_programs(1) - 1)
    def _():
        sout_ref[0] = s_scr[...]


def _rwkv_mixer(hx, w, shift0, s0, wts, b, t, c, nsub):
    rows = c * nsub
    grows = c * min(nsub, 2)
    nc = t // rows
    d = hx.shape[1]
    kern = functools.partial(_rwkv_kernel, c=c, nsub=nsub)
    full = lambda a: pl.BlockSpec(a.shape, lambda i, j: (0,) * a.ndim)
    st_spec = pl.BlockSpec((1, RWKV_PAIRS, LANE, LANE), lambda i, j: (i, 0, 0, 0))
    sh_spec = pl.BlockSpec((1, 1, RWKV_COLS), lambda i, j: (i, 0, 0))
    return pl.pallas_call(
        kern,
        grid=(b, nc),
        in_specs=list(_input_row_specs(b, t, rows, d, nsub)) + [
                  pl.BlockSpec(w.shape, lambda i, j: (0, 0), pipeline_mode=pl.Buffered(1)),
                  sh_spec, st_spec] + [full(a) for a in wts],
        out_specs=[pl.BlockSpec((rows, RWKV_WIDTH), lambda i, j: (i * nc + j, 0)), sh_spec, st_spec],
        out_shape=[jax.ShapeDtypeStruct((b * t, RWKV_WIDTH), BF16),
                   jax.ShapeDtypeStruct((b, 1, RWKV_COLS), F32),
                   jax.ShapeDtypeStruct((b, RWKV_PAIRS, LANE, LANE), F32)],
        scratch_shapes=[pltpu.VMEM((RWKV_PAIRS, LANE, LANE), F32), pltpu.VMEM((1, RWKV_COLS), F32),
                        pltpu.VMEM((grows if nsub > 1 else b * t, RWKV_COLS), F32),
                        pltpu.VMEM((grows, RWKV_COLS), F32)],
        compiler_params=_compiler_params(("arbitrary", "arbitrary")),
    )(hx, hx, w, shift0, s0, *wts)


def _rwkv_cols_in(a):
    return jnp.concatenate([a[..., 0:640], a[..., 704:1984], a[..., 640:704], a[..., 1984:2176]], axis=-1)


def _rwkv_cols_out(a):
    return jnp.concatenate([a[..., 0:640], a[..., 1920:1984], a[..., 640:1920], a[..., 1984:2176]], axis=-1)


def _pad_heads(w, heads, d, dp):
    lead = w.shape[:-1]
    w = w.reshape(lead + (heads, d))
    w = jnp.pad(w, [(0, 0)] * len(lead) + [(0, 0), (0, dp - d)])
    return w.reshape(lead + (heads * dp,))


def _prep_weights(w_in, gla_gate_up, gla_gate_bias, gla_norm, rwkv_mu, rwkv_w_up, rwkv_a_up,
                  rwkv_r_k, w_out):
    wb = w_in.astype(BF16)
    zeros = lambda n: jnp.zeros(wb.shape[:-1] + (n,), BF16)

    def padded_heads(start, d, dp):
        return [piece for h in range(GLA_HEADS)
                for piece in (wb[..., start + h * d:start + (h + 1) * d], zeros(dp - d))]

    w_gla = jnp.concatenate(
        padded_heads(0, GLA_DK, GLA_DKP) + padded_heads(384, GLA_DK, GLA_DKP)
        + padded_heads(768, GLA_DV, GLA_DVP) + padded_heads(1552, GLA_DV, GLA_DVP)
        + [wb[..., 1536:1552], zeros(GLA_RANKP - GLA_RANK)], axis=-1)
    w_rwkv = _rwkv_cols_in(wb[..., 2320:2320 + RWKV_COLS])
    w_hgrn = wb[..., 2320 + RWKV_COLS:]
    gup = jnp.pad(_pad_heads(gla_gate_up, GLA_HEADS, GLA_DK, GLA_DKP),
                  ((0, 0), (0, GLA_RANKP - GLA_RANK), (0, 0))).astype(BF16)
    gbias = _pad_heads(gla_gate_bias, GLA_HEADS, GLA_DK, GLA_DKP)[:, None, :]
    gnorm = jnp.pad(gla_norm, ((0, 0), (0, GLA_DVP - GLA_DV)))[:, None, :]
    mu = _rwkv_cols_in(rwkv_mu)[:, None, :]
    wup = jnp.pad(rwkv_w_up, ((0, 0), (0, 64), (0, 0))).astype(BF16)
    aup = jnp.pad(rwkv_a_up, ((0, 0), (64, 0), (0, 0))).astype(BF16)
    nl = w_in.shape[0]
    rk = rwkv_r_k.reshape(nl, 1, RWKV_WIDTH)
    wo_a = w_out[:, :768].reshape(nl, GLA_HEADS, GLA_DV, D_MODEL)
    wo_a = jnp.pad(wo_a, ((0, 0), (0, 0), (0, GLA_DVP - GLA_DV), (0, 0)))
    wo_a = wo_a.reshape(nl, GLA_VW, D_MODEL).astype(BF16)
    wo_b = w_out[:, 768:1408].astype(BF16)
    wo_c = w_out[:, 1408:].astype(BF16)
    return w_gla, w_rwkv, w_hgrn, gup, gbias, gnorm, mu, wup, aup, rk, wo_a, wo_b, wo_c


def _build_weights(norm_mix_pre, norm_mix_post, norm_ffn_pre, norm_ffn_post, w_in, gla_gate_up,
                   gla_gate_bias, gla_norm, rwkv_mu, rwkv_w0, rwkv_w_up, rwkv_a0, rwkv_a_up,
                   rwkv_g_up, rwkv_k_k, rwkv_k_a, rwkv_r_k, rwkv_ln_w, rwkv_ln_b,
                   hgrn_lb_logits, hgrn_norm, w_out, ffn_w_gate, ffn_w_up, ffn_w_down):
    names = ('w_gla', 'w_rwkv', 'w_hgrn', 'gup', 'gbias', 'gnorm', 'mu', 'wup', 'aup', 'rk',
             'wo_a', 'wo_b', 'wo_c')
    one = lambda a, l: a[l:l + 1]
    per_layer = [_prep_weights(one(w_in, l), one(gla_gate_up, l), one(gla_gate_bias, l),
                               one(gla_norm, l), one(rwkv_mu, l), one(rwkv_w_up, l),
                               one(rwkv_a_up, l), one(rwkv_r_k, l), one(w_out, l))
                 for l in range(DEPTH)]
    wt = {name: [per_layer[l][i][0] for l in range(DEPTH)] for i, name in enumerate(names)}
    vec = lambda a: a[:, None, :]
    wt.update(
        norm_mix_pre=vec(norm_mix_pre), norm_mix_post=vec(norm_mix_post),
        norm_ffn_pre=vec(norm_ffn_pre), norm_ffn_post=vec(norm_ffn_post),
        w0=vec(rwkv_w0), a0=vec(rwkv_a0), g_up=rwkv_g_up.astype(BF16),
        k_k=vec(rwkv_k_k), k_a=vec(rwkv_k_a), ln_w=vec(rwkv_ln_w), ln_b=vec(rwkv_ln_b),
        lb_logits=hgrn_lb_logits, hnorm=vec(hgrn_norm),
        ffn_w_gate=ffn_w_gate.astype(BF16), ffn_w_up=ffn_w_up.astype(BF16),
        ffn_w_down=ffn_w_down.astype(BF16))
    return wt


def _gla_state_in(s):
    s = jnp.swapaxes(s, -1, -2)
    return jnp.pad(s, [(0, 0)] * 3 + [(0, GLA_DVP - GLA_DV), (0, GLA_DKP - GLA_DK)])


def _gla_state_out(s):
    return jnp.swapaxes(s[..., :GLA_DV, :GLA_DK], -1, -2)


def _rwkv_state_in(s):
    d, b = s.shape[:2]
    s = s.reshape(d, b, RWKV_PAIRS, 2, RWKV_HD, RWKV_HD)
    z = jnp.zeros_like(s[:, :, :, 0])
    top = jnp.concatenate([s[:, :, :, 0], z], axis=-1)
    bot = jnp.concatenate([z, s[:, :, :, 1]], axis=-1)
    return jnp.concatenate([top, bot], axis=-2)


def _rwkv_state_out(s):
    d, b = s.shape[:2]
    out = jnp.stack([s[..., :RWKV_HD, :RWKV_HD], s[..., RWKV_HD:, RWKV_HD:]], axis=3)
    return out.reshape(d, b, RWKV_HEADS, RWKV_HD, RWKV_HD)


def _row_tile(n, cap):
    t = cap
    while n % t:
        t //= 2
    return t


def _trunk(x, s_gla, s_rwkv, s_shift, s_hgrn, wt, b, t):
    n = b * t
    tm = _row_tile(n, 512)
    c_glr = min(128, t)
    c_rwkv = min(64, t)
    n_rwkv = 4 if t % (4 * c_rwkv) == 0 else 1
    new_gla, new_rwkv, new_shift, new_hgrn = [], [], [], []
    h = _norm(x, wt['norm_mix_pre'][0], tm)
    for l in range(DEPTH):
        oa, g1 = _gla_mixer(h, wt['w_gla'][l], s_gla[l], wt['gup'][l], wt['gbias'][l],
                            wt['gnorm'][l], b, t, c_glr)
        rw = [wt[k][l] for k in ('mu', 'w0', 'wup', 'a0', 'aup', 'g_up', 'k_k', 'k_a', 'rk',
                                 'ln_w', 'ln_b')]
        ob, sh1, r1 = _rwkv_mixer(h, wt['w_rwkv'][l], s_shift[l], s_rwkv[l], rw, b, t, c_rwkv, n_rwkv)
        oc, h1 = _hgrn_mixer(h, wt['w_hgrn'][l], s_hgrn[l], wt['lb_logits'], wt['hnorm'][l],
                             b, t, c_glr, l)
        x, h = _out_proj(oa, ob, oc, wt['wo_a'][l], wt['wo_b'][l], wt['wo_c'][l], x,
                         wt['norm_mix_post'][l], wt['norm_ffn_pre'][l], tm)
        x, h = _ffn(h, x, wt['ffn_w_gate'], wt['ffn_w_up'], wt['ffn_w_down'], l,
                    wt['norm_ffn_post'][l], wt['norm_mix_pre'][(l + 1) % DEPTH], tm)
        new_gla.append(g1)
        new_rwkv.append(r1)
        new_shift.append(sh1)
        new_hgrn.append(h1)
    return x, jnp.stack(new_gla), jnp.stack(new_rwkv), jnp.stack(new_shift), jnp.stack(new_hgrn)


def _run_path(x, state_gla, state_rwkv, state_shift, state_hgrn, wt):
    b, t, d = x.shape
    y, g, r, sh, h = _trunk(
        x.reshape(b * t, d), _gla_state_in(state_gla), _rwkv_state_in(state_rwkv),
        _rwkv_cols_in(state_shift)[:, :, None, :], jnp.swapaxes(state_hgrn, -1, -2), wt, b, t)
    return (y.reshape(b, t, d), _gla_state_out(g), _rwkv_state_out(r),
            _rwkv_cols_out(sh[:, :, 0, :]), jnp.swapaxes(h, -1, -2))


def kernel(x_prompt, x_sample, state_gla, state_rwkv, state_rwkv_shift, state_hgrn, norm_mix_pre, norm_mix_post, norm_ffn_pre, norm_ffn_post, w_in, gla_gate_up, gla_gate_bias, gla_norm, rwkv_mu, rwkv_w0, rwkv_w_up, rwkv_a0, rwkv_a_up, rwkv_g_up, rwkv_k_k, rwkv_k_a, rwkv_r_k, rwkv_ln_w, rwkv_ln_b, hgrn_lb_logits, hgrn_norm, w_out, ffn_w_gate, ffn_w_up, ffn_w_down):
    wt = _build_weights(norm_mix_pre, norm_mix_post, norm_ffn_pre, norm_ffn_post, w_in, gla_gate_up,
                        gla_gate_bias, gla_norm, rwkv_mu, rwkv_w0, rwkv_w_up, rwkv_a0, rwkv_a_up,
                        rwkv_g_up, rwkv_k_k, rwkv_k_a, rwkv_r_k, rwkv_ln_w, rwkv_ln_b,
                        hgrn_lb_logits, hgrn_norm, w_out, ffn_w_gate, ffn_w_up, ffn_w_down)
    bp = x_prompt.shape[0]
    zeros = lambda s: jnp.zeros((DEPTH, bp) + s.shape[2:], x_prompt.dtype)
    y_p, gla_p, rwkv_p, shift_p, hgrn_p = _run_path(
        x_prompt, zeros(state_gla), zeros(state_rwkv), zeros(state_rwkv_shift), zeros(state_hgrn), wt)
    y_s, gla_s, rwkv_s, shift_s, hgrn_s = _run_path(
        x_sample, state_gla, state_rwkv, state_rwkv_shift, state_hgrn, wt)
    return (y_p, y_s, gla_p, rwkv_p, shift_p, hgrn_p, gla_s, rwkv_s, shift_s, hgrn_s)
```
